```python
import math
import jax
import jax.numpy as jnp
from jax import lax
import numpy as np

D_MODEL = 1024
BATCH = 8
SEQ = 8192
DEPTH = 2

GRID_W = 64
CTX_LEN = 256
EPS = 1e-6
MLA_HEADS = 8
MLA_Q_RANK = 256
MLA_KV_RANK = 256
MLA_NOPE = 64
MLA_ROPE = 32
MLA_V = 64
MLA_WIDTH = MLA_HEADS * MLA_V
MLA_SCALE = (MLA_NOPE + MLA_ROPE) ** -0.5
ROPE_FREQS = MLA_ROPE // 4
ROPE_BASE = 10000.0
Q_BLOCK = 128
SSM_WIDTH = 512
SSM_GROUP = 16
SSM_GROUPS = SSM_WIDTH // SSM_GROUP
SSM_STATE = 64
DT_MIN = 0.001
DT_MAX = 0.1
GLA_HEADS = 4
GLA_DK = 64
GLA_DV = 128
GLA_WIDTH = GLA_HEADS * GLA_DV
GLA_GATE_RANK = 16
GLA_GATE_TAU = 16.0
GLA_CHUNK = 64
N_BRANCH = 3
FFN_HIDDEN = -(-(8 * D_MODEL) // (3 * 256)) * 256
IN_SPLITS = (MLA_Q_RANK, MLA_KV_RANK, MLA_ROPE, SSM_WIDTH, GLA_HEADS * GLA_DK, GLA_HEADS * GLA_DK, GLA_WIDTH, GLA_WIDTH, GLA_GATE_RANK, GLA_GATE_RANK, N_BRANCH * D_MODEL)
IN_WIDTH = sum(IN_SPLITS)

kernel_name = 'hybrid_mla_s5_gla_prefix_dit'


def rms_norm(x, g):
    x32 = x.astype(jnp.float32)
    y = x32 * lax.rsqrt(jnp.mean(x32 * x32, axis=-1, keepdims=True) + EPS)
    return (y * g.astype(jnp.float32)).astype(x.dtype)


def modulate(x, g, shift, scale):
    return rms_norm(x, g) * (1 + scale) + shift


def split_cols(z, sizes):
    cuts, acc = [], 0
    for s in sizes[:-1]:
        acc += s
        cuts.append(acc)
    return jnp.split(z, cuts, axis=-1)


def axial_angles(rows_n):
    rows = jnp.repeat(jnp.arange(rows_n, dtype=jnp.float32), GRID_W)
    cols = jnp.tile(jnp.arange(GRID_W, dtype=jnp.float32), rows_n)
    inv = ROPE_BASE ** (-jnp.arange(ROPE_FREQS, dtype=jnp.float32) / ROPE_FREQS)
    ang = jnp.concatenate([rows[:, None] * inv, cols[:, None] * inv], axis=-1)
    return jnp.cos(ang), jnp.sin(ang)


def apply_rope(x, cos, sin):
    xp = x.reshape(x.shape[:-1] + (x.shape[-1] // 2, 2))
    x0, x1 = xp[..., 0], xp[..., 1]
    cos = cos.astype(x.dtype)
    sin = sin.astype(x.dtype)
    return jnp.stack([x0 * cos - x1 * sin, x0 * sin + x1 * cos], axis=-1).reshape(x.shape)


def mla_queries(cq, p):
    B, T, _ = cq.shape
    q = (rms_norm(cq, p['mla_q_norm']) @ p['mla_w_uq']).reshape(B, T, MLA_HEADS, MLA_NOPE + MLA_ROPE)
    return q[..., :MLA_NOPE], q[..., MLA_NOPE:]


def mla_keys(ckv, p):
    B, T, _ = ckv.shape
    kv = (rms_norm(ckv, p['mla_kv_norm']) @ p['mla_w_ukv']).reshape(B, T, MLA_HEADS, MLA_NOPE + MLA_V)
    return kv[..., :MLA_NOPE], kv[..., MLA_NOPE:]


def mla_attend(qn, qr, kn, kr, v):
    s = jnp.einsum('bqhd,bkhd->bhqk', qn, kn) + jnp.einsum('bqhr,bkr->bhqk', qr, kr)
    prob = jax.nn.softmax(s.astype(jnp.float32) * MLA_SCALE, axis=-1).astype(v.dtype)
    return jnp.einsum('bhqk,bkhv->bqhv', prob, v)


def mla_branch(zl, zc, p, cos, sin, with_ctx):
    cq, ckv, kr = zl
    cqc, ckvc, krc = zc
    B, T, _ = cq.shape
    qn, qr = mla_queries(cq, p)
    qr = apply_rope(qr, cos[:, None, :], sin[:, None, :])
    kn, v = mla_keys(ckv, p)
    kr = apply_rope(kr, cos, sin)
    knc, vc = mla_keys(ckvc, p)
    kn_all = jnp.concatenate([knc, kn], axis=1)
    kr_all = jnp.concatenate([krc, kr], axis=1)
    v_all = jnp.concatenate([vc, v], axis=1)
    nb = T // Q_BLOCK

    def blocks(t):
        return jnp.moveaxis(t.reshape((B, nb, Q_BLOCK) + t.shape[2:]), 1, 0)

    o = lax.map(lambda qb: mla_attend(qb[0], qb[1], kn_all, kr_all, v_all), (blocks(qn), blocks(qr)))
    y = jnp.moveaxis(o, 0, 1).reshape(B, T, MLA_WIDTH) @ p['mla_w_o']
    if not with_ctx:
        return y, None
    qnc, qrc = mla_queries(cqc, p)
    oc = mla_attend(qnc, qrc, knc, krc, vc).reshape(B, cqc.shape[1], MLA_WIDTH)
    return y, oc @ p['mla_w_o']


def s5_discretise(lam_re, lam_im, log_dt, b_re, b_im):
    lr = lam_re.astype(jnp.float32)
    li = lam_im.astype(jnp.float32)
    dt = jnp.exp(log_dt.astype(jnp.float32))[:, None]
    mag = jnp.exp(dt * lr)
    ar = mag * jnp.cos(dt * li)
    ai = mag * jnp.sin(dt * li)
    den = lr * lr + li * li
    fr = ((ar - 1) * lr + ai * li) / den
    fi = (ai * lr - (ar - 1) * li) / den
    br = b_re.astype(jnp.float32)
    bi = b_im.astype(jnp.float32)
    bbr = fr[..., None] * br - fi[..., None] * bi
    bbi = fr[..., None] * bi + fi[..., None] * br
    return ar, ai, bbr, bbi


def complex_affine_combine(e1, e2):
    a1r, a1i, b1r, b1i = e1
    a2r, a2i, b2r, b2i = e2
    return (a2r * a1r - a2i * a1i, a2r * a1i + a2i * a1r,
            a2r * b1r - a2i * b1i + b2r, a2r * b1i + a2i * b1r + b2i)


def s5_scan(u, ar, ai, bbr, bbi, h0r, h0i, reverse):
    T = u.shape[1]
    xr = jnp.einsum('btgh,gph->btgp', u, bbr)
    xi = jnp.einsum('btgh,gph->btgp', u, bbi)
    if h0r is not None:
        edge = T - 1 if reverse else 0
        xr = xr.at[:, edge].add(ar * h0r - ai * h0i)
        xi = xi.at[:, edge].add(ar * h0i + ai * h0r)
    a_r = jnp.broadcast_to(ar, (1, T) + ar.shape)
    a_i = jnp.broadcast_to(ai, (1, T) + ai.shape)
    _, _, sr, si = lax.associative_scan(complex_affine_combine, (a_r, a_i, xr, xi), reverse=reverse, axis=1)
    return sr, si


def s5_readout(sr, si, c_re, c_im):
    return jnp.einsum('btgp,ghp->btgh', sr, c_re) - jnp.einsum('btgp,ghp->btgh', si, c_im)


def s5_out(y, p):
    y = jax.nn.gelu(y)
    y = y * jax.nn.sigmoid(y @ p['ssm_w_glu'] + p['ssm_b_glu'])
    return y @ p['ssm_w_o']


def s5_branch(ul, uc, p, with_ctx):
    B, T, _ = ul.shape
    C = uc.shape[1]
    u = ul.astype(jnp.float32).reshape(B, T, SSM_GROUPS, SSM_GROUP)
    w = uc.astype(jnp.float32).reshape(B, C, SSM_GROUPS, SSM_GROUP)
    d_skip = p['ssm_d'].astype(jnp.float32).reshape(SSM_GROUPS, SSM_GROUP)
    c_re = p['ssm_c_re'].astype(jnp.float32)
    c_im = p['ssm_c_im'].astype(jnp.float32)
    y = d_skip * u
    yc = d_skip * w if with_ctx else None
    for d in range(2):
        reverse = d == 1
        ar, ai, bbr, bbi = s5_discretise(p['ssm_lam_re'][d], p['ssm_lam_im'][d], p['ssm_log_dt'][d], p['ssm_b_re'], p['ssm_b_im'])
        cr, ci = s5_scan(w, ar, ai, bbr, bbi, None, None, reverse)
        edge = 0 if reverse else C - 1
        sr, si = s5_scan(u, ar, ai, bbr, bbi, cr[:, edge], ci[:, edge], reverse)
        y = y + s5_readout(sr, si, c_re, c_im)
        if with_ctx:
            yc = yc + s5_readout(cr, ci, c_re, c_im)
    y = s5_out(y.reshape(B, T, SSM_WIDTH).astype(ul.dtype), p)
    if not with_ctx:
        return y, None
    return y, s5_out(yc.reshape(B, C, SSM_WIDTH).astype(uc.dtype), p)


def gla_chunked(q, k, v, log_a, s0):
    B, T, H, DK = k.shape
    DV = v.shape[-1]
    n = T // GLA_CHUNK

    def chunks(t):
        return t.reshape(B, n, GLA_CHUNK, H, t.shape[-1])

    k, v, log_a = chunks(k), chunks(v), chunks(log_a)
    b = jnp.cumsum(log_a, axis=2)
    b_last = b[:, :, -1]
    kv = jnp.einsum('bnlhd,bnlhv->bnhdv', k * jnp.exp(b_last[:, :, None] - b), v)

    def step(s, inp):
        dec, kv_c = inp
        return dec[..., None] * s + kv_c, s

    s_fin, s_prev = lax.scan(step, s0, (jnp.moveaxis(jnp.exp(b_last), 1, 0), jnp.moveaxis(kv, 1, 0)))
    if q is None:
        return None, s_fin
    q = chunks(q)
    s_prev = jnp.moveaxis(s_prev, 0, 1)
    q_in = q * jnp.exp(b)
    k_in = k * jnp.exp(-b)
    mask = jnp.tril(jnp.ones((GLA_CHUNK, GLA_CHUNK), dtype=bool))
    att = jnp.where(mask, jnp.einsum('bnlhd,bnmhd->bnhlm', q_in, k_in), 0.0)
    o = jnp.einsum('bnhlm,bnmhv->bnlhv', att, v) + jnp.einsum('bnlhd,bnhdv->bnlhv', q_in, s_prev)
    return o.reshape(B, T, H, DV), s_fin


def gla_out(o, r, p):
    B, T = o.shape[0], o.shape[1]
    o = o * lax.rsqrt(jnp.mean(o * o, axis=-1, keepdims=True) + EPS)
    o = o.reshape(B, T, GLA_WIDTH) * p['gla_norm'].astype(jnp.float32)
    return (o.astype(r.dtype) * jax.nn.silu(r)) @ p['gla_w_o']


def gla_branch(zl, zc, p, with_ctx):
    ql, kl, vl, rl, afl, abl = zl
    qc, kc, vc, rc, afc, abc = zc
    B = ql.shape[0]

    def heads(t):
        return t.astype(jnp.float32).reshape(t.shape[0], t.shape[1], GLA_HEADS, -1)

    def log_gate(a_lr, d):
        z = a_lr.astype(jnp.float32) @ p['gla_w_a2'][d].astype(jnp.float32) + p['gla_b_a2'][d].astype(jnp.float32)
        return heads(jax.nn.log_sigmoid(z) / GLA_GATE_TAU)

    def flip(t):
        return None if t is None else jnp.flip(t, axis=1)

    qscale = GLA_DK ** -0.5
    Ql, Kl, Vl = heads(ql) * qscale, heads(kl), heads(vl)
    Kc, Vc = heads(kc), heads(vc)
    Qc = heads(qc) * qscale if with_ctx else None
    s0 = jnp.zeros((B, GLA_HEADS, GLA_DK, GLA_DV), jnp.float32)
    oc_f, s_f = gla_chunked(Qc, Kc, Vc, log_gate(afc, 0), s0)
    o_f, _ = gla_chunked(Ql, Kl, Vl, log_gate(afl, 0), s_f)
    oc_b, s_b = gla_chunked(flip(Qc), flip(Kc), flip(Vc), flip(log_gate(abc, 1)), s0)
    o_b, _ = gla_chunked(flip(Ql), flip(Kl), flip(Vl), flip(log_gate(abl, 1)), s_b)
    y = gla_out(o_f + flip(o_b), rl, p)
    if not with_ctx:
        return y, None
    return y, gla_out(oc_f + flip(oc_b), rc, p)


def merge_branches(gates, ya, yb, yc, p):
    ga, gb, gc = jnp.split(jax.nn.sigmoid(gates), N_BRANCH, axis=-1)
    return (ga * ya + gb * yb + gc * yc) @ p['w_out']


def mixer(h, hc, p, cos, sin, with_ctx):
    z = split_cols(h @ p['w_in'], IN_SPLITS)
    zc = split_cols(hc @ p['w_in'], IN_SPLITS)
    ya, yac = mla_branch(z[0:3], zc[0:3], p, cos, sin, with_ctx)
    yb, ybc = s5_branch(z[3], zc[3], p, with_ctx)
    yg, ygc = gla_branch(z[4:10], zc[4:10], p, with_ctx)
    y = merge_branches(z[10], ya, yb, yg, p)
    if not with_ctx:
        return y, None
    return y, merge_branches(zc[10], yac, ybc, ygc, p)


def swiglu(h, p):
    g, u = jnp.split(h @ p['ffn_w_up'], 2, axis=-1)
    return (jax.nn.silu(g) * u) @ p['ffn_w_down']


def trunk_layer(x, xc, c, c_ctx, p, cos, sin, with_ctx):
    mod = jax.nn.silu(c) @ p['w_mod'] + p['b_mod']
    sh1, sc1, g1, sh2, sc2, g2 = jnp.split(mod[:, None, :], 6, axis=-1)
    modc = jax.nn.silu(c_ctx) @ p['w_mod'] + p['b_mod']
    shc1, scc1, gc1, shc2, scc2, gc2 = jnp.split(modc, 6, axis=-1)
    h = modulate(x, p['norm1'], sh1, sc1)
    hc = modulate(xc, p['norm1'], shc1, scc1)
    y, yc = mixer(h, hc, p, cos, sin, with_ctx)
    x = x + g1 * y
    x = x + g2 * swiglu(modulate(x, p['norm2'], sh2, sc2), p)
    if with_ctx:
        xc = xc + gc1 * yc
        xc = xc + gc2 * swiglu(modulate(xc, p['norm2'], shc2, scc2), p)
    return x, xc


def setup_inputs(seed: int = 0) -> dict:
    key = jax.random.key(seed)
    ks = iter(jax.random.split(key, 48))
    f32 = jnp.float32
    L, D, G, P, H16 = DEPTH, D_MODEL, SSM_GROUPS, SSM_STATE, SSM_GROUP

    def nrm(shape, scale):
        return scale * jax.random.normal(next(ks), shape, f32)

    def gain(shape):
        return 1.0 + 0.05 * jax.random.normal(next(ks), shape, f32)

    return {
        'x': nrm((BATCH, SEQ, D), 1.0),
        'c': nrm((BATCH, D), 1.0),
        'ctx': nrm((BATCH, CTX_LEN, D), 1.0),
        'c_ctx': nrm((D,), 1.0),
        'w_mod': nrm((L, D, 6 * D), 0.5 * D ** -0.5),
        'b_mod': nrm((L, 6 * D), 0.02),
        'norm1': gain((L, D)),
        'norm2': gain((L, D)),
        'w_in': nrm((L, D, IN_WIDTH), D ** -0.5),
        'mla_q_norm': gain((L, MLA_Q_RANK)),
        'mla_w_uq': nrm((L, MLA_Q_RANK, MLA_HEADS * (MLA_NOPE + MLA_ROPE)), MLA_Q_RANK ** -0.5),
        'mla_kv_norm': gain((L, MLA_KV_RANK)),
        'mla_w_ukv': nrm((L, MLA_KV_RANK, MLA_HEADS * (MLA_NOPE + MLA_V)), MLA_KV_RANK ** -0.5),
        'mla_w_o': nrm((L, MLA_WIDTH, D), MLA_WIDTH ** -0.5),
        'ssm_lam_re': -0.5 + 0.01 * jax.random.normal(next(ks), (L, 2, G, P), f32),
        'ssm_lam_im': math.pi * jnp.arange(P, dtype=f32) + 0.01 * jax.random.normal(next(ks), (L, 2, G, P), f32),
        'ssm_log_dt': jax.random.uniform(next(ks), (L, 2, G), f32, math.log(DT_MIN), math.log(DT_MAX)),
        'ssm_b_re': nrm((L, G, P, H16), (2 * H16) ** -0.5),
        'ssm_b_im': nrm((L, G, P, H16), (2 * H16) ** -0.5),
        'ssm_c_re': nrm((L, G, H16, P), P ** -0.5),
        'ssm_c_im': nrm((L, G, H16, P), P ** -0.5),
        'ssm_d': nrm((L, SSM_WIDTH), 0.5),
        'ssm_w_glu': nrm((L, SSM_WIDTH, SSM_WIDTH), SSM_WIDTH ** -0.5),
        'ssm_b_glu': nrm((L, SSM_WIDTH), 0.02),
        'ssm_w_o': nrm((L, SSM_WIDTH, D), SSM_WIDTH ** -0.5),
        'gla_w_a2': nrm((L, 2, GLA_GATE_RANK, GLA_HEADS * GLA_DK), 0.5 * GLA_GATE_RANK ** -0.5),
        'gla_b_a2': nrm((L, 2, GLA_HEADS * GLA_DK), 0.1),
        'gla_norm': gain((L, GLA_WIDTH)),
        'gla_w_o': nrm((L, GLA_WIDTH, D), GLA_WIDTH ** -0.5),
        'w_out': nrm((L, D, D), D ** -0.5),
        'ffn_w_up': nrm((L, D, 2 * FFN_HIDDEN), D ** -0.5),
        'ffn_w_down': nrm((L, FFN_HIDDEN, D), FFN_HIDDEN ** -0.5),
        'final_norm': gain((D,)),
    }


def reference(x, c, ctx, c_ctx, w_mod, b_mod, norm1, norm2, w_in, mla_q_norm, mla_w_uq, mla_kv_norm, mla_w_ukv, mla_w_o, ssm_lam_re, ssm_lam_im, ssm_log_dt, ssm_b_re, ssm_b_im, ssm_c_re, ssm_c_im, ssm_d, ssm_w_glu, ssm_b_glu, ssm_w_o, gla_w_a2, gla_b_a2, gla_norm, gla_w_o, w_out, ffn_w_up, ffn_w_down, final_norm):
    ROWS = x.shape[1] // GRID_W
    cos, sin = axial_angles(ROWS)
    xc = ctx
    for l in range(DEPTH):
        p = dict(w_mod=w_mod[l], b_mod=b_mod[l], norm1=norm1[l], norm2=norm2[l], w_in=w_in[l],
                 mla_q_norm=mla_q_norm[l], mla_w_uq=mla_w_uq[l], mla_kv_norm=mla_kv_norm[l],
                 mla_w_ukv=mla_w_ukv[l], mla_w_o=mla_w_o[l],
                 ssm_lam_re=ssm_lam_re[l], ssm_lam_im=ssm_lam_im[l], ssm_log_dt=ssm_log_dt[l],
                 ssm_b_re=ssm_b_re[l], ssm_b_im=ssm_b_im[l], ssm_c_re=ssm_c_re[l], ssm_c_im=ssm_c_im[l],
                 ssm_d=ssm_d[l], ssm_w_glu=ssm_w_glu[l], ssm_b_glu=ssm_b_glu[l], ssm_w_o=ssm_w_o[l],
                 gla_w_a2=gla_w_a2[l], gla_b_a2=gla_b_a2[l], gla_norm=gla_norm[l], gla_w_o=gla_w_o[l],
                 w_out=w_out[l], ffn_w_up=ffn_w_up[l], ffn_w_down=ffn_w_down[l])
        x, xc = trunk_layer(x, xc, c, c_ctx, p, cos, sin, l < DEPTH - 1)
    return rms_norm(x, final_norm)
```

```python
import functools
import math

import jax
import jax.numpy as jnp
from jax import lax
from jax.experimental import pallas as pl
from jax.experimental.pallas import tpu as pltpu

F32 = jnp.float32
BF16 = jnp.bfloat16

D_MODEL = 1024
GRID_W = 64
EPS = 1e-6
MLA_HEADS = 8
MLA_RANK = 256
MLA_NOPE = 64
MLA_ROPE = 32
MLA_V = 64
MLA_WIDTH = MLA_HEADS * MLA_V
MLA_SCALE = (MLA_NOPE + MLA_ROPE) ** -0.5
ROPE_FREQS = MLA_ROPE // 4
ROPE_BASE = 10000.0
HEAD_SLOT = 128
SSM_WIDTH = 512
SSM_GROUP = 16
SSM_GROUPS = SSM_WIDTH // SSM_GROUP
SSM_STATE = 64
SSM_N = SSM_GROUPS * SSM_STATE
GLA_HEADS = 4
GLA_DK = 64
GLA_DV = 128
GLA_KW = GLA_HEADS * GLA_DK
GLA_WIDTH = GLA_HEADS * GLA_DV
GLA_GATE_RANK = 16
GLA_GATE_TAU = 16.0
GLA_CHUNK = 64
FFN_HIDDEN = 2816
LR_PAD = 128

VMEM_LIMIT = 56 * 1024 * 1024

TM_IN = 512
TM_MERGE = 256
TM_FFN = 512
TQ = 512
TK = 512
S5_TL = 32
S5_ROWS_OUT = 2048
GLA_BLOCK = 256


def _const_spec(shape):
    nd = len(shape)
    return pl.BlockSpec(shape, lambda *_: (0,) * nd, pipeline_mode=pl.Buffered(1))


def _params(sem):
    return pltpu.CompilerParams(dimension_semantics=sem, vmem_limit_bytes=VMEM_LIMIT)


def _rms(x):
    return x * lax.rsqrt(jnp.mean(x * x, axis=-1, keepdims=True) + EPS)


def _dot(a, b):
    return jnp.dot(a, b, preferred_element_type=F32)


def _dot_nt(a, b):
    return lax.dot_general(a, b, (((1,), (1,)), ((), ())), preferred_element_type=F32)


def _dot_tn(a, b):
    return lax.dot_general(a, b, (((0,), (0,)), ((), ())), preferred_element_type=F32)


def _mod_kernel(c_ref, w_ref, b_ref, o_ref):
    c = c_ref[...]
    s = (c * jax.nn.sigmoid(c)).astype(BF16)
    o_ref[0] = _dot(s, w_ref[0]) + b_ref[0]


def _mod_call(cvec, w_mod, b_mod):
    L, D, N = w_mod.shape
    R = cvec.shape[0]
    tn = 1536
    return pl.pallas_call(
        _mod_kernel,
        out_shape=jax.ShapeDtypeStruct((L, R, N), F32),
        grid=(L, N // tn),
        in_specs=[pl.BlockSpec((R, D), lambda l, j: (0, 0)),
                  pl.BlockSpec((1, D, tn), lambda l, j: (l, 0, j)),
                  pl.BlockSpec((1, 1, tn), lambda l, j: (l, 0, j))],
        out_specs=pl.BlockSpec((1, R, tn), lambda l, j: (l, 0, j)),
        compiler_params=_params(("arbitrary", "arbitrary")),
        name="mod",
    )(cvec, w_mod, b_mod.reshape(L, 1, N))


def _in_proj_kernel(x_ref, sh_ref, sc_ref, n1_ref, cos_ref, sin_ref,
                    wa_ref, wkr_ref, wu_ref, wg_ref, wlr_ref,
                    qn_ref, kvn_ref, wuq_ref, wuqs_ref, wukv_ref, wa2_ref, ba2_ref,
                    q_ref, k_ref, kv_ref, u_ref, gq_ref, gk_ref, gv_ref, lgf_ref, lgb_ref):
    x = x_ref[0]
    h = _rms(x) * n1_ref[...]
    h = h * (1.0 + sc_ref[0]) + sh_ref[0]
    hb = h.astype(BF16)
    cos = cos_ref[...]
    sin = sin_ref[...]
    cos8 = jnp.concatenate([cos] * MLA_HEADS, axis=1)
    sin8 = jnp.concatenate([sin] * MLA_HEADS, axis=1)

    za = _dot(hb, wa_ref[...])
    cqn = (_rms(za[:, :MLA_RANK]) * qn_ref[...]).astype(BF16)
    q = _dot(cqn, wuq_ref[...]) * cos8 + _dot(cqn, wuqs_ref[...]) * sin8
    q_ref[0] = (q * MLA_SCALE).astype(BF16)

    ckvn = (_rms(za[:, MLA_RANK:]) * kvn_ref[...]).astype(BF16)
    kv = _dot(ckvn, wukv_ref[...])
    kv_ref[0] = kv.astype(BF16)
    kr2 = _dot(hb, wkr_ref[...])
    krr = kr2[:, :HEAD_SLOT] * cos + kr2[:, HEAD_SLOT:] * sin
    lane = lax.broadcasted_iota(jnp.int32, krr.shape, 1)
    for hd in range(MLA_HEADS):
        sl = slice(hd * HEAD_SLOT, (hd + 1) * HEAD_SLOT)
        k_ref[0, :, sl] = jnp.where(lane < MLA_NOPE, kv[:, sl], krr).astype(BF16)

    u_ref[0] = _dot(hb, wu_ref[...])
    zg = _dot(hb, wg_ref[...])
    gq_ref[0] = zg[:, :GLA_KW]
    gk_ref[0] = zg[:, GLA_KW:2 * GLA_KW]
    gv_ref[0] = zg[:, 2 * GLA_KW:]
    zlr = _dot(hb, wlr_ref[...]).astype(BF16)
    for d, o_ref in enumerate((lgf_ref, lgb_ref)):
        z = _dot(zlr, wa2_ref[d]) + ba2_ref[d]
        o_ref[0] = (jnp.minimum(z, 0.0) - jnp.log1p(jnp.exp(-jnp.abs(z)))) * (1.0 / GLA_GATE_TAU)


def _in_proj_call(x, mod_sh, mod_sc, mod_row, n1, cos_t, sin_t, w, tm):
    Bn, Tn, D = x.shape
    nt = Tn // tm
    row = (lambda b, i: (b, 0, 0)) if mod_row is None else (lambda b, i: (mod_row, 0, 0))
    tok = lambda width: pl.BlockSpec((1, tm, width), lambda b, i: (b, i, 0))
    weights = [w["wa"], w["wkr"], w["wu"], w["wg"], w["wlr"], w["qn"], w["kvn"],
               w["wuq"], w["wuqs"], w["wukv"], w["wa2"], w["ba2"]]
    outs = [(D, BF16), (D, BF16), (D, BF16), (SSM_WIDTH, F32), (GLA_KW, F32), (GLA_KW, F32),
            (GLA_WIDTH, F32), (GLA_KW, F32), (GLA_KW, F32)]
    return pl.pallas_call(
        _in_proj_kernel,
        out_shape=[jax.ShapeDtypeStruct((Bn, Tn, wd), dt) for wd, dt in outs],
        grid=(Bn, nt),
        in_specs=[tok(D),
                  pl.BlockSpec((1, 1, D), row), pl.BlockSpec((1, 1, D), row),
                  _const_spec((1, D)),
                  pl.BlockSpec((tm, HEAD_SLOT), lambda b, i: (i, 0)),
                  pl.BlockSpec((tm, HEAD_SLOT), lambda b, i: (i, 0))]
                 + [_const_spec(a.shape) for a in weights],
        out_specs=[tok(wd) for wd, _ in outs],
        compiler_params=_params(("parallel", "parallel")),
        name="in_proj",
    )(x, mod_sh, mod_sc, n1, cos_t, sin_t, *weights)


def _attn_tile(q, kt, kvt, carry):
    m, l, acc = carry
    s = _dot_nt(q, kt)
    m_new = jnp.maximum(m, jnp.max(s, axis=-1, keepdims=True))
    alpha = jnp.exp(m - m_new)
    p = jnp.exp(s - m_new)
    l = alpha * l + jnp.sum(p, axis=-1, keepdims=True)
    acc = alpha * acc + _dot(p.astype(BF16), kvt)
    return m_new, l, acc


def _attn_kernel(*refs, n_lat_tiles, tk):
    if n_lat_tiles:
        q_ref, kc_ref, kvc_ref, kl_ref, kvl_ref, o_ref = refs
    else:
        q_ref, kc_ref, kvc_ref, o_ref = refs
    tq = q_ref.shape[1]
    outs = []
    for hh in range(2):
        sl = slice(hh * HEAD_SLOT, (hh + 1) * HEAD_SLOT)
        q = q_ref[0, :, sl]
        carry = (jnp.full((tq, 1), -jnp.inf, F32), jnp.zeros((tq, 1), F32), jnp.zeros((tq, HEAD_SLOT), F32))
        carry = _attn_tile(q, kc_ref[0, :, sl], kvc_ref[0, :, sl], carry)
        if n_lat_tiles:
            def body(i, c, sl=sl, q=q):
                off = pl.multiple_of(i * tk, tk)
                return _attn_tile(q, kl_ref[0, pl.ds(off, tk), sl], kvl_ref[0, pl.ds(off, tk), sl], c)
            carry = lax.fori_loop(0, n_lat_tiles, body, carry)
        _, l, acc = carry
        outs.append((acc / l)[:, MLA_NOPE:])
    o_ref[0] = jnp.concatenate(outs, axis=1).astype(o_ref.dtype)


def _attn_call(q, k_ctx, kv_ctx, k_lat, kv_lat, tq, tk):
    Bn, Tq, D = q.shape
    C = k_ctx.shape[1]
    hp = MLA_HEADS // 2
    w2 = 2 * HEAD_SLOT
    n_lat_tiles = 0 if k_lat is None else k_lat.shape[1] // tk
    in_specs = [pl.BlockSpec((1, tq, w2), lambda b, h, i: (b, i, h)),
                pl.BlockSpec((1, C, w2), lambda b, h, i: (b, 0, h)),
                pl.BlockSpec((1, C, w2), lambda b, h, i: (b, 0, h))]
    args = [q, k_ctx, kv_ctx]
    if n_lat_tiles:
        Tk = k_lat.shape[1]
        in_specs += [pl.BlockSpec((1, Tk, w2), lambda b, h, i: (b, 0, h)),
                     pl.BlockSpec((1, Tk, w2), lambda b, h, i: (b, 0, h))]
        args += [k_lat, kv_lat]
    return pl.pallas_call(
        functools.partial(_attn_kernel, n_lat_tiles=n_lat_tiles, tk=tk),
        out_shape=jax.ShapeDtypeStruct((Bn, Tq, MLA_WIDTH), BF16),
        grid=(Bn, hp, Tq // tq),
        in_specs=in_specs,
        out_specs=pl.BlockSpec((1, tq, 2 * MLA_V), lambda b, h, i: (b, i, h)),
        compiler_params=_params(("parallel", "parallel", "parallel")),
        name="attn",
    )(*args)


def _s5_disc_kernel(lr_ref, li_ref, ldt_ref, br_ref, bi_ref, ar_ref, ai_ref, bbr_ref, bbi_ref):
    lr = lr_ref[0]
    li = li_ref[0]
    dt = jnp.exp(ldt_ref[0])
    mag = jnp.exp(dt * lr)
    ar = mag * jnp.cos(dt * li)
    ai = mag * jnp.sin(dt * li)
    den = lr * lr + li * li
    fr = ((ar - 1.0) * lr + ai * li) / den
    fi = (ai * lr - (ar - 1.0) * li) / den
    ar_ref[0] = ar
    ai_ref[0] = ai
    br = br_ref[...]
    bi = bi_ref[...]
    bbr_ref[0] = fr * br - fi * bi
    bbi_ref[0] = fr * bi + fi * br


def _s5_disc_call(lam_re, lam_im, log_dt, b_re, b_im):
    n = SSM_N
    lr = lam_re.reshape(2, 1, n)
    li = lam_im.reshape(2, 1, n)
    ldt = jnp.repeat(log_dt, SSM_STATE, axis=1).reshape(2, 1, n)
    br = b_re.reshape(n, SSM_GROUP).T
    bi = b_im.reshape(n, SSM_GROUP).T
    vec = pl.BlockSpec((1, 1, n), lambda d: (d, 0, 0))
    mat = pl.BlockSpec((SSM_GROUP, n), lambda d: (0, 0))
    omat = pl.BlockSpec((1, SSM_GROUP, n), lambda d: (d, 0, 0))
    return pl.pallas_call(
        _s5_disc_kernel,
        out_shape=[jax.ShapeDtypeStruct((2, 1, n), F32), jax.ShapeDtypeStruct((2, 1, n), F32),
                   jax.ShapeDtypeStruct((2, SSM_GROUP, n), F32), jax.ShapeDtypeStruct((2, SSM_GROUP, n), F32)],
        grid=(2,),
        in_specs=[vec, vec, vec, mat, mat],
        out_specs=[vec, vec, omat, omat],
        compiler_params=_params(("arbitrary",)),
        name="s5_disc",
    )(lr, li, ldt, br, bi)


def _s5_scan_kernel(uf_ref, ub_ref, bd_ref, ar_ref, ai_ref, cre_ref, cim_ref, h0_ref,
                    yf_ref, yb_ref, hfin_ref, xs_ref, h_ref, *, tl, nb):
    i = pl.program_id(0)
    n = SSM_N

    @pl.when(i == 0)
    def _():
        h_ref[...] = h0_ref[...]

    for d, u_ref in enumerate((uf_ref, ub_ref)):
        u = u_ref[...].reshape(tl * nb, SSM_WIDTH).astype(BF16)
        xs_ref[d] = _dot(u, bd_ref[d])

    half = n // 2
    for d in range(2):
        for cb in range(2):
            re = slice(cb * half, (cb + 1) * half)
            im = slice(n + cb * half, n + (cb + 1) * half)
            ar = jnp.broadcast_to(ar_ref[d, :, re], (nb, half))
            ai = jnp.broadcast_to(ai_ref[d, :, re], (nb, half))

            def body(t, carry, d=d, re=re, im=im, ar=ar, ai=ai):
                hr, hi = carry
                tt = t if d == 0 else tl - 1 - t
                row = pl.multiple_of(tt * nb, nb)
                nr = ar * hr - ai * hi + xs_ref[d, pl.ds(row, nb), re]
                ni = ar * hi + ai * hr + xs_ref[d, pl.ds(row, nb), im]
                xs_ref[d, pl.ds(row, nb), re] = nr
                xs_ref[d, pl.ds(row, nb), im] = ni
                return nr, ni

            hr, hi = lax.fori_loop(0, tl, body, (h_ref[d, :, re], h_ref[d, :, im]), unroll=4)
            h_ref[d, :, re] = hr
            h_ref[d, :, im] = hi

    for d, y_ref in enumerate((yf_ref, yb_ref)):
        y = _dot(xs_ref[d, :, :n].astype(BF16), cre_ref[...]) - _dot(xs_ref[d, :, n:].astype(BF16), cim_ref[...])
        y_ref[...] = y.reshape(tl, nb, SSM_WIDTH)

    @pl.when(i == pl.num_programs(0) - 1)
    def _():
        hfin_ref[...] = h_ref[...]


def _s5_scan_call(u_tb, bd, ar, ai, cre_t, cim_t, h0, tl):
    Tn, nb, W = u_tb.shape
    nc = Tn // tl
    n = SSM_N
    fwd = pl.BlockSpec((tl, nb, W), lambda i: (i, 0, 0))
    bwd = pl.BlockSpec((tl, nb, W), lambda i: (nc - 1 - i, 0, 0))
    return pl.pallas_call(
        functools.partial(_s5_scan_kernel, tl=tl, nb=nb),
        out_shape=[jax.ShapeDtypeStruct((Tn, nb, W), F32), jax.ShapeDtypeStruct((Tn, nb, W), F32),
                   jax.ShapeDtypeStruct((2, nb, 2 * n), F32)],
        grid=(nc,),
        in_specs=[fwd, bwd, _const_spec(bd.shape), _const_spec(ar.shape), _const_spec(ai.shape),
                  _const_spec(cre_t.shape), _const_spec(cim_t.shape), _const_spec(h0.shape)],
        out_specs=[fwd, bwd, pl.BlockSpec((2, nb, 2 * n), lambda i: (0, 0, 0))],
        scratch_shapes=[pltpu.VMEM((2, tl * nb, 2 * n), F32), pltpu.VMEM((2, nb, 2 * n), F32)],
        compiler_params=_params(("arbitrary",)),
        name="s5_scan",
    )(u_tb, u_tb, bd, ar, ai, cre_t, cim_t, h0)


def _s5_out_kernel(u_ref, yf_ref, yb_ref, d_ref, wglu_ref, bglu_ref, o_ref):
    y = d_ref[...] * u_ref[...] + yf_ref[...] + yb_ref[...]
    g = jax.nn.gelu(y)
    z = _dot(g.astype(BF16), wglu_ref[...]) + bglu_ref[...]
    o_ref[...] = (g * jax.nn.sigmoid(z)).astype(o_ref.dtype)


def _s5_out_call(u, yf, yb, d_skip, wglu, bglu, tr):
    R, W = u.shape
    tr = min(tr, R)
    blk = pl.BlockSpec((tr, W), lambda i: (i, 0))
    return pl.pallas_call(
        _s5_out_kernel,
        out_shape=jax.ShapeDtypeStruct((R, W), BF16),
        grid=(R // tr,),
        in_specs=[blk, blk, blk, _const_spec(d_skip.shape), _const_spec(wglu.shape), _const_spec(bglu.shape)],
        out_specs=blk,
        compiler_params=_params(("parallel",)),
        name="s5_out",
    )(u, yf, yb, d_skip, wglu, bglu)


def _split3(x):
    hi = x.astype(BF16)
    r1 = x - hi.astype(F32)
    mid = r1.astype(BF16)
    lo = (r1 - mid.astype(F32)).astype(BF16)
    return hi, mid, lo


def _gla_chunk(q, k, v, lg, s_ref, d, reverse):
    L = GLA_CHUNK
    r_i = lax.broadcasted_iota(jnp.int32, (L, L), 0)
    c_i = lax.broadcasted_iota(jnp.int32, (L, L), 1)
    mask = (c_i >= r_i) if reverse else (c_i <= r_i)
    tri = jnp.where(mask, 1.0, 0.0).astype(BF16)
    ones = jnp.ones((L, GLA_DV), BF16)
    parts = _split3(lg)
    b = sum(_dot(tri, p) for p in parts)
    edge = 0 if reverse else L - 1
    b_last = b[edge:edge + 1, :]
    q_in = q * (GLA_DK ** -0.5) * jnp.exp(b)
    k_in = k * jnp.exp(-b)
    k_dec = k * jnp.exp(b_last - b)
    decay = jnp.exp(sum(_dot_tn(p, ones) for p in parts))
    outs = []
    for hd in range(GLA_HEADS):
        ks = slice(hd * GLA_DK, (hd + 1) * GLA_DK)
        vs = slice(hd * GLA_DV, (hd + 1) * GLA_DV)
        qh = q_in[:, ks].astype(BF16)
        vh = v[:, vs].astype(BF16)
        att = jnp.where(mask, _dot_nt(qh, k_in[:, ks].astype(BF16)), 0.0)
        s_prev = s_ref[d, ks, :]
        outs.append(_dot(att.astype(BF16), vh) + _dot(qh, s_prev.astype(BF16)))
        s_ref[d, ks, :] = decay[ks, :] * s_prev + _dot_tn(k_dec[:, ks].astype(BF16), vh)
    return jnp.concatenate(outs, axis=1)


def _gla_kernel(qf_ref, kf_ref, vf_ref, lgf_ref, qb_ref, kb_ref, vb_ref, lgb_ref, s0_ref,
                of_ref, ob_ref, sfin_ref, s_ref, *, n_chunks):
    i = pl.program_id(1)

    @pl.when(i == 0)
    def _():
        s_ref[...] = s0_ref[0]

    def body(c, carry):
        rf = pl.multiple_of(c * GLA_CHUNK, GLA_CHUNK)
        rows = pl.ds(rf, GLA_CHUNK)
        of_ref[0, rows, :] = _gla_chunk(qf_ref[0, rows, :], kf_ref[0, rows, :], vf_ref[0, rows, :],
                                        lgf_ref[0, rows, :], s_ref, 0, False)
        rb = pl.multiple_of((n_chunks - 1 - c) * GLA_CHUNK, GLA_CHUNK)
        rows = pl.ds(rb, GLA_CHUNK)
        ob_ref[0, rows, :] = _gla_chunk(qb_ref[0, rows, :], kb_ref[0, rows, :], vb_ref[0, rows, :],
                                        lgb_ref[0, rows, :], s_ref, 1, True)
        return carry

    lax.fori_loop(0, n_chunks, body, 0)

    @pl.when(i == pl.num_programs(1) - 1)
    def _():
        sfin_ref[0] = s_ref[...]


def _gla_call(q, k, v, lgf, lgb, s0, tg):
    Bn, Tn, _ = q.shape
    nblk = Tn // tg
    fwd = lambda w: pl.BlockSpec((1, tg, w), lambda b, i: (b, i, 0))
    bwd = lambda w: pl.BlockSpec((1, tg, w), lambda b, i: (b, nblk - 1 - i, 0))
    st = pl.BlockSpec((1, 2, GLA_KW, GLA_DV), lambda b, i: (b, 0, 0, 0))
    return pl.pallas_call(
        functools.partial(_gla_kernel, n_chunks=tg // GLA_CHUNK),
        out_shape=[jax.ShapeDtypeStruct((Bn, Tn, GLA_WIDTH), F32), jax.ShapeDtypeStruct((Bn, Tn, GLA_WIDTH), F32),
                   jax.ShapeDtypeStruct((Bn, 2, GLA_KW, GLA_DV), F32)],
        grid=(Bn, nblk),
        in_specs=[fwd(GLA_KW), fwd(GLA_KW), fwd(GLA_WIDTH), fwd(GLA_KW),
                  bwd(GLA_KW), bwd(GLA_KW), bwd(GLA_WIDTH), bwd(GLA_KW), st],
        out_specs=[fwd(GLA_WIDTH), bwd(GLA_WIDTH), st],
        scratch_shapes=[pltpu.VMEM((2, GLA_KW, GLA_DV), F32)],
        compiler_params=_params(("parallel", "arbitrary")),
        name="gla",
    )(q, k, v, lgf, q, k, v, lgb, s0)


def _merge_kernel(x_ref, oa_ref, ys_ref, of_ref, ob_ref, sh_ref, sc_ref, g1_ref, n1_ref,
                  wgate_ref, wr_ref, wmo_ref, wso_ref, gn_ref, wgo_ref, wout_ref, o_ref):
    x = x_ref[0]
    h = _rms(x) * n1_ref[...]
    hb = (h * (1.0 + sc_ref[0]) + sh_ref[0]).astype(BF16)
    ya = _dot(oa_ref[0], wmo_ref[...])
    yb = _dot(ys_ref[0], wso_ref[...])
    o = of_ref[0] + ob_ref[0]
    on = jnp.concatenate([_rms(o[:, hd * GLA_DV:(hd + 1) * GLA_DV]) for hd in range(GLA_HEADS)], axis=1)
    r = _dot(hb, wr_ref[...])
    yg = _dot(((on * gn_ref[...]) * (r * jax.nn.sigmoid(r))).astype(BF16), wgo_ref[...])
    m = jnp.zeros_like(ya)
    for j, br in enumerate((ya, yb, yg)):
        gate = jax.nn.sigmoid(_dot(hb, wgate_ref[:, j * D_MODEL:(j + 1) * D_MODEL]))
        m = m + gate * br
    o_ref[0] = x + g1_ref[0] * _dot(m.astype(BF16), wout_ref[...])


def _merge_call(x, oa, ys, of, ob, mod_sh, mod_sc, mod_g, mod_row, n1, w, tm):
    Bn, Tn, D = x.shape
    row = (lambda b, i: (b, 0, 0)) if mod_row is None else (lambda b, i: (mod_row, 0, 0))
    tok = lambda width: pl.BlockSpec((1, tm, width), lambda b, i: (b, i, 0))
    mrow = pl.BlockSpec((1, 1, D), row)
    weights = [w["wgate"], w["wr"], w["wmo"], w["wso"], w["gn"], w["wgo"], w["wout"]]
    return pl.pallas_call(
        _merge_kernel,
        out_shape=jax.ShapeDtypeStruct((Bn, Tn, D), F32),
        grid=(Bn, Tn // tm),
        in_specs=[tok(D), tok(MLA_WIDTH), tok(SSM_WIDTH), tok(GLA_WIDTH), tok(GLA_WIDTH),
                  mrow, mrow, mrow, _const_spec((1, D))] + [_const_spec(a.shape) for a in weights],
        out_specs=tok(D),
        compiler_params=_params(("parallel", "parallel")),
        name="merge",
    )(x, oa, ys, of, ob, mod_sh, mod_sc, mod_g, n1, *weights)


def _ffn_kernel(x_ref, sh_ref, sc_ref, g2_ref, n2_ref, wg_ref, wu_ref, wd_ref, fn_ref, o_ref, *, final):
    x = x_ref[0]
    h = _rms(x) * n2_ref[...]
    hb = (h * (1.0 + sc_ref[0]) + sh_ref[0]).astype(BF16)
    acc = jnp.zeros_like(x)
    half = FFN_HIDDEN // 2
    for j in range(2):
        cs = slice(j * half, (j + 1) * half)
        g = _dot(hb, wg_ref[:, cs])
        u = _dot(hb, wu_ref[:, cs])
        acc = acc + _dot(((g * jax.nn.sigmoid(g)) * u).astype(BF16), wd_ref[cs, :])
    y = x + g2_ref[0] * acc
    if final:
        y = _rms(y) * fn_ref[...]
    o_ref[0] = y


def _ffn_call(x, mod_sh, mod_sc, mod_g, mod_row, n2, wg, wu, wd, fn, final, tm):
    Bn, Tn, D = x.shape
    row = (lambda b, i: (b, 0, 0)) if mod_row is None else (lambda b, i: (mod_row, 0, 0))
    tok = pl.BlockSpec((1, tm, D), lambda b, i: (b, i, 0))
    mrow = pl.BlockSpec((1, 1, D), row)
    return pl.pallas_call(
        functools.partial(_ffn_kernel, final=final),
        out_shape=jax.ShapeDtypeStruct((Bn, Tn, D), F32),
        grid=(Bn, Tn // tm),
        in_specs=[tok, mrow, mrow, mrow, _const_spec((1, D)), _const_spec(wg.shape), _const_spec(wu.shape),
                  _const_spec(wd.shape), _const_spec((1, D))],
        out_specs=tok,
        compiler_params=_params(("parallel", "parallel")),
        name="ffn",
    )(x, mod_sh, mod_sc, mod_g, n2, wg, wu, wd, fn)


def _pair_swap(w):
    w2 = w.reshape(w.shape[0], -1, 2)
    return jnp.stack([-w2[..., 1], w2[..., 0]], axis=-1).reshape(w.shape)


def _layer_weights(l, w_in, mla_q_norm, mla_w_uq, mla_kv_norm, mla_w_ukv, gla_w_a2, gla_b_a2):
    wi = w_in[l]
    D = wi.shape[0]
    o = 0
    cuts = {}
    for name, sz in (("cq", MLA_RANK), ("ckv", MLA_RANK), ("kr", MLA_ROPE), ("u", SSM_WIDTH), ("gq", GLA_KW),
                     ("gk", GLA_KW), ("gv", GLA_WIDTH), ("gr", GLA_WIDTH), ("af", GLA_GATE_RANK),
                     ("ab", GLA_GATE_RANK), ("gate", 3 * D_MODEL)):
        cuts[name] = wi[:, o:o + sz]
        o += sz
    zeros = lambda n: jnp.zeros((D, n), F32)
    kr = cuts["kr"]
    pad_k = HEAD_SLOT - MLA_NOPE - MLA_ROPE
    wkr = jnp.concatenate([zeros(MLA_NOPE), kr, zeros(pad_k), zeros(MLA_NOPE), _pair_swap(kr), zeros(pad_k)], axis=1)
    wlr = jnp.concatenate([cuts["af"], cuts["ab"], zeros(LR_PAD - 2 * GLA_GATE_RANK)], axis=1)
    uq = mla_w_uq[l].reshape(MLA_RANK, MLA_HEADS, MLA_NOPE + MLA_ROPE)
    zq = jnp.zeros((MLA_RANK, MLA_HEADS, pad_k), F32)
    wuq = jnp.concatenate([uq, zq], axis=2).reshape(MLA_RANK, MLA_HEADS * HEAD_SLOT)
    uq_sw = _pair_swap(uq[..., MLA_NOPE:].reshape(MLA_RANK, -1)).reshape(MLA_RANK, MLA_HEADS, MLA_ROPE)
    wuqs = jnp.concatenate([jnp.zeros((MLA_RANK, MLA_HEADS, MLA_NOPE), F32), uq_sw, zq], axis=2)
    wuqs = wuqs.reshape(MLA_RANK, MLA_HEADS * HEAD_SLOT)
    wa2 = jnp.zeros((2, LR_PAD, GLA_KW), F32)
    wa2 = wa2.at[0, :GLA_GATE_RANK].set(gla_w_a2[l, 0]).at[1, GLA_GATE_RANK:2 * GLA_GATE_RANK].set(gla_w_a2[l, 1])
    b16 = lambda a: a.astype(BF16)
    return dict(
        wa=b16(jnp.concatenate([cuts["cq"], cuts["ckv"]], axis=1)), wkr=b16(wkr), wu=b16(cuts["u"]),
        wg=b16(jnp.concatenate([cuts["gq"], cuts["gk"], cuts["gv"]], axis=1)), wlr=b16(wlr),
        qn=mla_q_norm[l].reshape(1, -1), kvn=mla_kv_norm[l].reshape(1, -1),
        wuq=b16(wuq), wuqs=b16(wuqs), wukv=b16(mla_w_ukv[l]), wa2=b16(wa2),
        ba2=gla_b_a2[l].reshape(2, 1, GLA_KW), wr=b16(cuts["gr"]), wgate=b16(cuts["gate"]))


def _rope_tables(T):
    t = jnp.arange(T, dtype=jnp.int32)
    rows = (t // GRID_W).astype(F32)
    cols = (t % GRID_W).astype(F32)
    inv = ROPE_BASE ** (-jnp.arange(ROPE_FREQS, dtype=F32) / ROPE_FREQS)
    ang = jnp.concatenate([rows[:, None] * inv, cols[:, None] * inv], axis=-1)
    ang2 = jnp.repeat(ang, 2, axis=1)
    pad_k = HEAD_SLOT - MLA_NOPE - MLA_ROPE
    cos = jnp.concatenate([jnp.ones((T, MLA_NOPE), F32), jnp.cos(ang2), jnp.ones((T, pad_k), F32)], axis=1)
    sin = jnp.concatenate([jnp.zeros((T, MLA_NOPE), F32), jnp.sin(ang2), jnp.zeros((T, pad_k), F32)], axis=1)
    return cos, sin


def _s5_matrices(bbr, bbi, c_re, c_im):
    n = SSM_N
    grp_row = jnp.arange(SSM_WIDTH)[:, None] // SSM_GROUP
    grp_col = jnp.arange(n)[None, :] // SSM_STATE
    blk = grp_row == grp_col
    tile = lambda m: jnp.where(blk, jnp.tile(m, (SSM_GROUPS, 1)), 0.0)
    bd = jnp.stack([jnp.concatenate([tile(bbr[d]), tile(bbi[d])], axis=1) for d in range(2)]).astype(BF16)
    def readout(c):
        ct = jnp.transpose(c, (0, 2, 1)).reshape(n, SSM_GROUP)
        return jnp.where(blk.T, jnp.tile(ct, (1, SSM_GROUPS)), 0.0).astype(BF16)
    return bd, readout(c_re), readout(c_im)


def kernel(x, c, ctx, c_ctx, w_mod, b_mod, norm1, norm2, w_in, mla_q_norm, mla_w_uq, mla_kv_norm, mla_w_ukv, mla_w_o, ssm_lam_re, ssm_lam_im, ssm_log_dt, ssm_b_re, ssm_b_im, ssm_c_re, ssm_c_im, ssm_d, ssm_w_glu, ssm_b_glu, ssm_w_o, gla_w_a2, gla_b_a2, gla_norm, gla_w_o, w_out, ffn_w_up, ffn_w_down, final_norm):
    B, T, D = x.shape
    C = ctx.shape[1]
    depth = w_mod.shape[0]
    n_rows = -(-(B + 1) // 8) * 8
    cvec = jnp.zeros((n_rows, D), F32).at[:B].set(c).at[B].set(c_ctx)
    mod = _mod_call(cvec, w_mod.astype(BF16), b_mod)
    cos_l, sin_l = _rope_tables(T)
    cos_c = jnp.concatenate([jnp.ones((C, HEAD_SLOT), F32)], axis=0)
    sin_c = jnp.zeros((C, HEAD_SLOT), F32)
    tm_in, tm_merge, tm_ffn = min(TM_IN, T), min(TM_MERGE, T), min(TM_FFN, T)
    tq, tk = min(TQ, T), min(TK, T)
    tg = min(GLA_BLOCK, T)
    b16 = lambda a: a.astype(BF16)

    xc = ctx
    for l in range(depth):
        with_ctx = l < depth - 1
        m6 = [mod[l, :, j * D:(j + 1) * D].reshape(n_rows, 1, D) for j in range(6)]
        sh1, sc1, g1, sh2, sc2, g2 = m6
        n1 = norm1[l].reshape(1, D)
        n2 = norm2[l].reshape(1, D)
        w = _layer_weights(l, w_in, mla_q_norm, mla_w_uq, mla_kv_norm, mla_w_ukv, gla_w_a2, gla_b_a2)
        w.update(wmo=b16(mla_w_o[l]), wso=b16(ssm_w_o[l]), gn=gla_norm[l].reshape(1, -1), wgo=b16(gla_w_o[l]),
                 wout=b16(w_out[l]))
        up = ffn_w_up[l]
        wup_g, wup_u, wdown = b16(up[:, :FFN_HIDDEN]), b16(up[:, FFN_HIDDEN:]), b16(ffn_w_down[l])
        fn = final_norm.reshape(1, D)

        qc, kc, kvc, uc, gqc, gkc, gvc, lgfc, lgbc = _in_proj_call(xc, sh1, sc1, B, n1, cos_c, sin_c, w, C)
        ql, kl, kvl, ul, gql, gkl, gvl, lgfl, lgbl = _in_proj_call(x, sh1, sc1, None, n1, cos_l, sin_l, w, tm_in)

        oa = _attn_call(ql, kc, kvc, kl, kvl, tq, tk)
        oac = _attn_call(qc, kc, kvc, None, None, C, tk) if with_ctx else None

        ar, ai, bbr, bbi = _s5_disc_call(ssm_lam_re[l], ssm_lam_im[l], ssm_log_dt[l], ssm_b_re[l], ssm_b_im[l])
        bd, cre_t, cim_t = _s5_matrices(bbr, bbi, ssm_c_re[l], ssm_c_im[l])
        uc_tb = jnp.transpose(uc, (1, 0, 2))
        ul_tb = jnp.transpose(ul, (1, 0, 2))
        h0 = jnp.zeros((2, B, 2 * SSM_N), F32)
        yfc, ybc, hc = _s5_scan_call(uc_tb, bd, ar, ai, cre_t, cim_t, h0, S5_TL)
        yfl, ybl, _ = _s5_scan_call(ul_tb, bd, ar, ai, cre_t, cim_t, hc, S5_TL)
        d_skip = ssm_d[l].reshape(1, -1)
        wglu, bglu = b16(ssm_w_glu[l]), ssm_b_glu[l].reshape(1, -1)
        flat = lambda a: a.reshape(-1, SSM_WIDTH)
        ysl = _s5_out_call(flat(ul_tb), flat(yfl), flat(ybl), d_skip, wglu, bglu, S5_ROWS_OUT)
        ysl = jnp.transpose(ysl.reshape(T, B, SSM_WIDTH), (1, 0, 2))

        s0 = jnp.zeros((B, 2, GLA_KW, GLA_DV), F32)
        ofc, obc, sc_fin = _gla_call(gqc, gkc, gvc, lgfc, lgbc, s0, min(GLA_BLOCK, C))
        ofl, obl, _ = _gla_call(gql, gkl, gvl, lgfl, lgbl, sc_fin, tg)

        x = _merge_call(x, oa, ysl, ofl, obl, sh1, sc1, g1, None, n1, w, tm_merge)
        x = _ffn_call(x, sh2, sc2, g2, None, n2, wup_g, wup_u, wdown, fn, l == depth - 1, tm_ffn)
        if with_ctx:
            ysc = _s5_out_call(flat(uc_tb), flat(yfc), flat(ybc), d_skip, wglu, bglu, S5_ROWS_OUT)
            ysc = jnp.transpose(ysc.reshape(C, B, SSM_WIDTH), (1, 0, 2))
            xc = _merge_call(xc, oac, ysc, ofc, obc, sh1, sc1, g1, B, n1, w, C)
            xc = _ffn_call(xc, sh2, sc2, g2, B, n2, wup_g, wup_u, wdown, fn, False, C)
    return x
```

```python
import functools
import math

import jax
import jax.numpy as jnp
from jax import lax
from jax.experimental import pallas as pl
from jax.experimental.pallas import tpu as pltpu

F32 = jnp.float32
BF16 = jnp.bfloat16

D_MODEL = 1024
GRID_W = 64
EPS = 1e-6
MLA_HEADS = 8
MLA_RANK = 256
MLA_NOPE = 64
MLA_ROPE = 32
MLA_V = 64
MLA_WIDTH = MLA_HEADS * MLA_V
MLA_SCALE = (MLA_NOPE + MLA_ROPE) ** -0.5
ROPE_FREQS = MLA_ROPE // 4
ROPE_BASE = 10000.0
LOG2_E = math.log2(math.e)
HEAD_SLOT = 128
SSM_WIDTH = 512
SSM_GROUP = 16
SSM_GROUPS = SSM_WIDTH // SSM_GROUP
SSM_STATE = 64
SSM_N = SSM_GROUPS * SSM_STATE
GLA_HEADS = 4
GLA_DK = 64
GLA_DV = 128
GLA_KW = GLA_HEADS * GLA_DK
GLA_WIDTH = GLA_HEADS * GLA_DV
GLA_GATE_RANK = 16
GLA_GATE_TAU = 16.0
GLA_CHUNK = 64
FFN_HIDDEN = 2816
LR_PAD = 128

VMEM_LIMIT = 56 * 1024 * 1024

TM_IN = 512
TM_MERGE = 256
TM_FFN = 512
TQ = 512
TK = 512
ATT_STRIP = 64
S5_TL = 64
S5_ROWS_OUT = 2048
GLA_BLOCK = 256


def _const_spec(shape):
    nd = len(shape)
    return pl.BlockSpec(shape, lambda *_: (0,) * nd, pipeline_mode=pl.Buffered(1))


def _params(sem):
    return pltpu.CompilerParams(dimension_semantics=sem, vmem_limit_bytes=VMEM_LIMIT)


def _rms(x):
    return x * lax.rsqrt(jnp.mean(x * x, axis=-1, keepdims=True) + EPS)


def _dot(a, b):
    return jnp.dot(a, b, preferred_element_type=F32)


def _dot_nt(a, b):
    return lax.dot_general(a, b, (((1,), (1,)), ((), ())), preferred_element_type=F32)


def _dot_tn(a, b):
    return lax.dot_general(a, b, (((0,), (0,)), ((), ())), preferred_element_type=F32)


def _mod_kernel(c_ref, w_ref, b_ref, o_ref):
    c = c_ref[...]
    s = (c * jax.nn.sigmoid(c)).astype(BF16)
    o_ref[0] = _dot(s, w_ref[0]) + b_ref[0]


def _mod_call(cvec, w_mod, b_mod):
    L, D, N = w_mod.shape
    R = cvec.shape[0]
    tn = 1536
    return pl.pallas_call(
        _mod_kernel,
        out_shape=jax.ShapeDtypeStruct((L, R, N), F32),
        grid=(L, N // tn),
        in_specs=[pl.BlockSpec((R, D), lambda l, j: (0, 0)),
                  pl.BlockSpec((1, D, tn), lambda l, j: (l, 0, j)),
                  pl.BlockSpec((1, 1, tn), lambda l, j: (l, 0, j))],
        out_specs=pl.BlockSpec((1, R, tn), lambda l, j: (l, 0, j)),
        compiler_params=_params(("arbitrary", "arbitrary")),
        name="mod",
    )(cvec, w_mod, b_mod.reshape(L, 1, N))


def _in_proj_kernel(x_ref, sh_ref, sc_ref, n1_ref, cos_ref, sin_ref,
                    wa_ref, wkr_ref, wu_ref, wg_ref, wlr_ref,
                    qn_ref, kvn_ref, wuq_ref, wuqs_ref, wukv_ref, wa2_ref, ba2_ref,
                    q_ref, k_ref, kv_ref, u_ref, gq_ref, gk_ref, gv_ref, lgf_ref, lgb_ref):
    x = x_ref[0]
    h = _rms(x) * n1_ref[...]
    h = h * (1.0 + sc_ref[0]) + sh_ref[0]
    hb = h.astype(BF16)
    cos = cos_ref[...]
    sin = sin_ref[...]
    cos8 = jnp.concatenate([cos] * MLA_HEADS, axis=1)
    sin8 = jnp.concatenate([sin] * MLA_HEADS, axis=1)

    za = _dot(hb, wa_ref[...])
    cqn = (_rms(za[:, :MLA_RANK]) * qn_ref[...]).astype(BF16)
    q = _dot(cqn, wuq_ref[...]) * cos8 + _dot(cqn, wuqs_ref[...]) * sin8
    q_ref[0] = (q * (MLA_SCALE * LOG2_E)).astype(BF16)

    ckvn = (_rms(za[:, MLA_RANK:]) * kvn_ref[...]).astype(BF16)
    kv = _dot(ckvn, wukv_ref[...])
    kv_ref[0] = kv.astype(BF16)
    kr2 = _dot(hb, wkr_ref[...])
    krr = kr2[:, :HEAD_SLOT] * cos + kr2[:, HEAD_SLOT:] * sin
    lane = lax.broadcasted_iota(jnp.int32, krr.shape, 1)
    for hd in range(MLA_HEADS):
        sl = slice(hd * HEAD_SLOT, (hd + 1) * HEAD_SLOT)
        k_ref[0, :, sl] = jnp.where(lane < MLA_NOPE, kv[:, sl], krr).astype(BF16)

    u_ref[0] = _dot(hb, wu_ref[...])
    zg = _dot(hb, wg_ref[...])
    gq_ref[0] = zg[:, :GLA_KW]
    gk_ref[0] = zg[:, GLA_KW:2 * GLA_KW]
    gv_ref[0] = zg[:, 2 * GLA_KW:]
    zlr = _dot(hb, wlr_ref[...]).astype(BF16)
    for d, o_ref in enumerate((lgf_ref, lgb_ref)):
        z = _dot(zlr, wa2_ref[d]) + ba2_ref[d]
        o_ref[0] = (jnp.minimum(z, 0.0) - jnp.log1p(jnp.exp(-jnp.abs(z)))) * (1.0 / GLA_GATE_TAU)


def _in_proj_call(x, mod_sh, mod_sc, mod_row, n1, cos_t, sin_t, w, tm):
    Bn, Tn, D = x.shape
    nt = Tn // tm
    row = (lambda b, i: (b, 0, 0)) if mod_row is None else (lambda b, i: (mod_row, 0, 0))
    tok = lambda width: pl.BlockSpec((1, tm, width), lambda b, i: (b, i, 0))
    weights = [w["wa"], w["wkr"], w["wu"], w["wg"], w["wlr"], w["qn"], w["kvn"],
               w["wuq"], w["wuqs"], w["wukv"], w["wa2"], w["ba2"]]
    outs = [(D, BF16), (D, BF16), (D, BF16), (SSM_WIDTH, F32), (GLA_KW, F32), (GLA_KW, F32),
            (GLA_WIDTH, F32), (GLA_KW, F32), (GLA_KW, F32)]
    return pl.pallas_call(
        _in_proj_kernel,
        out_shape=[jax.ShapeDtypeStruct((Bn, Tn, wd), dt) for wd, dt in outs],
        grid=(Bn, nt),
        in_specs=[tok(D),
                  pl.BlockSpec((1, 1, D), row), pl.BlockSpec((1, 1, D), row),
                  _const_spec((1, D)),
                  pl.BlockSpec((tm, HEAD_SLOT), lambda b, i: (i, 0)),
                  pl.BlockSpec((tm, HEAD_SLOT), lambda b, i: (i, 0))]
                 + [_const_spec(a.shape) for a in weights],
        out_specs=[tok(wd) for wd, _ in outs],
        compiler_params=_params(("parallel", "parallel")),
        name="in_proj",
    )(x, mod_sh, mod_sc, n1, cos_t, sin_t, *weights)


def _lane_fold(x, op):
    out = x[:, :HEAD_SLOT]
    for j in range(1, x.shape[1] // HEAD_SLOT):
        out = op(out, x[:, j * HEAD_SLOT:(j + 1) * HEAD_SLOT])
    return out


def _attn_update(s, kvt, carry):
    m, l, acc = carry
    m_new = jnp.maximum(m, jnp.max(_lane_fold(s, jnp.maximum), axis=-1, keepdims=True))
    alpha = jnp.exp2(m - m_new)
    p = jnp.exp2(s - m_new)
    l = alpha * l + _lane_fold(p, jnp.add)
    acc = alpha * acc + _dot(p.astype(BF16), kvt)
    return m_new, l, acc


def _attn_kernel(*refs, n_lat_tiles, tk):
    if n_lat_tiles:
        q_ref, kc_ref, kvc_ref, kl_ref, kvl_ref, o_ref, s_ref, p_ref, m_ref, l_ref, acc_ref = refs
    else:
        q_ref, kc_ref, kvc_ref, o_ref = refs
    tq = q_ref.shape[1]
    sls = [slice(hh * HEAD_SLOT, (hh + 1) * HEAD_SLOT) for hh in range(2)]
    qs = [q_ref[0, :, sl] for sl in sls]
    init = (jnp.full((tq, 1), -jnp.inf, F32), jnp.zeros((tq, HEAD_SLOT), F32), jnp.zeros((tq, HEAD_SLOT), F32))
    carry = tuple(_attn_update(_dot_nt(qs[hh], kc_ref[0, :, sls[hh]]), kvc_ref[0, :, sls[hh]], init)
                  for hh in range(2))
    if n_lat_tiles:
        for hh in range(2):
            m, l, acc = carry[hh]
            m_ref[hh] = jnp.broadcast_to(m, (tq, HEAD_SLOT))
            l_ref[hh] = l
            acc_ref[hh] = acc
        rows = lambda i: pl.ds(pl.multiple_of(i * tk, tk), tk)

        def scores(i, slot):
            for hh in range(2):
                s_ref[slot, hh] = _dot_nt(qs[hh], kl_ref[0, rows(i), sls[hh]])

        def update(i, slot):
            for hh in range(2):
                for r in range(0, tq, ATT_STRIP):
                    rs = slice(r, r + ATT_STRIP)
                    s = s_ref[slot, hh, rs, :]
                    m_old = m_ref[hh, rs, :]
                    m_new = jnp.maximum(m_old, jnp.max(_lane_fold(s, jnp.maximum), axis=-1, keepdims=True))
                    alpha = jnp.exp2(m_old - m_new)
                    p = jnp.exp2(s - jnp.concatenate([m_new] * (tk // HEAD_SLOT), axis=1))
                    m_ref[hh, rs, :] = m_new
                    l_ref[hh, rs, :] = alpha * l_ref[hh, rs, :] + _lane_fold(p, jnp.add)
                    acc_ref[hh, rs, :] = alpha * acc_ref[hh, rs, :]
                    p_ref[hh, rs, :] = p.astype(BF16)
                acc_ref[hh] += _dot(p_ref[hh], kvl_ref[0, rows(i), sls[hh]])

        scores(0, 0)

        def body(j, c):
            i = 2 * j
            scores(i + 1, 1)
            update(i, 0)
            scores(jnp.minimum(i + 2, n_lat_tiles - 1), 0)
            update(i + 1, 1)
            return c

        lax.fori_loop(0, n_lat_tiles // 2, body, 0)
        carry = tuple((None, l_ref[hh], acc_ref[hh]) for hh in range(2))
    outs = [(acc / jnp.sum(l, axis=-1, keepdims=True))[:, MLA_NOPE:] for _, l, acc in carry]
    o_ref[0] = jnp.concatenate(outs, axis=1).astype(o_ref.dtype)


def _attn_call(q, k_ctx, kv_ctx, k_lat, kv_lat, tq, tk):
    Bn, Tq, D = q.shape
    C = k_ctx.shape[1]
    hp = MLA_HEADS // 2
    w2 = 2 * HEAD_SLOT
    n_lat_tiles = 0 if k_lat is None else k_lat.shape[1] // tk
    in_specs = [pl.BlockSpec((1, tq, w2), lambda b, h, i: (b, i, h)),
                pl.BlockSpec((1, C, w2), lambda b, h, i: (b, 0, h)),
                pl.BlockSpec((1, C, w2), lambda b, h, i: (b, 0, h))]
    args = [q, k_ctx, kv_ctx]
    scratch = []
    if n_lat_tiles:
        assert n_lat_tiles % 2 == 0 and tq % ATT_STRIP == 0
        Tk = k_lat.shape[1]
        in_specs += [pl.BlockSpec((1, Tk, w2), lambda b, h, i: (b, 0, h)),
                     pl.BlockSpec((1, Tk, w2), lambda b, h, i: (b, 0, h))]
        args += [k_lat, kv_lat]
        stat = pltpu.VMEM((2, tq, HEAD_SLOT), F32)
        scratch = [pltpu.VMEM((2, 2, tq, tk), F32), pltpu.VMEM((2, tq, tk), BF16), stat, stat, stat]
    return pl.pallas_call(
        functools.partial(_attn_kernel, n_lat_tiles=n_lat_tiles, tk=tk),
        out_shape=jax.ShapeDtypeStruct((Bn, Tq, MLA_WIDTH), BF16),
        grid=(Bn, hp, Tq // tq),
        in_specs=in_specs,
        out_specs=pl.BlockSpec((1, tq, 2 * MLA_V), lambda b, h, i: (b, i, h)),
        scratch_shapes=scratch,
        compiler_params=_params(("parallel", "parallel", "parallel")),
        name="attn",
    )(*args)


def _s5_disc_kernel(lr_ref, li_ref, ldt_ref, br_ref, bi_ref, ar_ref, ai_ref, bbr_ref, bbi_ref):
    lr = lr_ref[0]
    li = li_ref[0]
    dt = jnp.exp(ldt_ref[0])
    mag = jnp.exp(dt * lr)
    ar = mag * jnp.cos(dt * li)
    ai = mag * jnp.sin(dt * li)
    den = lr * lr + li * li
    fr = ((ar - 1.0) * lr + ai * li) / den
    fi = (ai * lr - (ar - 1.0) * li) / den
    ar_ref[0] = ar
    ai_ref[0] = ai
    br = br_ref[...]
    bi = bi_ref[...]
    bbr_ref[0] = fr * br - fi * bi
    bbi_ref[0] = fr * bi + fi * br


def _s5_disc_call(lam_re, lam_im, log_dt, b_re, b_im):
    n = SSM_N
    lr = lam_re.reshape(2, 1, n)
    li = lam_im.reshape(2, 1, n)
    ldt = jnp.repeat(log_dt, SSM_STATE, axis=1).reshape(2, 1, n)
    br = b_re.reshape(n, SSM_GROUP).T
    bi = b_im.reshape(n, SSM_GROUP).T
    vec = pl.BlockSpec((1, 1, n), lambda d: (d, 0, 0))
    mat = pl.BlockSpec((SSM_GROUP, n), lambda d: (0, 0))
    omat = pl.BlockSpec((1, SSM_GROUP, n), lambda d: (d, 0, 0))
    return pl.pallas_call(
        _s5_disc_kernel,
        out_shape=[jax.ShapeDtypeStruct((2, 1, n), F32), jax.ShapeDtypeStruct((2, 1, n), F32),
                   jax.ShapeDtypeStruct((2, SSM_GROUP, n), F32), jax.ShapeDtypeStruct((2, SSM_GROUP, n), F32)],
        grid=(2,),
        in_specs=[vec, vec, vec, mat, mat],
        out_specs=[vec, vec, omat, omat],
        compiler_params=_params(("arbitrary",)),
        name="s5_disc",
    )(lr, li, ldt, br, bi)


def _s5_scan_kernel(uf_ref, ub_ref, bd_ref, ar_ref, ai_ref, cre_ref, cim_ref, h0_ref,
                    yf_ref, yb_ref, hfin_ref, xs_ref, h_ref, *, tl, nb):
    i = pl.program_id(0)
    n = SSM_N

    @pl.when(i == 0)
    def _():
        h_ref[...] = h0_ref[...]

    half = n // 2
    hw = SSM_WIDTH // 2
    for d, u_ref in enumerate((uf_ref, ub_ref)):
        u = u_ref[...].reshape(tl * nb, SSM_WIDTH).astype(BF16)
        for k in range(2):
            xk = _dot(u[:, k * hw:(k + 1) * hw], bd_ref[d, k])
            xs_ref[d, :, k * half:(k + 1) * half] = xk[:, :half]
            xs_ref[d, :, n + k * half:n + (k + 1) * half] = xk[:, half:]

    for d in range(2):
        for cb in range(2):
            re = slice(cb * half, (cb + 1) * half)
            im = slice(n + cb * half, n + (cb + 1) * half)
            ar = jnp.broadcast_to(ar_ref[d, :, re], (nb, half))
            ai = jnp.broadcast_to(ai_ref[d, :, re], (nb, half))

            def body(t, carry, d=d, re=re, im=im, ar=ar, ai=ai):
                hr, hi = carry
                tt = t if d == 0 else tl - 1 - t
                row = pl.multiple_of(tt * nb, nb)
                nr = ar * hr - ai * hi + xs_ref[d, pl.ds(row, nb), re]
                ni = ar * hi + ai * hr + xs_ref[d, pl.ds(row, nb), im]
                xs_ref[d, pl.ds(row, nb), re] = nr
                xs_ref[d, pl.ds(row, nb), im] = ni
                return nr, ni

            hr, hi = lax.fori_loop(0, tl, body, (h_ref[d, :, re], h_ref[d, :, im]), unroll=4)
            h_ref[d, :, re] = hr
            h_ref[d, :, im] = hi

    for d, y_ref in enumerate((yf_ref, yb_ref)):
        ys = []
        for k in range(2):
            hr = xs_ref[d, :, k * half:(k + 1) * half].astype(BF16)
            hi = xs_ref[d, :, n + k * half:n + (k + 1) * half].astype(BF16)
            ys.append(_dot(hr, cre_ref[k]) - _dot(hi, cim_ref[k]))
        y_ref[...] = jnp.concatenate(ys, axis=1).reshape(tl, nb, SSM_WIDTH)

    @pl.when(i == pl.num_programs(0) - 1)
    def _():
        hfin_ref[...] = h_ref[...]


def _s5_scan_call(u_tb, bd, ar, ai, cre_t, cim_t, h0, tl):
    Tn, nb, W = u_tb.shape
    nc = Tn // tl
    n = SSM_N
    fwd = pl.BlockSpec((tl, nb, W), lambda i: (i, 0, 0))
    bwd = pl.BlockSpec((tl, nb, W), lambda i: (nc - 1 - i, 0, 0))
    return pl.pallas_call(
        functools.partial(_s5_scan_kernel, tl=tl, nb=nb),
        out_shape=[jax.ShapeDtypeStruct((Tn, nb, W), F32), jax.ShapeDtypeStruct((Tn, nb, W), F32),
                   jax.ShapeDtypeStruct((2, nb, 2 * n), F32)],
        grid=(nc,),
        in_specs=[fwd, bwd, _const_spec(bd.shape), _const_spec(ar.shape), _const_spec(ai.shape),
                  _const_spec(cre_t.shape), _const_spec(cim_t.shape), _const_spec(h0.shape)],
        out_specs=[fwd, bwd, pl.BlockSpec((2, nb, 2 * n), lambda i: (0, 0, 0))],
        scratch_shapes=[pltpu.VMEM((2, tl * nb, 2 * n), F32), pltpu.VMEM((2, nb, 2 * n), F32)],
        compiler_params=_params(("arbitrary",)),
        name="s5_scan",
    )(u_tb, u_tb, bd, ar, ai, cre_t, cim_t, h0)


def _s5_out_kernel(u_ref, yf_ref, yb_ref, d_ref, wglu_ref, bglu_ref, o_ref):
    y = d_ref[...] * u_ref[...] + yf_ref[...] + yb_ref[...]
    g = jax.nn.gelu(y)
    z = _dot(g.astype(BF16), wglu_ref[...]) + bglu_ref[...]
    o_ref[...] = (g * jax.nn.sigmoid(z)).astype(o_ref.dtype)


def _s5_out_call(u, yf, yb, d_skip, wglu, bglu, tr):
    R, W = u.shape
    tr = min(tr, R)
    blk = pl.BlockSpec((tr, W), lambda i: (i, 0))
    return pl.pallas_call(
        _s5_out_kernel,
        out_shape=jax.ShapeDtypeStruct((R, W), BF16),
        grid=(R // tr,),
        in_specs=[blk, blk, blk, _const_spec(d_skip.shape), _const_spec(wglu.shape), _const_spec(bglu.shape)],
        out_specs=blk,
        compiler_params=_params(("parallel",)),
        name="s5_out",
    )(u, yf, yb, d_skip, wglu, bglu)


def _split3(x):
    hi = x.astype(BF16)
    r1 = x - hi.astype(F32)
    mid = r1.astype(BF16)
    lo = (r1 - mid.astype(F32)).astype(BF16)
    return hi, mid, lo


def _gla_block(q, k, v, lg, s, d, cst, n_chunks):
    tri_ref, mask_ref, same_ref, hmask_ref, bmask_ref = cst
    L = GLA_CHUNK
    parts = _split3(lg)
    b = sum(_dot(tri_ref[d], p) for p in parts)
    tot = sum(_dot(same_ref[...], p) for p in parts)
    q_in = q * (GLA_DK ** -0.5) * jnp.exp(b)
    k_in = (k * jnp.exp(-b)).astype(BF16)
    k_dec = (k * jnp.exp(tot - b)).astype(BF16)
    vb = v.astype(BF16)
    mask = mask_ref[d] > 0.0
    outs = []
    for hd in range(GLA_HEADS):
        qh = (q_in * hmask_ref[hd]).astype(BF16)
        att = jnp.where(mask, _dot_nt(qh, k_in), 0.0).astype(BF16)
        outs.append(_dot(att, vb[:, hd * GLA_DV:(hd + 1) * GLA_DV]))
    o_intra = jnp.concatenate(outs, axis=1)
    q_in_b = q_in.astype(BF16)
    bmask = bmask_ref[...]
    o_rows = [None] * n_chunks
    for c in (range(n_chunks) if d == 0 else reversed(range(n_chunks))):
        rows = slice(c * L, (c + 1) * L)
        o_rows[c] = _dot_nt(q_in_b[rows], s.astype(BF16))
        dec = jnp.exp(tot[c * L:c * L + 1, :])
        s = dec * s + _dot_tn(vb[rows], k_dec[rows]) * bmask
    return o_intra + jnp.concatenate(o_rows, axis=0), s


def _gla_kernel(qf_ref, kf_ref, vf_ref, lgf_ref, qb_ref, kb_ref, vb_ref, lgb_ref, s0_ref,
                tri_ref, mask_ref, same_ref, hmask_ref, bmask_ref,
                of_ref, ob_ref, sfin_ref, s_ref, *, n_chunks):
    i = pl.program_id(1)
    cst = (tri_ref, mask_ref, same_ref, hmask_ref, bmask_ref)

    @pl.when(i == 0)
    def _():
        s_ref[...] = s0_ref[0]

    o, s = _gla_block(qf_ref[0], kf_ref[0], vf_ref[0], lgf_ref[0], s_ref[0], 0, cst, n_chunks)
    of_ref[0] = o
    s_ref[0] = s
    o, s = _gla_block(qb_ref[0], kb_ref[0], vb_ref[0], lgb_ref[0], s_ref[1], 1, cst, n_chunks)
    ob_ref[0] = o
    s_ref[1] = s

    @pl.when(i == pl.num_programs(1) - 1)
    def _():
        sfin_ref[0] = s_ref[...]


def _gla_consts(tg):
    t = jnp.arange(tg)
    same = (t[:, None] // GLA_CHUNK) == (t[None, :] // GLA_CHUNK)
    low = same & (t[None, :] <= t[:, None])
    upp = same & (t[None, :] >= t[:, None])
    tri = jnp.stack([low, upp])
    hmask = (jnp.arange(GLA_KW)[None, None, :] // GLA_DK) == jnp.arange(GLA_HEADS)[:, None, None]
    bmask = (jnp.arange(GLA_WIDTH)[:, None] // GLA_DV) == (jnp.arange(GLA_KW)[None, :] // GLA_DK)
    return tri.astype(BF16), tri.astype(F32), same.astype(BF16), hmask.astype(F32), bmask.astype(F32)


def _gla_call(q, k, v, lgf, lgb, s0, tg):
    Bn, Tn, _ = q.shape
    nblk = Tn // tg
    consts = _gla_consts(tg)
    fwd = lambda w: pl.BlockSpec((1, tg, w), lambda b, i: (b, i, 0))
    bwd = lambda w: pl.BlockSpec((1, tg, w), lambda b, i: (b, nblk - 1 - i, 0))
    st = pl.BlockSpec((1, 2, GLA_WIDTH, GLA_KW), lambda b, i: (b, 0, 0, 0))
    return pl.pallas_call(
        functools.partial(_gla_kernel, n_chunks=tg // GLA_CHUNK),
        out_shape=[jax.ShapeDtypeStruct((Bn, Tn, GLA_WIDTH), F32), jax.ShapeDtypeStruct((Bn, Tn, GLA_WIDTH), F32),
                   jax.ShapeDtypeStruct((Bn, 2, GLA_WIDTH, GLA_KW), F32)],
        grid=(Bn, nblk),
        in_specs=[fwd(GLA_KW), fwd(GLA_KW), fwd(GLA_WIDTH), fwd(GLA_KW),
                  bwd(GLA_KW), bwd(GLA_KW), bwd(GLA_WIDTH), bwd(GLA_KW), st]
                 + [_const_spec(a.shape) for a in consts],
        out_specs=[fwd(GLA_WIDTH), bwd(GLA_WIDTH), st],
        scratch_shapes=[pltpu.VMEM((2, GLA_WIDTH, GLA_KW), F32)],
        compiler_params=_params(("parallel", "arbitrary")),
        name="gla",
    )(q, k, v, lgf, q, k, v, lgb, s0, *consts)


def _merge_kernel(x_ref, oa_ref, ys_ref, of_ref, ob_ref, sh_ref, sc_ref, g1_ref, n1_ref,
                  wgate_ref, wr_ref, wmo_ref, wso_ref, gn_ref, wgo_ref, wout_ref, o_ref):
    x = x_ref[0]
    h = _rms(x) * n1_ref[...]
    hb = (h * (1.0 + sc_ref[0]) + sh_ref[0]).astype(BF16)
    ya = _dot(oa_ref[0], wmo_ref[...])
    yb = _dot(ys_ref[0], wso_ref[...])
    o = of_ref[0] + ob_ref[0]
    on = jnp.concatenate([_rms(o[:, hd * GLA_DV:(hd + 1) * GLA_DV]) for hd in range(GLA_HEADS)], axis=1)
    r = _dot(hb, wr_ref[...])
    yg = _dot(((on * gn_ref[...]) * (r * jax.nn.sigmoid(r))).astype(BF16), wgo_ref[...])
    m = jnp.zeros_like(ya)
    for j, br in enumerate((ya, yb, yg)):
        gate = jax.nn.sigmoid(_dot(hb, wgate_ref[:, j * D_MODEL:(j + 1) * D_MODEL]))
        m = m + gate * br
    o_ref[0] = x + g1_ref[0] * _dot(m.astype(BF16), wout_ref[...])


def _merge_call(x, oa, ys, of, ob, mod_sh, mod_sc, mod_g, mod_row, n1, w, tm):
    Bn, Tn, D = x.shape
    row = (lambda b, i: (b, 0, 0)) if mod_row is None else (lambda b, i: (mod_row, 0, 0))
    tok = lambda width: pl.BlockSpec((1, tm, width), lambda b, i: (b, i, 0))
    mrow = pl.BlockSpec((1, 1, D), row)
    weights = [w["wgate"], w["wr"], w["wmo"], w["wso"], w["gn"], w["wgo"], w["wout"]]
    return pl.pallas_call(
        _merge_kernel,
        out_shape=jax.ShapeDtypeStruct((Bn, Tn, D), F32),
        grid=(Bn, Tn // tm),
        in_specs=[tok(D), tok(MLA_WIDTH), tok(SSM_WIDTH), tok(GLA_WIDTH), tok(GLA_WIDTH),
                  mrow, mrow, mrow, _const_spec((1, D))] + [_const_spec(a.shape) for a in weights],
        out_specs=tok(D),
        compiler_params=_params(("parallel", "parallel")),
        name="merge",
    )(x, oa, ys, of, ob, mod_sh, mod_sc, mod_g, n1, *weights)


def _ffn_kernel(x_ref, sh_ref, sc_ref, g2_ref, n2_ref, wg_ref, wu_ref, wd_ref, fn_ref, o_ref, *, final):
    x = x_ref[0]
    h = _rms(x) * n2_ref[...]
    hb = (h * (1.0 + sc_ref[0]) + sh_ref[0]).astype(BF16)
    acc = jnp.zeros_like(x)
    half = FFN_HIDDEN // 2
    for j in range(2):
        cs = slice(j * half, (j + 1) * half)
        g = _dot(hb, wg_ref[:, cs])
        u = _dot(hb, wu_ref[:, cs])
        acc = acc + _dot(((g * jax.nn.sigmoid(g)) * u).astype(BF16), wd_ref[cs, :])
    y = x + g2_ref[0] * acc
    if final:
        y = _rms(y) * fn_ref[...]
    o_ref[0] = y


def _ffn_call(x, mod_sh, mod_sc, mod_g, mod_row, n2, wg, wu, wd, fn, final, tm):
    Bn, Tn, D = x.shape
    row = (lambda b, i: (b, 0, 0)) if mod_row is None else (lambda b, i: (mod_row, 0, 0))
    tok = pl.BlockSpec((1, tm, D), lambda b, i: (b, i, 0))
    mrow = pl.BlockSpec((1, 1, D), row)
    return pl.pallas_call(
        functools.partial(_ffn_kernel, final=final),
        out_shape=jax.ShapeDtypeStruct((Bn, Tn, D), F32),
        grid=(Bn, Tn // tm),
        in_specs=[tok, mrow, mrow, mrow, _const_spec((1, D)), _const_spec(wg.shape), _const_spec(wu.shape),
                  _const_spec(wd.shape), _const_spec((1, D))],
        out_specs=tok,
        compiler_params=_params(("parallel", "parallel")),
        name="ffn",
    )(x, mod_sh, mod_sc, mod_g, n2, wg, wu, wd, fn)


def _pair_swap(w):
    w2 = w.reshape(w.shape[0], -1, 2)
    return jnp.stack([-w2[..., 1], w2[..., 0]], axis=-1).reshape(w.shape)


def _layer_weights(l, w_in, mla_q_norm, mla_w_uq, mla_kv_norm, mla_w_ukv, gla_w_a2, gla_b_a2):
    wi = w_in[l]
    D = wi.shape[0]
    o = 0
    cuts = {}
    for name, sz in (("cq", MLA_RANK), ("ckv", MLA_RANK), ("kr", MLA_ROPE), ("u", SSM_WIDTH), ("gq", GLA_KW),
                     ("gk", GLA_KW), ("gv", GLA_WIDTH), ("gr", GLA_WIDTH), ("af", GLA_GATE_RANK),
                     ("ab", GLA_GATE_RANK), ("gate", 3 * D_MODEL)):
        cuts[name] = wi[:, o:o + sz]
        o += sz
    zeros = lambda n: jnp.zeros((D, n), F32)
    kr = cuts["kr"]
    pad_k = HEAD_SLOT - MLA_NOPE - MLA_ROPE
    wkr = jnp.concatenate([zeros(MLA_NOPE), kr, zeros(pad_k), zeros(MLA_NOPE), _pair_swap(kr), zeros(pad_k)], axis=1)
    wlr = jnp.concatenate([cuts["af"], cuts["ab"], zeros(LR_PAD - 2 * GLA_GATE_RANK)], axis=1)
    uq = mla_w_uq[l].reshape(MLA_RANK, MLA_HEADS, MLA_NOPE + MLA_ROPE)
    zq = jnp.zeros((MLA_RANK, MLA_HEADS, pad_k), F32)
    wuq = jnp.concatenate([uq, zq], axis=2).reshape(MLA_RANK, MLA_HEADS * HEAD_SLOT)
    uq_sw = _pair_swap(uq[..., MLA_NOPE:].reshape(MLA_RANK, -1)).reshape(MLA_RANK, MLA_HEADS, MLA_ROPE)
    wuqs = jnp.concatenate([jnp.zeros((MLA_RANK, MLA_HEADS, MLA_NOPE), F32), uq_sw, zq], axis=2)
    wuqs = wuqs.reshape(MLA_RANK, MLA_HEADS * HEAD_SLOT)
    wa2 = jnp.zeros((2, LR_PAD, GLA_KW), F32)
    wa2 = wa2.at[0, :GLA_GATE_RANK].set(gla_w_a2[l, 0]).at[1, GLA_GATE_RANK:2 * GLA_GATE_RANK].set(gla_w_a2[l, 1])
    b16 = lambda a: a.astype(BF16)
    return dict(
        wa=b16(jnp.concatenate([cuts["cq"], cuts["ckv"]], axis=1)), wkr=b16(wkr), wu=b16(cuts["u"]),
        wg=b16(jnp.concatenate([cuts["gq"], cuts["gk"], cuts["gv"]], axis=1)), wlr=b16(wlr),
        qn=mla_q_norm[l].reshape(1, -1), kvn=mla_kv_norm[l].reshape(1, -1),
        wuq=b16(wuq), wuqs=b16(wuqs), wukv=b16(mla_w_ukv[l]), wa2=b16(wa2),
        ba2=gla_b_a2[l].reshape(2, 1, GLA_KW), wr=b16(cuts["gr"]), wgate=b16(cuts["gate"]))


def _rope_tables(T):
    t = jnp.arange(T, dtype=jnp.int32)
    rows = (t // GRID_W).astype(F32)
    cols = (t % GRID_W).astype(F32)
    inv = ROPE_BASE ** (-jnp.arange(ROPE_FREQS, dtype=F32) / ROPE_FREQS)
    ang = jnp.concatenate([rows[:, None] * inv, cols[:, None] * inv], axis=-1)
    ang2 = jnp.repeat(ang, 2, axis=1)
    pad_k = HEAD_SLOT - MLA_NOPE - MLA_ROPE
    cos = jnp.concatenate([jnp.ones((T, MLA_NOPE), F32), jnp.cos(ang2), jnp.ones((T, pad_k), F32)], axis=1)
    sin = jnp.concatenate([jnp.zeros((T, MLA_NOPE), F32), jnp.sin(ang2), jnp.zeros((T, pad_k), F32)], axis=1)
    return cos, sin


def _s5_matrices(bbr, bbi, c_re, c_im):
    n = SSM_N
    grp_row = jnp.arange(SSM_WIDTH)[:, None] // SSM_GROUP
    grp_col = jnp.arange(n)[None, :] // SSM_STATE
    blk = grp_row == grp_col
    tile = lambda m: jnp.where(blk, jnp.tile(m, (SSM_GROUPS, 1)), 0.0)
    hw, hn = SSM_WIDTH // 2, n // 2
    halves = lambda m: [m[k * hw:(k + 1) * hw, k * hn:(k + 1) * hn] for k in range(2)]
    bd = jnp.stack([jnp.stack([jnp.concatenate([r, i], axis=1)
                               for r, i in zip(halves(tile(bbr[d])), halves(tile(bbi[d])))])
                    for d in range(2)]).astype(BF16)
    def readout(c):
        ct = jnp.transpose(c, (0, 2, 1)).reshape(n, SSM_GROUP)
        full = jnp.where(blk.T, jnp.tile(ct, (1, SSM_GROUPS)), 0.0)
        return jnp.stack([full[k * hn:(k + 1) * hn, k * hw:(k + 1) * hw] for k in range(2)]).astype(BF16)
    return bd, readout(c_re), readout(c_im)


def kernel(x, c, ctx, c_ctx, w_mod, b_mod, norm1, norm2, w_in, mla_q_norm, mla_w_uq, mla_kv_norm, mla_w_ukv, mla_w_o, ssm_lam_re, ssm_lam_im, ssm_log_dt, ssm_b_re, ssm_b_im, ssm_c_re, ssm_c_im, ssm_d, ssm_w_glu, ssm_b_glu, ssm_w_o, gla_w_a2, gla_b_a2, gla_norm, gla_w_o, w_out, ffn_w_up, ffn_w_down, final_norm):
    B, T, D = x.shape
    C = ctx.shape[1]
    depth = w_mod.shape[0]
    n_rows = -(-(B + 1) // 8) * 8
    cvec = jnp.zeros((n_rows, D), F32).at[:B].set(c).at[B].set(c_ctx)
    mod = _mod_call(cvec, w_mod.astype(BF16), b_mod)
    cos_l, sin_l = _rope_tables(T)
    cos_c = jnp.concatenate([jnp.ones((C, HEAD_SLOT), F32)], axis=0)
    sin_c = jnp.zeros((C, HEAD_SLOT), F32)
    tm_in, tm_merge, tm_ffn = min(TM_IN, T), min(TM_MERGE, T), min(TM_FFN, T)
    tq, tk = min(TQ, T), min(TK, T // 2)
    tg = min(GLA_BLOCK, T)
    b16 = lambda a: a.astype(BF16)

    xc = ctx
    for l in range(depth):
        with_ctx = l < depth - 1
        m6 = [mod[l, :, j * D:(j + 1) * D].reshape(n_rows, 1, D) for j in range(6)]
        sh1, sc1, g1, sh2, sc2, g2 = m6
        n1 = norm1[l].reshape(1, D)
        n2 = norm2[l].reshape(1, D)
        w = _layer_weights(l, w_in, mla_q_norm, mla_w_uq, mla_kv_norm, mla_w_ukv, gla_w_a2, gla_b_a2)
        w.update(wmo=b16(mla_w_o[l]), wso=b16(ssm_w_o[l]), gn=gla_norm[l].reshape(1, -1), wgo=b16(gla_w_o[l]),
                 wout=b16(w_out[l]))
        up = ffn_w_up[l]
        wup_g, wup_u, wdown = b16(up[:, :FFN_HIDDEN]), b16(up[:, FFN_HIDDEN:]), b16(ffn_w_down[l])
        fn = final_norm.reshape(1, D)

        qc, kc, kvc, uc, gqc, gkc, gvc, lgfc, lgbc = _in_proj_call(xc, sh1, sc1, B, n1, cos_c, sin_c, w, C)
        ql, kl, kvl, ul, gql, gkl, gvl, lgfl, lgbl = _in_proj_call(x, sh1, sc1, None, n1, cos_l, sin_l, w, tm_in)

        oa = _attn_call(ql, kc, kvc, kl, kvl, tq, tk)
        oac = _attn_call(qc, kc, kvc, None, None, C, tk) if with_ctx else None

        ar, ai, bbr, bbi = _s5_disc_call(ssm_lam_re[l], ssm_lam_im[l], ssm_log_dt[l], ssm_b_re[l], ssm_b_im[l])
        bd, cre_t, cim_t = _s5_matrices(bbr, bbi, ssm_c_re[l], ssm_c_im[l])
        uc_tb = jnp.transpose(uc, (1, 0, 2))
        ul_tb = jnp.transpose(ul, (1, 0, 2))
        h0 = jnp.zeros((2, B, 2 * SSM_N), F32)
        yfc, ybc, hc = _s5_scan_call(uc_tb, bd, ar, ai, cre_t, cim_t, h0, S5_TL)
        yfl, ybl, _ = _s5_scan_call(ul_tb, bd, ar, ai, cre_t, cim_t, hc, S5_TL)
        d_skip = ssm_d[l].reshape(1, -1)
        wglu, bglu = b16(ssm_w_glu[l]), ssm_b_glu[l].reshape(1, -1)
        flat = lambda a: a.reshape(-1, SSM_WIDTH)
        ysl = _s5_out_call(flat(ul_tb), flat(yfl), flat(ybl), d_skip, wglu, bglu, S5_ROWS_OUT)
        ysl = jnp.transpose(ysl.reshape(T, B, SSM_WIDTH), (1, 0, 2))

        s0 = jnp.zeros((B, 2, GLA_WIDTH, GLA_KW), F32)
        ofc, obc, sc_fin = _gla_call(gqc, gkc, gvc, lgfc, lgbc, s0, min(GLA_BLOCK, C))
        ofl, obl, _ = _gla_call(gql, gkl, gvl, lgfl, lgbl, sc_fin, tg)

        x = _merge_call(x, oa, ysl, ofl, obl, sh1, sc1, g1, None, n1, w, tm_merge)
        x = _ffn_call(x, sh2, sc2, g2, None, n2, wup_g, wup_u, wdown, fn, l == depth - 1, tm_ffn)
        if with_ctx:
            ysc = _s5_out_call(flat(uc_tb), flat(yfc), flat(ybc), d_skip, wglu, bglu, S5_ROWS_OUT)
            ysc = jnp.transpose(ysc.reshape(C, B, SSM_WIDTH), (1, 0, 2))
            xc = _merge_call(xc, oac, ysc, ofc, obc, sh1, sc1, g1, B, n1, w, C)
            xc = _ffn_call(xc, sh2, sc2, g2, B, n2, wup_g, wup_u, wdown, fn, False, C)
    return x
```

```python
import functools
import math

import jax
import jax.numpy as jnp
from jax import lax
from jax.experimental import pallas as pl
from jax.experimental.pallas import tpu as pltpu

F32 = jnp.float32
BF16 = jnp.bfloat16

D_MODEL = 1024
GRID_W = 64
EPS = 1e-6
MLA_HEADS = 8
MLA_RANK = 256
MLA_NOPE = 64
MLA_ROPE = 32
MLA_V = 64
MLA_WIDTH = MLA_HEADS * MLA_V
MLA_SCALE = (MLA_NOPE + MLA_ROPE) ** -0.5
ROPE_FREQS = MLA_ROPE // 4
ROPE_BASE = 10000.0
LOG2_E = math.log2(math.e)
HEAD_SLOT = 128
SSM_WIDTH = 512
SSM_GROUP = 16
SSM_GROUPS = SSM_WIDTH // SSM_GROUP
SSM_STATE = 64
SSM_N = SSM_GROUPS * SSM_STATE
GLA_HEADS = 4
GLA_DK = 64
GLA_DV = 128
GLA_KW = GLA_HEADS * GLA_DK
GLA_WIDTH = GLA_HEADS * GLA_DV
GLA_GATE_RANK = 16
GLA_GATE_TAU = 16.0
GLA_CHUNK = 64
FFN_HIDDEN = 2816
LR_PAD = 128

VMEM_LIMIT = 56 * 1024 * 1024

TM_IN = 512
TM_MERGE = 512
TM_FFN = 512
TQ = 512
TK = 512
ATT_STRIP = 64
S5_TL = 64
S5_ROWS_OUT = 2048
GLA_BLOCK = 256


def _const_spec(shape):
    nd = len(shape)
    return pl.BlockSpec(shape, lambda *_: (0,) * nd, pipeline_mode=pl.Buffered(1))


def _params(sem):
    return pltpu.CompilerParams(dimension_semantics=sem, vmem_limit_bytes=VMEM_LIMIT)


def _rms(x):
    return x * lax.rsqrt(jnp.mean(x * x, axis=-1, keepdims=True) + EPS)


def _dot(a, b):
    return jnp.dot(a, b, preferred_element_type=F32)


def _dot_nt(a, b):
    return lax.dot_general(a, b, (((1,), (1,)), ((), ())), preferred_element_type=F32)


def _dot_tn(a, b):
    return lax.dot_general(a, b, (((0,), (0,)), ((), ())), preferred_element_type=F32)


def _mod_kernel(c_ref, w_ref, b_ref, o_ref):
    c = c_ref[...]
    s = (c * jax.nn.sigmoid(c)).astype(BF16)
    o_ref[0] = _dot(s, w_ref[0]) + b_ref[0]


def _mod_call(cvec, w_mod, b_mod):
    L, D, N = w_mod.shape
    R = cvec.shape[0]
    tn = 1536
    return pl.pallas_call(
        _mod_kernel,
        out_shape=jax.ShapeDtypeStruct((L, R, N), F32),
        grid=(L, N // tn),
        in_specs=[pl.BlockSpec((R, D), lambda l, j: (0, 0)),
                  pl.BlockSpec((1, D, tn), lambda l, j: (l, 0, j)),
                  pl.BlockSpec((1, 1, tn), lambda l, j: (l, 0, j))],
        out_specs=pl.BlockSpec((1, R, tn), lambda l, j: (l, 0, j)),
        compiler_params=_params(("arbitrary", "arbitrary")),
        name="mod",
    )(cvec, w_mod, b_mod.reshape(L, 1, N))


def _in_proj_kernel(x_ref, sh_ref, sc_ref, n1_ref, cos_ref, sin_ref,
                    wa_ref, wkr_ref, wu_ref, wg_ref, wlr_ref,
                    qn_ref, kvn_ref, wuq_ref, wuqs_ref, wukv_ref, wa2_ref, ba2_ref,
                    q_ref, k_ref, kv_ref, u_ref, gq_ref, gk_ref, gv_ref, lgf_ref, lgb_ref):
    x = x_ref[0]
    h = _rms(x) * n1_ref[...]
    h = h * (1.0 + sc_ref[0]) + sh_ref[0]
    hb = h.astype(BF16)
    cos = cos_ref[...]
    sin = sin_ref[...]
    cos8 = jnp.concatenate([cos] * MLA_HEADS, axis=1)
    sin8 = jnp.concatenate([sin] * MLA_HEADS, axis=1)

    za = _dot(hb, wa_ref[...])
    cqn = (_rms(za[:, :MLA_RANK]) * qn_ref[...]).astype(BF16)
    q = _dot(cqn, wuq_ref[...]) * cos8 + _dot(cqn, wuqs_ref[...]) * sin8
    q_ref[0] = (q * (MLA_SCALE * LOG2_E)).astype(BF16)

    ckvn = (_rms(za[:, MLA_RANK:]) * kvn_ref[...]).astype(BF16)
    kv = _dot(ckvn, wukv_ref[...])
    kr2 = _dot(hb, wkr_ref[...])
    krr = kr2[:, :HEAD_SLOT] * cos + kr2[:, HEAD_SLOT:] * sin
    nope = lax.broadcasted_iota(jnp.int32, krr.shape, 1) < MLA_NOPE
    for hd in range(MLA_HEADS):
        sl = slice(hd * HEAD_SLOT, (hd + 1) * HEAD_SLOT)
        k_ref[0, :, sl] = jnp.where(nope, kv[:, sl], krr).astype(BF16)
        kv_ref[0, :, sl] = jnp.where(nope, 1.0, kv[:, sl]).astype(BF16)

    u_ref[0] = _dot(hb, wu_ref[...])
    zg = _dot(hb, wg_ref[...])
    gq_ref[0] = zg[:, :GLA_KW]
    gk_ref[0] = zg[:, GLA_KW:2 * GLA_KW]
    gv_ref[0] = zg[:, 2 * GLA_KW:]
    zlr = _dot(hb, wlr_ref[...]).astype(BF16)
    for d, o_ref in enumerate((lgf_ref, lgb_ref)):
        z = _dot(zlr, wa2_ref[d]) + ba2_ref[d]
        o_ref[0] = (jnp.minimum(z, 0.0) - jnp.log1p(jnp.exp(-jnp.abs(z)))) * (1.0 / GLA_GATE_TAU)


def _in_proj_call(x, mod_sh, mod_sc, mod_row, n1, cos_t, sin_t, w, tm):
    Bn, Tn, D = x.shape
    nt = Tn // tm
    row = (lambda b, i: (b, 0, 0)) if mod_row is None else (lambda b, i: (mod_row, 0, 0))
    tok = lambda width: pl.BlockSpec((1, tm, width), lambda b, i: (b, i, 0))
    weights = [w["wa"], w["wkr"], w["wu"], w["wg"], w["wlr"], w["qn"], w["kvn"],
               w["wuq"], w["wuqs"], w["wukv"], w["wa2"], w["ba2"]]
    outs = [(D, BF16), (D, BF16), (D, BF16), (SSM_WIDTH, F32), (GLA_KW, F32), (GLA_KW, F32),
            (GLA_WIDTH, F32), (GLA_KW, F32), (GLA_KW, F32)]
    out_shape = [jax.ShapeDtypeStruct((Bn, Tn, wd), dt) for wd, dt in outs]
    out_specs = [tok(wd) for wd, _ in outs]
    return pl.pallas_call(
        _in_proj_kernel,
        out_shape=out_shape,
        grid=(Bn, nt),
        in_specs=[tok(D),
                  pl.BlockSpec((1, 1, D), row), pl.BlockSpec((1, 1, D), row),
                  _const_spec((1, D)),
                  pl.BlockSpec((tm, HEAD_SLOT), lambda b, i: (i, 0)),
                  pl.BlockSpec((tm, HEAD_SLOT), lambda b, i: (i, 0))]
                 + [_const_spec(a.shape) for a in weights],
        out_specs=out_specs,
        compiler_params=_params(("parallel", "parallel")),
        name="in_proj",
    )(x, mod_sh, mod_sc, n1, cos_t, sin_t, *weights)


def _lane_fold(x, op):
    out = x[:, :HEAD_SLOT]
    for j in range(1, x.shape[1] // HEAD_SLOT):
        out = op(out, x[:, j * HEAD_SLOT:(j + 1) * HEAD_SLOT])
    return out


def _attn_update(s, kvt, carry):
    m, acc = carry
    m_new = jnp.maximum(m, jnp.max(_lane_fold(s, jnp.maximum), axis=-1, keepdims=True))
    p = jnp.exp2(s - m_new)
    acc = jnp.exp2(m - m_new) * acc + _dot(p.astype(BF16), kvt)
    return m_new, acc


def _attn_kernel(*refs, n_lat_tiles, tk):
    if n_lat_tiles:
        q_ref, kc_ref, kvc_ref, kl_ref, kvl_ref, o_ref, s_ref, mx_ref, p_ref, m_ref, acc_ref = refs
    else:
        q_ref, kc_ref, kvc_ref, o_ref = refs
    tq = q_ref.shape[1]
    sls = [slice(hh * HEAD_SLOT, (hh + 1) * HEAD_SLOT) for hh in range(2)]
    qs = [q_ref[0, :, sl] for sl in sls]
    init = (jnp.full((tq, 1), -jnp.inf, F32), jnp.zeros((tq, HEAD_SLOT), F32))
    carry = tuple(_attn_update(_dot_nt(qs[hh], kc_ref[0, :, sls[hh]]), kvc_ref[0, :, sls[hh]], init)
                  for hh in range(2))
    if n_lat_tiles:
        for hh in range(2):
            m, acc = carry[hh]
            m_ref[hh] = jnp.broadcast_to(m, (tq, HEAD_SLOT))
            acc_ref[hh] = acc
        rows = lambda i: pl.ds(pl.multiple_of(i * tk, tk), tk)

        def scores(i, slot):
            for hh in range(2):
                s = _dot_nt(qs[hh], kl_ref[0, rows(i), sls[hh]])
                s_ref[slot, hh] = s
                mx_ref[slot, hh] = _lane_fold(s, jnp.maximum)

        def update(i, slot):
            for hh in range(2):
                for r in range(0, tq, ATT_STRIP):
                    rs = slice(r, r + ATT_STRIP)
                    m_old = m_ref[hh, rs, :]
                    m_new = jnp.maximum(m_old, jnp.max(mx_ref[slot, hh, rs, :], axis=-1, keepdims=True))
                    p = jnp.exp2(s_ref[slot, hh, rs, :] - jnp.concatenate([m_new] * (tk // HEAD_SLOT), axis=1))
                    m_ref[hh, rs, :] = m_new
                    acc_ref[hh, rs, :] = jnp.exp2(m_old - m_new) * acc_ref[hh, rs, :]
                    p_ref[hh, rs, :] = p.astype(BF16)
                acc_ref[hh] += _dot(p_ref[hh], kvl_ref[0, rows(i), sls[hh]])

        scores(0, 0)

        def body(j, c):
            i = 2 * j
            scores(i + 1, 1)
            update(i, 0)
            scores(jnp.minimum(i + 2, n_lat_tiles - 1), 0)
            update(i + 1, 1)
            return c

        lax.fori_loop(0, n_lat_tiles // 2, body, 0)
        carry = tuple((None, acc_ref[hh]) for hh in range(2))
    outs = [(acc * pltpu.roll(1.0 / acc, MLA_NOPE, 1))[:, MLA_NOPE:] for _, acc in carry]
    o_ref[0] = jnp.concatenate(outs, axis=1).astype(o_ref.dtype)


def _attn_call(q, k_ctx, kv_ctx, k_lat, kv_lat, tq, tk):
    Bn, Tq, D = q.shape
    C = k_ctx.shape[1]
    hp = MLA_HEADS // 2
    w2 = 2 * HEAD_SLOT
    n_lat_tiles = 0 if k_lat is None else k_lat.shape[1] // tk
    in_specs = [pl.BlockSpec((1, tq, w2), lambda b, h, i: (b, i, h)),
                pl.BlockSpec((1, C, w2), lambda b, h, i: (b, 0, h)),
                pl.BlockSpec((1, C, w2), lambda b, h, i: (b, 0, h))]
    args = [q, k_ctx, kv_ctx]
    scratch = []
    if n_lat_tiles:
        assert n_lat_tiles % 2 == 0 and tq % ATT_STRIP == 0
        Tk = k_lat.shape[1]
        in_specs += [pl.BlockSpec((1, Tk, w2), lambda b, h, i: (b, 0, h)),
                     pl.BlockSpec((1, Tk, w2), lambda b, h, i: (b, 0, h))]
        args += [k_lat, kv_lat]
        stat = pltpu.VMEM((2, tq, HEAD_SLOT), F32)
        scratch = [pltpu.VMEM((2, 2, tq, tk), F32), pltpu.VMEM((2, 2, tq, HEAD_SLOT), F32),
                   pltpu.VMEM((2, tq, tk), BF16), stat, stat]
    return pl.pallas_call(
        functools.partial(_attn_kernel, n_lat_tiles=n_lat_tiles, tk=tk),
        out_shape=jax.ShapeDtypeStruct((Bn, Tq, MLA_WIDTH), BF16),
        grid=(Bn, hp, Tq // tq),
        in_specs=in_specs,
        out_specs=pl.BlockSpec((1, tq, 2 * MLA_V), lambda b, h, i: (b, i, h)),
        scratch_shapes=scratch,
        compiler_params=_params(("parallel", "parallel", "parallel")),
        name="attn",
    )(*args)


def _s5_disc_kernel(lr_ref, li_ref, ldt_ref, br_ref, bi_ref, ar_ref, ai_ref, bbr_ref, bbi_ref):
    lr = lr_ref[0]
    li = li_ref[0]
    dt = jnp.exp(ldt_ref[0])
    mag = jnp.exp(dt * lr)
    ar = mag * jnp.cos(dt * li)
    ai = mag * jnp.sin(dt * li)
    den = lr * lr + li * li
    fr = ((ar - 1.0) * lr + ai * li) / den
    fi = (ai * lr - (ar - 1.0) * li) / den
    ar_ref[0] = ar
    ai_ref[0] = ai
    br = br_ref[...]
    bi = bi_ref[...]
    bbr_ref[0] = fr * br - fi * bi
    bbi_ref[0] = fr * bi + fi * br


def _s5_disc_call(lam_re, lam_im, log_dt, b_re, b_im):
    n = SSM_N
    lr = lam_re.reshape(2, 1, n)
    li = lam_im.reshape(2, 1, n)
    ldt = jnp.repeat(log_dt, SSM_STATE, axis=1).reshape(2, 1, n)
    br = b_re.reshape(n, SSM_GROUP).T
    bi = b_im.reshape(n, SSM_GROUP).T
    vec = pl.BlockSpec((1, 1, n), lambda d: (d, 0, 0))
    mat = pl.BlockSpec((SSM_GROUP, n), lambda d: (0, 0))
    omat = pl.BlockSpec((1, SSM_GROUP, n), lambda d: (d, 0, 0))
    return pl.pallas_call(
        _s5_disc_kernel,
        out_shape=[jax.ShapeDtypeStruct((2, 1, n), F32), jax.ShapeDtypeStruct((2, 1, n), F32),
                   jax.ShapeDtypeStruct((2, SSM_GROUP, n), F32), jax.ShapeDtypeStruct((2, SSM_GROUP, n), F32)],
        grid=(2,),
        in_specs=[vec, vec, vec, mat, mat],
        out_specs=[vec, vec, omat, omat],
        compiler_params=_params(("arbitrary",)),
        name="s5_disc",
    )(lr, li, ldt, br, bi)


def _s5_scan_kernel(uf_ref, ub_ref, bd_ref, ar_ref, ai_ref, cre_ref, cim_ref, h0_ref,
                    yf_ref, yb_ref, hfin_ref, xs_ref, h_ref, *, tl, nb):
    i = pl.program_id(0)
    n = SSM_N

    @pl.when(i == 0)
    def _():
        h_ref[...] = h0_ref[...]

    half = n // 2
    hw = SSM_WIDTH // 2
    for d, u_ref in enumerate((uf_ref, ub_ref)):
        u = u_ref[...].reshape(tl * nb, SSM_WIDTH).astype(BF16)
        for k in range(2):
            xk = _dot(u[:, k * hw:(k + 1) * hw], bd_ref[d, k])
            xs_ref[d, :, k * half:(k + 1) * half] = xk[:, :half]
            xs_ref[d, :, n + k * half:n + (k + 1) * half] = xk[:, half:]

    for d in range(2):
        for cb in range(2):
            re = slice(cb * half, (cb + 1) * half)
            im = slice(n + cb * half, n + (cb + 1) * half)
            ar = jnp.broadcast_to(ar_ref[d, :, re], (nb, half))
            ai = jnp.broadcast_to(ai_ref[d, :, re], (nb, half))

            def body(t, carry, d=d, re=re, im=im, ar=ar, ai=ai):
                hr, hi = carry
                tt = t if d == 0 else tl - 1 - t
                row = pl.multiple_of(tt * nb, nb)
                nr = ar * hr - ai * hi + xs_ref[d, pl.ds(row, nb), re]
                ni = ar * hi + ai * hr + xs_ref[d, pl.ds(row, nb), im]
                xs_ref[d, pl.ds(row, nb), re] = nr
                xs_ref[d, pl.ds(row, nb), im] = ni
                return nr, ni

            hr, hi = lax.fori_loop(0, tl, body, (h_ref[d, :, re], h_ref[d, :, im]), unroll=4)
            h_ref[d, :, re] = hr
            h_ref[d, :, im] = hi

    for d, y_ref in enumerate((yf_ref, yb_ref)):
        ys = []
        for k in range(2):
            hr = xs_ref[d, :, k * half:(k + 1) * half].astype(BF16)
            hi = xs_ref[d, :, n + k * half:n + (k + 1) * half].astype(BF16)
            ys.append(_dot(hr, cre_ref[k]) - _dot(hi, cim_ref[k]))
        y_ref[...] = jnp.concatenate(ys, axis=1).reshape(tl, nb, SSM_WIDTH)

    @pl.when(i == pl.num_programs(0) - 1)
    def _():
        hfin_ref[...] = h_ref[...]


def _s5_scan_call(u_tb, bd, ar, ai, cre_t, cim_t, h0, tl):
    Tn, nb, W = u_tb.shape
    nc = Tn // tl
    n = SSM_N
    fwd = pl.BlockSpec((tl, nb, W), lambda i: (i, 0, 0))
    bwd = pl.BlockSpec((tl, nb, W), lambda i: (nc - 1 - i, 0, 0))
    return pl.pallas_call(
        functools.partial(_s5_scan_kernel, tl=tl, nb=nb),
        out_shape=[jax.ShapeDtypeStruct((Tn, nb, W), F32), jax.ShapeDtypeStruct((Tn, nb, W), F32),
                   jax.ShapeDtypeStruct((2, nb, 2 * n), F32)],
        grid=(nc,),
        in_specs=[fwd, bwd, _const_spec(bd.shape), _const_spec(ar.shape), _const_spec(ai.shape),
                  _const_spec(cre_t.shape), _const_spec(cim_t.shape), _const_spec(h0.shape)],
        out_specs=[fwd, bwd, pl.BlockSpec((2, nb, 2 * n), lambda i: (0, 0, 0))],
        scratch_shapes=[pltpu.VMEM((2, tl * nb, 2 * n), F32), pltpu.VMEM((2, nb, 2 * n), F32)],
        compiler_params=_params(("arbitrary",)),
        name="s5_scan",
    )(u_tb, u_tb, bd, ar, ai, cre_t, cim_t, h0)


def _s5_out_kernel(u_ref, yf_ref, yb_ref, d_ref, wglu_ref, bglu_ref, o_ref):
    y = d_ref[...] * u_ref[...] + yf_ref[...] + yb_ref[...]
    g = jax.nn.gelu(y)
    z = _dot(g.astype(BF16), wglu_ref[...]) + bglu_ref[...]
    o_ref[...] = (g * jax.nn.sigmoid(z)).astype(o_ref.dtype)


def _s5_out_call(u, yf, yb, d_skip, wglu, bglu, tr):
    R, W = u.shape
    tr = min(tr, R)
    blk = pl.BlockSpec((tr, W), lambda i: (i, 0))
    return pl.pallas_call(
        _s5_out_kernel,
        out_shape=jax.ShapeDtypeStruct((R, W), BF16),
        grid=(R // tr,),
        in_specs=[blk, blk, blk, _const_spec(d_skip.shape), _const_spec(wglu.shape), _const_spec(bglu.shape)],
        out_specs=blk,
        compiler_params=_params(("parallel",)),
        name="s5_out",
    )(u, yf, yb, d_skip, wglu, bglu)


def _split3(x):
    hi = x.astype(BF16)
    r1 = x - hi.astype(F32)
    mid = r1.astype(BF16)
    lo = (r1 - mid.astype(F32)).astype(BF16)
    return hi, mid, lo


def _gla_block(q, k, v, lg, s, d, cst, n_chunks):
    tri_ref, mask_ref, same_ref, hmask_ref, bmask_ref = cst
    L = GLA_CHUNK
    parts = _split3(lg)
    b = sum(_dot(tri_ref[d], p) for p in parts)
    tot = sum(_dot(same_ref[...], p) for p in parts)
    q_in = q * (GLA_DK ** -0.5) * jnp.exp(b)
    k_in = (k * jnp.exp(-b)).astype(BF16)
    k_dec = (k * jnp.exp(tot - b)).astype(BF16)
    vb = v.astype(BF16)
    mask = mask_ref[d] > 0.0
    outs = []
    for hd in range(GLA_HEADS):
        qh = (q_in * hmask_ref[hd]).astype(BF16)
        att = jnp.where(mask, _dot_nt(qh, k_in), 0.0).astype(BF16)
        outs.append(_dot(att, vb[:, hd * GLA_DV:(hd + 1) * GLA_DV]))
    o_intra = jnp.concatenate(outs, axis=1)
    q_in_b = q_in.astype(BF16)
    bmask = bmask_ref[...]
    o_rows = [None] * n_chunks
    for c in (range(n_chunks) if d == 0 else reversed(range(n_chunks))):
        rows = slice(c * L, (c + 1) * L)
        o_rows[c] = _dot_nt(q_in_b[rows], s.astype(BF16))
        dec = jnp.exp(tot[c * L:c * L + 1, :])
        s = dec * s + _dot_tn(vb[rows], k_dec[rows]) * bmask
    return o_intra + jnp.concatenate(o_rows, axis=0), s


def _gla_kernel(qf_ref, kf_ref, vf_ref, lgf_ref, qb_ref, kb_ref, vb_ref, lgb_ref, s0_ref,
                tri_ref, mask_ref, same_ref, hmask_ref, bmask_ref,
                of_ref, ob_ref, sfin_ref, s_ref, *, n_chunks):
    i = pl.program_id(1)
    cst = (tri_ref, mask_ref, same_ref, hmask_ref, bmask_ref)

    @pl.when(i == 0)
    def _():
        s_ref[...] = s0_ref[0]

    o, s = _gla_block(qf_ref[0], kf_ref[0], vf_ref[0], lgf_ref[0], s_ref[0], 0, cst, n_chunks)
    of_ref[0] = o
    s_ref[0] = s
    o, s = _gla_block(qb_ref[0], kb_ref[0], vb_ref[0], lgb_ref[0], s_ref[1], 1, cst, n_chunks)
    ob_ref[0] = o
    s_ref[1] = s

    @pl.when(i == pl.num_programs(1) - 1)
    def _():
        sfin_ref[0] = s_ref[...]


def _gla_consts(tg):
    t = jnp.arange(tg)
    same = (t[:, None] // GLA_CHUNK) == (t[None, :] // GLA_CHUNK)
    low = same & (t[None, :] <= t[:, None])
    upp = same & (t[None, :] >= t[:, None])
    tri = jnp.stack([low, upp])
    hmask = (jnp.arange(GLA_KW)[None, None, :] // GLA_DK) == jnp.arange(GLA_HEADS)[:, None, None]
    bmask = (jnp.arange(GLA_WIDTH)[:, None] // GLA_DV) == (jnp.arange(GLA_KW)[None, :] // GLA_DK)
    return tri.astype(BF16), tri.astype(F32), same.astype(BF16), hmask.astype(F32), bmask.astype(F32)


def _gla_call(q, k, v, lgf, lgb, s0, tg):
    Bn, Tn, _ = q.shape
    nblk = Tn // tg
    consts = _gla_consts(tg)
    fwd = lambda w: pl.BlockSpec((1, tg, w), lambda b, i: (b, i, 0))
    bwd = lambda w: pl.BlockSpec((1, tg, w), lambda b, i: (b, nblk - 1 - i, 0))
    st = pl.BlockSpec((1, 2, GLA_WIDTH, GLA_KW), lambda b, i: (b, 0, 0, 0))
    return pl.pallas_call(
        functools.partial(_gla_kernel, n_chunks=tg // GLA_CHUNK),
        out_shape=[jax.ShapeDtypeStruct((Bn, Tn, GLA_WIDTH), F32), jax.ShapeDtypeStruct((Bn, Tn, GLA_WIDTH), F32),
                   jax.ShapeDtypeStruct((Bn, 2, GLA_WIDTH, GLA_KW), F32)],
        grid=(Bn, nblk),
        in_specs=[fwd(GLA_KW), fwd(GLA_KW), fwd(GLA_WIDTH), fwd(GLA_KW),
                  bwd(GLA_KW), bwd(GLA_KW), bwd(GLA_WIDTH), bwd(GLA_KW), st]
                 + [_const_spec(a.shape) for a in consts],
        out_specs=[fwd(GLA_WIDTH), bwd(GLA_WIDTH), st],
        scratch_shapes=[pltpu.VMEM((2, GLA_WIDTH, GLA_KW), F32)],
        compiler_params=_params(("parallel", "arbitrary")),
        name="gla",
    )(q, k, v, lgf, q, k, v, lgb, s0, *consts)


def _merge_kernel(x_ref, oa_ref, ys_ref, of_ref, ob_ref, sh_ref, sc_ref, g1_ref, n1_ref,
                  wgate_ref, wr_ref, wmo_ref, wso_ref, gn_ref, wgo_ref, wout_ref, o_ref):
    x = x_ref[0]
    h = _rms(x) * n1_ref[...]
    hb = (h * (1.0 + sc_ref[0]) + sh_ref[0]).astype(BF16)
    ya = _dot(oa_ref[0], wmo_ref[...])
    yb = _dot(ys_ref[0], wso_ref[...])
    o = of_ref[0] + ob_ref[0]
    on = jnp.concatenate([_rms(o[:, hd * GLA_DV:(hd + 1) * GLA_DV]) for hd in range(GLA_HEADS)], axis=1)
    r = _dot(hb, wr_ref[...])
    yg = _dot(((on * gn_ref[...]) * (r * jax.nn.sigmoid(r))).astype(BF16), wgo_ref[...])
    m = jnp.zeros_like(ya)
    for j, br in enumerate((ya, yb, yg)):
        gate = jax.nn.sigmoid(_dot(hb, wgate_ref[:, j * D_MODEL:(j + 1) * D_MODEL]))
        m = m + gate * br
    o_ref[0] = x + g1_ref[0] * _dot(m.astype(BF16), wout_ref[...])


def _merge_call(x, oa, ys, of, ob, mod_sh, mod_sc, mod_g, mod_row, n1, w, tm):
    Bn, Tn, D = x.shape
    row = (lambda b, i: (b, 0, 0)) if mod_row is None else (lambda b, i: (mod_row, 0, 0))
    tok = lambda width: pl.BlockSpec((1, tm, width), lambda b, i: (b, i, 0))
    mrow = pl.BlockSpec((1, 1, D), row)
    weights = [w["wgate"], w["wr"], w["wmo"], w["wso"], w["gn"], w["wgo"], w["wout"]]
    return pl.pallas_call(
        _merge_kernel,
        out_shape=jax.ShapeDtypeStruct((Bn, Tn, D), F32),
        grid=(Bn, Tn // tm),
        in_specs=[tok(D), tok(MLA_WIDTH), tok(SSM_WIDTH), tok(GLA_WIDTH), tok(GLA_WIDTH),
                  mrow, mrow, mrow, _const_spec((1, D))] + [_const_spec(a.shape) for a in weights],
        out_specs=tok(D),
        compiler_params=_params(("parallel", "parallel")),
        name="merge",
    )(x, oa, ys, of, ob, mod_sh, mod_sc, mod_g, n1, *weights)


def _ffn_kernel(x_ref, sh_ref, sc_ref, g2_ref, n2_ref, wg_ref, wu_ref, wd_ref, fn_ref, o_ref, *, final):
    x = x_ref[0]
    h = _rms(x) * n2_ref[...]
    hb = (h * (1.0 + sc_ref[0]) + sh_ref[0]).astype(BF16)
    acc = jnp.zeros_like(x)
    half = FFN_HIDDEN // 2
    for j in range(2):
        cs = slice(j * half, (j + 1) * half)
        g = _dot(hb, wg_ref[:, cs])
        u = _dot(hb, wu_ref[:, cs])
        acc = acc + _dot(((g * jax.nn.sigmoid(g)) * u).astype(BF16), wd_ref[cs, :])
    y = x + g2_ref[0] * acc
    if final:
        y = _rms(y) * fn_ref[...]
    o_ref[0] = y


def _ffn_call(x, mod_sh, mod_sc, mod_g, mod_row, n2, wg, wu, wd, fn, final, tm):
    Bn, Tn, D = x.shape
    row = (lambda b, i: (b, 0, 0)) if mod_row is None else (lambda b, i: (mod_row, 0, 0))
    tok = pl.BlockSpec((1, tm, D), lambda b, i: (b, i, 0))
    mrow = pl.BlockSpec((1, 1, D), row)
    return pl.pallas_call(
        functools.partial(_ffn_kernel, final=final),
        out_shape=jax.ShapeDtypeStruct((Bn, Tn, D), F32),
        grid=(Bn, Tn // tm),
        in_specs=[tok, mrow, mrow, mrow, _const_spec((1, D)), _const_spec(wg.shape), _const_spec(wu.shape),
                  _const_spec(wd.shape), _const_spec((1, D))],
        out_specs=tok,
        compiler_params=_params(("parallel", "parallel")),
        name="ffn",
    )(x, mod_sh, mod_sc, mod_g, n2, wg, wu, wd, fn)


def _pair_swap(w):
    w2 = w.reshape(w.shape[0], -1, 2)
    return jnp.stack([-w2[..., 1], w2[..., 0]], axis=-1).reshape(w.shape)


def _layer_weights(l, w_in, mla_q_norm, mla_w_uq, mla_kv_norm, mla_w_ukv, gla_w_a2, gla_b_a2):
    wi = w_in[l]
    D = wi.shape[0]
    o = 0
    cuts = {}
    for name, sz in (("cq", MLA_RANK), ("ckv", MLA_RANK), ("kr", MLA_ROPE), ("u", SSM_WIDTH), ("gq", GLA_KW),
                     ("gk", GLA_KW), ("gv", GLA_WIDTH), ("gr", GLA_WIDTH), ("af", GLA_GATE_RANK),
                     ("ab", GLA_GATE_RANK), ("gate", 3 * D_MODEL)):
        cuts[name] = wi[:, o:o + sz]
        o += sz
    zeros = lambda n: jnp.zeros((D, n), F32)
    kr = cuts["kr"]
    pad_k = HEAD_SLOT - MLA_NOPE - MLA_ROPE
    wkr = jnp.concatenate([zeros(MLA_NOPE), kr, zeros(pad_k), zeros(MLA_NOPE), _pair_swap(kr), zeros(pad_k)], axis=1)
    wlr = jnp.concatenate([cuts["af"], cuts["ab"], zeros(LR_PAD - 2 * GLA_GATE_RANK)], axis=1)
    uq = mla_w_uq[l].reshape(MLA_RANK, MLA_HEADS, MLA_NOPE + MLA_ROPE)
    zq = jnp.zeros((MLA_RANK, MLA_HEADS, pad_k), F32)
    wuq = jnp.concatenate([uq, zq], axis=2).reshape(MLA_RANK, MLA_HEADS * HEAD_SLOT)
    uq_sw = _pair_swap(uq[..., MLA_NOPE:].reshape(MLA_RANK, -1)).reshape(MLA_RANK, MLA_HEADS, MLA_ROPE)
    wuqs = jnp.concatenate([jnp.zeros((MLA_RANK, MLA_HEADS, MLA_NOPE), F32), uq_sw, zq], axis=2)
    wuqs = wuqs.reshape(MLA_RANK, MLA_HEADS * HEAD_SLOT)
    wa2 = jnp.zeros((2, LR_PAD, GLA_KW), F32)
    wa2 = wa2.at[0, :GLA_GATE_RANK].set(gla_w_a2[l, 0]).at[1, GLA_GATE_RANK:2 * GLA_GATE_RANK].set(gla_w_a2[l, 1])
    b16 = lambda a: a.astype(BF16)
    return dict(
        wa=b16(jnp.concatenate([cuts["cq"], cuts["ckv"]], axis=1)), wkr=b16(wkr), wu=b16(cuts["u"]),
        wg=b16(jnp.concatenate([cuts["gq"], cuts["gk"], cuts["gv"]], axis=1)), wlr=b16(wlr),
        qn=mla_q_norm[l].reshape(1, -1), kvn=mla_kv_norm[l].reshape(1, -1),
        wuq=b16(wuq), wuqs=b16(wuqs), wukv=b16(mla_w_ukv[l]), wa2=b16(wa2),
        ba2=gla_b_a2[l].reshape(2, 1, GLA_KW), wr=b16(cuts["gr"]), wgate=b16(cuts["gate"]))


def _rope_tables(T):
    t = jnp.arange(T, dtype=jnp.int32)
    rows = (t // GRID_W).astype(F32)
    cols = (t % GRID_W).astype(F32)
    inv = ROPE_BASE ** (-jnp.arange(ROPE_FREQS, dtype=F32) / ROPE_FREQS)
    ang = jnp.concatenate([rows[:, None] * inv, cols[:, None] * inv], axis=-1)
    ang2 = jnp.repeat(ang, 2, axis=1)
    pad_k = HEAD_SLOT - MLA_NOPE - MLA_ROPE
    cos = jnp.concatenate([jnp.ones((T, MLA_NOPE), F32), jnp.cos(ang2), jnp.ones((T, pad_k), F32)], axis=1)
    sin = jnp.concatenate([jnp.zeros((T, MLA_NOPE), F32), jnp.sin(ang2), jnp.zeros((T, pad_k), F32)], axis=1)
    return cos, sin


def _s5_matrices(bbr, bbi, c_re, c_im):
    n = SSM_N
    grp_row = jnp.arange(SSM_WIDTH)[:, None] // SSM_GROUP
    grp_col = jnp.arange(n)[None, :] // SSM_STATE
    blk = grp_row == grp_col
    tile = lambda m: jnp.where(blk, jnp.tile(m, (SSM_GROUPS, 1)), 0.0)
    hw, hn = SSM_WIDTH // 2, n // 2
    halves = lambda m: [m[k * hw:(k + 1) * hw, k * hn:(k + 1) * hn] for k in range(2)]
    bd = jnp.stack([jnp.stack([jnp.concatenate([r, i], axis=1)
                               for r, i in zip(halves(tile(bbr[d])), halves(tile(bbi[d])))])
                    for d in range(2)]).astype(BF16)
    def readout(c):
        ct = jnp.transpose(c, (0, 2, 1)).reshape(n, SSM_GROUP)
        full = jnp.where(blk.T, jnp.tile(ct, (1, SSM_GROUPS)), 0.0)
        return jnp.stack([full[k * hn:(k + 1) * hn, k * hw:(k + 1) * hw] for k in range(2)]).astype(BF16)
    return bd, readout(c_re), readout(c_im)


def kernel(x, c, ctx, c_ctx, w_mod, b_mod, norm1, norm2, w_in, mla_q_norm, mla_w_uq, mla_kv_norm, mla_w_ukv, mla_w_o, ssm_lam_re, ssm_lam_im, ssm_log_dt, ssm_b_re, ssm_b_im, ssm_c_re, ssm_c_im, ssm_d, ssm_w_glu, ssm_b_glu, ssm_w_o, gla_w_a2, gla_b_a2, gla_norm, gla_w_o, w_out, ffn_w_up, ffn_w_down, final_norm):
    B, T, D = x.shape
    C = ctx.shape[1]
    depth = w_mod.shape[0]
    n_rows = -(-(B + 1) // 8) * 8
    cvec = jnp.zeros((n_rows, D), F32).at[:B].set(c).at[B].set(c_ctx)
    mod = _mod_call(cvec, w_mod.astype(BF16), b_mod)
    cos_l, sin_l = _rope_tables(T)
    cos_c = jnp.concatenate([jnp.ones((C, HEAD_SLOT), F32)], axis=0)
    sin_c = jnp.zeros((C, HEAD_SLOT), F32)
    tm_in, tm_merge, tm_ffn = min(TM_IN, T), min(TM_MERGE, T), min(TM_FFN, T)
    tq, tk = min(TQ, T), min(TK, T // 2)
    tg = min(GLA_BLOCK, T)
    b16 = lambda a: a.astype(BF16)

    xc = ctx
    for l in range(depth):
        with_ctx = l < depth - 1
        m6 = [mod[l, :, j * D:(j + 1) * D].reshape(n_rows, 1, D) for j in range(6)]
        sh1, sc1, g1, sh2, sc2, g2 = m6
        n1 = norm1[l].reshape(1, D)
        n2 = norm2[l].reshape(1, D)
        w = _layer_weights(l, w_in, mla_q_norm, mla_w_uq, mla_kv_norm, mla_w_ukv, gla_w_a2, gla_b_a2)
        w.update(wmo=b16(mla_w_o[l]), wso=b16(ssm_w_o[l]), gn=gla_norm[l].reshape(1, -1), wgo=b16(gla_w_o[l]),
                 wout=b16(w_out[l]))
        up = ffn_w_up[l]
        wup_g, wup_u, wdown = b16(up[:, :FFN_HIDDEN]), b16(up[:, FFN_HIDDEN:]), b16(ffn_w_down[l])
        fn = final_norm.reshape(1, D)

        qc, kc, kvc, uc, gqc, gkc, gvc, lgfc, lgbc = _in_proj_call(xc, sh1, sc1, B, n1, cos_c, sin_c, w, C)
        ql, kl, kvl, ul, gql, gkl, gvl, lgfl, lgbl = _in_proj_call(x, sh1, sc1, None, n1, cos_l, sin_l, w, tm_in)

        oa = _attn_call(ql, kc, kvc, kl, kvl, tq, tk)
        oac = _attn_call(qc, kc, kvc, None, None, C, tk) if with_ctx else None

        ar, ai, bbr, bbi = _s5_disc_call(ssm_lam_re[l], ssm_lam_im[l], ssm_log_dt[l], ssm_b_re[l], ssm_b_im[l])
        bd, cre_t, cim_t = _s5_matrices(bbr, bbi, ssm_c_re[l], ssm_c_im[l])
        uc_tb = jnp.transpose(uc, (1, 0, 2))
        ul_tb = jnp.transpose(ul, (1, 0, 2))
        h0 = jnp.zeros((2, B, 2 * SSM_N), F32)
        yfc, ybc, hc = _s5_scan_call(uc_tb, bd, ar, ai, cre_t, cim_t, h0, S5_TL)
        yfl, ybl, _ = _s5_scan_call(ul_tb, bd, ar, ai, cre_t, cim_t, hc, S5_TL)
        d_skip = ssm_d[l].reshape(1, -1)
        wglu, bglu = b16(ssm_w_glu[l]), ssm_b_glu[l].reshape(1, -1)
        flat = lambda a: a.reshape(-1, SSM_WIDTH)
        ysl = _s5_out_call(flat(ul_tb), flat(yfl), flat(ybl), d_skip, wglu, bglu, S5_ROWS_OUT)
        ysl = jnp.transpose(ysl.reshape(T, B, SSM_WIDTH), (1, 0, 2))

        s0 = jnp.zeros((B, 2, GLA_WIDTH, GLA_KW), F32)
        ofc, obc, sc_fin = _gla_call(gqc, gkc, gvc, lgfc, lgbc, s0, min(GLA_BLOCK, C))
        ofl, obl, _ = _gla_call(gql, gkl, gvl, lgfl, lgbl, sc_fin, tg)

        x = _merge_call(x, oa, ysl, ofl, obl, sh1, sc1, g1, None, n1, w, tm_merge)
        x = _ffn_call(x, sh2, sc2, g2, None, n2, wup_g, wup_u, wdown, fn, l == depth - 1, tm_ffn)
        if with_ctx:
            ysc = _s5_out_call(flat(uc_tb), flat(yfc), flat(ybc), d_skip, wglu, bglu, S5_ROWS_OUT)
            ysc = jnp.transpose(ysc.reshape(C, B, SSM_WIDTH), (1, 0, 2))
            xc = _merge_call(xc, oac, ysc, ofc, obc, sh1, sc1, g1, B, n1, w, C)
            xc = _ffn_call(xc, sh2, sc2, g2, B, n2, wup_g, wup_u, wdown, fn, False, C)
    return x
```

```python
import functools
import math

import jax
import jax.numpy as jnp
from jax import lax
from jax.experimental import pallas as pl
from jax.experimental.pallas import tpu as pltpu

F32 = jnp.float32
BF16 = jnp.bfloat16

D_MODEL = 1024
GRID_W = 64
EPS = 1e-6
MLA_HEADS = 8
MLA_RANK = 256
MLA_NOPE = 64
MLA_ROPE = 32
MLA_V = 64
MLA_WIDTH = MLA_HEADS * MLA_V
MLA_SCALE = (MLA_NOPE + MLA_ROPE) ** -0.5
ROPE_FREQS = MLA_ROPE // 4
ROPE_BASE = 10000.0
LOG2_E = math.log2(math.e)
HEAD_SLOT = 128
SSM_WIDTH = 512
SSM_GROUP = 16
SSM_GROUPS = SSM_WIDTH // SSM_GROUP
SSM_STATE = 64
GLA_HEADS = 4
GLA_DK = 64
GLA_DV = 128
GLA_KW = GLA_HEADS * GLA_DK
GLA_WIDTH = GLA_HEADS * GLA_DV
GLA_GATE_RANK = 16
GLA_GATE_TAU = 16.0
GLA_CHUNK = 64
FFN_HIDDEN = 2816
LR_PAD = 128

VMEM_LIMIT = 56 * 1024 * 1024

TM_IN = 512
TM_MERGE = 512
TM_FFN = 512
TQ = 512
TK = 512
ATT_STRIP = 64
S5_L = 16
S5_PAIRS = SSM_GROUPS // 2
S5_PW = 2 * S5_L * SSM_GROUP
GLA_BLOCK = 256


def _const_spec(shape):
    nd = len(shape)
    return pl.BlockSpec(shape, lambda *_: (0,) * nd, pipeline_mode=pl.Buffered(1))


def _params(sem):
    return pltpu.CompilerParams(dimension_semantics=sem, vmem_limit_bytes=VMEM_LIMIT)


def _rms(x):
    return x * lax.rsqrt(jnp.mean(x * x, axis=-1, keepdims=True) + EPS)


def _dot(a, b):
    return jnp.dot(a, b, preferred_element_type=F32)


def _dot_nt(a, b):
    return lax.dot_general(a, b, (((1,), (1,)), ((), ())), preferred_element_type=F32)


def _dot_tn(a, b):
    return lax.dot_general(a, b, (((0,), (0,)), ((), ())), preferred_element_type=F32)


def _mod_kernel(c_ref, w_ref, b_ref, o_ref):
    c = c_ref[...]
    s = (c * jax.nn.sigmoid(c)).astype(BF16)
    o_ref[0] = _dot(s, w_ref[0]) + b_ref[0]


def _mod_call(cvec, w_mod, b_mod):
    L, D, N = w_mod.shape
    R = cvec.shape[0]
    tn = 1536
    return pl.pallas_call(
        _mod_kernel,
        out_shape=jax.ShapeDtypeStruct((L, R, N), F32),
        grid=(L, N // tn),
        in_specs=[pl.BlockSpec((R, D), lambda l, j: (0, 0)),
                  pl.BlockSpec((1, D, tn), lambda l, j: (l, 0, j)),
                  pl.BlockSpec((1, 1, tn), lambda l, j: (l, 0, j))],
        out_specs=pl.BlockSpec((1, R, tn), lambda l, j: (l, 0, j)),
        compiler_params=_params(("arbitrary", "arbitrary")),
        name="mod",
    )(cvec, w_mod, b_mod.reshape(L, 1, N))


def _in_proj_kernel(x_ref, sh_ref, sc_ref, n1_ref, cos_ref, sin_ref,
                    wa_ref, wkr_ref, wu_ref, wg_ref, wlr_ref,
                    qn_ref, kvn_ref, wuq_ref, wuqs_ref, wukv_ref, wa2_ref, ba2_ref,
                    q_ref, k_ref, kv_ref, u_ref, gq_ref, gk_ref, gv_ref, lgf_ref, lgb_ref):
    x = x_ref[0]
    h = _rms(x) * n1_ref[...]
    h = h * (1.0 + sc_ref[0]) + sh_ref[0]
    hb = h.astype(BF16)
    cos = cos_ref[...]
    sin = sin_ref[...]
    cos8 = jnp.concatenate([cos] * MLA_HEADS, axis=1)
    sin8 = jnp.concatenate([sin] * MLA_HEADS, axis=1)

    za = _dot(hb, wa_ref[...])
    cqn = (_rms(za[:, :MLA_RANK]) * qn_ref[...]).astype(BF16)
    q = _dot(cqn, wuq_ref[...]) * cos8 + _dot(cqn, wuqs_ref[...]) * sin8
    q_ref[0] = (q * (MLA_SCALE * LOG2_E)).astype(BF16)

    ckvn = (_rms(za[:, MLA_RANK:]) * kvn_ref[...]).astype(BF16)
    kv = _dot(ckvn, wukv_ref[...])
    kr2 = _dot(hb, wkr_ref[...])
    krr = kr2[:, :HEAD_SLOT] * cos + kr2[:, HEAD_SLOT:] * sin
    nope = lax.broadcasted_iota(jnp.int32, krr.shape, 1) < MLA_NOPE
    for hd in range(MLA_HEADS):
        sl = slice(hd * HEAD_SLOT, (hd + 1) * HEAD_SLOT)
        k_ref[0, :, sl] = jnp.where(nope, kv[:, sl], krr).astype(BF16)
        kv_ref[0, :, sl] = jnp.where(nope, 1.0, kv[:, sl]).astype(BF16)

    u_ref[0] = _dot(hb, wu_ref[...])
    zg = _dot(hb, wg_ref[...])
    gq_ref[0] = zg[:, :GLA_KW]
    gk_ref[0] = zg[:, GLA_KW:2 * GLA_KW]
    gv_ref[0] = zg[:, 2 * GLA_KW:]
    zlr = _dot(hb, wlr_ref[...]).astype(BF16)
    for d, o_ref in enumerate((lgf_ref, lgb_ref)):
        z = _dot(zlr, wa2_ref[d]) + ba2_ref[d]
        o_ref[0] = (jnp.minimum(z, 0.0) - jnp.log1p(jnp.exp(-jnp.abs(z)))) * (1.0 / GLA_GATE_TAU)


def _in_proj_call(x, mod_sh, mod_sc, mod_row, n1, cos_t, sin_t, w, tm):
    Bn, Tn, D = x.shape
    nt = Tn // tm
    row = (lambda b, i: (b, 0, 0)) if mod_row is None else (lambda b, i: (mod_row, 0, 0))
    tok = lambda width: pl.BlockSpec((1, tm, width), lambda b, i: (b, i, 0))
    weights = [w["wa"], w["wkr"], w["wu"], w["wg"], w["wlr"], w["qn"], w["kvn"],
               w["wuq"], w["wuqs"], w["wukv"], w["wa2"], w["ba2"]]
    outs = [(D, BF16), (D, BF16), (D, BF16), (SSM_WIDTH, F32), (GLA_KW, F32), (GLA_KW, F32),
            (GLA_WIDTH, F32), (GLA_KW, F32), (GLA_KW, F32)]
    out_shape = [jax.ShapeDtypeStruct((Bn, Tn, wd), dt) for wd, dt in outs]
    out_specs = [tok(wd) for wd, _ in outs]
    return pl.pallas_call(
        _in_proj_kernel,
        out_shape=out_shape,
        grid=(Bn, nt),
        in_specs=[tok(D),
                  pl.BlockSpec((1, 1, D), row), pl.BlockSpec((1, 1, D), row),
                  _const_spec((1, D)),
                  pl.BlockSpec((tm, HEAD_SLOT), lambda b, i: (i, 0)),
                  pl.BlockSpec((tm, HEAD_SLOT), lambda b, i: (i, 0))]
                 + [_const_spec(a.shape) for a in weights],
        out_specs=out_specs,
        compiler_params=_params(("parallel", "parallel")),
        name="in_proj",
    )(x, mod_sh, mod_sc, n1, cos_t, sin_t, *weights)


def _lane_fold(x, op):
    out = x[:, :HEAD_SLOT]
    for j in range(1, x.shape[1] // HEAD_SLOT):
        out = op(out, x[:, j * HEAD_SLOT:(j + 1) * HEAD_SLOT])
    return out


def _attn_update(s, kvt, carry):
    m, acc = carry
    m_new = jnp.maximum(m, jnp.max(_lane_fold(s, jnp.maximum), axis=-1, keepdims=True))
    p = jnp.exp2(s - m_new)
    acc = jnp.exp2(m - m_new) * acc + _dot(p.astype(BF16), kvt)
    return m_new, acc


def _attn_kernel(*refs, n_lat_tiles, tk):
    if n_lat_tiles:
        q_ref, kc_ref, kvc_ref, kl_ref, kvl_ref, o_ref, s_ref, mx_ref, p_ref, m_ref, acc_ref = refs
    else:
        q_ref, kc_ref, kvc_ref, o_ref = refs
    tq = q_ref.shape[1]
    sls = [slice(hh * HEAD_SLOT, (hh + 1) * HEAD_SLOT) for hh in range(2)]
    qs = [q_ref[0, :, sl] for sl in sls]
    init = (jnp.full((tq, 1), -jnp.inf, F32), jnp.zeros((tq, HEAD_SLOT), F32))
    carry = tuple(_attn_update(_dot_nt(qs[hh], kc_ref[0, :, sls[hh]]), kvc_ref[0, :, sls[hh]], init)
                  for hh in range(2))
    if n_lat_tiles:
        for hh in range(2):
            m, acc = carry[hh]
            m_ref[hh] = jnp.broadcast_to(m, (tq, HEAD_SLOT))
            acc_ref[hh] = acc
        rows = lambda i: pl.ds(pl.multiple_of(i * tk, tk), tk)

        def scores(i, slot):
            for hh in range(2):
                s = _dot_nt(qs[hh], kl_ref[0, rows(i), sls[hh]])
                s_ref[slot, hh] = s
                mx_ref[slot, hh] = _lane_fold(s, jnp.maximum)

        def update(i, slot):
            for hh in range(2):
                for r in range(0, tq, ATT_STRIP):
                    rs = slice(r, r + ATT_STRIP)
                    m_old = m_ref[hh, rs, :]
                    m_new = jnp.maximum(m_old, jnp.max(mx_ref[slot, hh, rs, :], axis=-1, keepdims=True))
                    p = jnp.exp2(s_ref[slot, hh, rs, :] - jnp.concatenate([m_new] * (tk // HEAD_SLOT), axis=1))
                    m_ref[hh, rs, :] = m_new
                    acc_ref[hh, rs, :] = jnp.exp2(m_old - m_new) * acc_ref[hh, rs, :]
                    p_ref[hh, rs, :] = p.astype(BF16)
                acc_ref[hh] += _dot(p_ref[hh], kvl_ref[0, rows(i), sls[hh]])

        scores(0, 0)

        def body(j, c):
            i = 2 * j
            scores(i + 1, 1)
            update(i, 0)
            scores(jnp.minimum(i + 2, n_lat_tiles - 1), 0)
            update(i + 1, 1)
            return c

        lax.fori_loop(0, n_lat_tiles // 2, body, 0)
        carry = tuple((None, acc_ref[hh]) for hh in range(2))
    outs = [(acc * pltpu.roll(1.0 / acc, MLA_NOPE, 1))[:, MLA_NOPE:] for _, acc in carry]
    o_ref[0] = jnp.concatenate(outs, axis=1).astype(o_ref.dtype)


def _attn_call(q, k_ctx, kv_ctx, k_lat, kv_lat, tq, tk):
    Bn, Tq, D = q.shape
    C = k_ctx.shape[1]
    hp = MLA_HEADS // 2
    w2 = 2 * HEAD_SLOT
    n_lat_tiles = 0 if k_lat is None else k_lat.shape[1] // tk
    in_specs = [pl.BlockSpec((1, tq, w2), lambda b, h, i: (b, i, h)),
                pl.BlockSpec((1, C, w2), lambda b, h, i: (b, 0, h)),
                pl.BlockSpec((1, C, w2), lambda b, h, i: (b, 0, h))]
    args = [q, k_ctx, kv_ctx]
    scratch = []
    if n_lat_tiles:
        assert n_lat_tiles % 2 == 0 and tq % ATT_STRIP == 0
        Tk = k_lat.shape[1]
        in_specs += [pl.BlockSpec((1, Tk, w2), lambda b, h, i: (b, 0, h)),
                     pl.BlockSpec((1, Tk, w2), lambda b, h, i: (b, 0, h))]
        args += [k_lat, kv_lat]
        stat = pltpu.VMEM((2, tq, HEAD_SLOT), F32)
        scratch = [pltpu.VMEM((2, 2, tq, tk), F32), pltpu.VMEM((2, 2, tq, HEAD_SLOT), F32),
                   pltpu.VMEM((2, tq, tk), BF16), stat, stat]
    return pl.pallas_call(
        functools.partial(_attn_kernel, n_lat_tiles=n_lat_tiles, tk=tk),
        out_shape=jax.ShapeDtypeStruct((Bn, Tq, MLA_WIDTH), BF16),
        grid=(Bn, hp, Tq // tq),
        in_specs=in_specs,
        out_specs=pl.BlockSpec((1, tq, 2 * MLA_V), lambda b, h, i: (b, i, h)),
        scratch_shapes=scratch,
        compiler_params=_params(("parallel", "parallel", "parallel")),
        name="attn",
    )(*args)


def _s5_prep_kernel(lr_ref, li_ref, ldt_ref, br_ref, bi_ref, cr_ref, ci_ref, ctr_ref, cti_ref,
                    k_ref, m2r_ref, m2i_ref, nr_ref, nin_ref, a16r_ref, a16i_ref):
    lr = lr_ref[0, 0]
    li = li_ref[0, 0]
    dt = jnp.exp(ldt_ref[0, 0])
    rho = dt * lr
    theta = dt * li
    jf = (lax.broadcasted_iota(jnp.int32, lr.shape, 1) // SSM_GROUP).astype(F32)

    def power(j):
        mag = jnp.exp(j * rho)
        return mag * jnp.cos(j * theta), mag * jnp.sin(j * theta)

    ar, ai = power(1.0)
    den = lr * lr + li * li
    fr = ((ar - 1.0) * lr + ai * li) / den
    fi = (ai * lr - (ar - 1.0) * li) / den
    br = br_ref[0]
    bi = bi_ref[0]
    bbr = fr * br - fi * bi
    bbi = fr * bi + fi * br
    pr, pi = power(jf)
    m2r = pr * bbr - pi * bbi
    m2i = pr * bbi + pi * bbr
    m2r_ref[0, 0] = m2r
    m2i_ref[0, 0] = m2i
    hp = lax.Precision.HIGHEST
    k_ref[0, 0] = (jnp.dot(cr_ref[0], m2r, precision=hp, preferred_element_type=F32)
                   - jnp.dot(ci_ref[0], m2i, precision=hp, preferred_element_type=F32))
    qr, qi = power(jf + 1.0)
    ctr = ctr_ref[0]
    cti = cti_ref[0]
    nr_ref[0, 0] = ctr * qr - cti * qi
    nin_ref[0, 0] = -(ctr * qi + cti * qr)
    a16r_ref[0, 0] = qr
    a16i_ref[0, 0] = qi


def _s5_prep_call(lam_re, lam_im, log_dt, b_re, b_im, c_re, c_im):
    G, P, H, L = SSM_GROUPS, SSM_STATE, SSM_GROUP, S5_L
    W = L * H
    lanes = lambda a: jnp.broadcast_to(a[..., None], a.shape + (W,))
    lr, li = lanes(lam_re), lanes(lam_im)
    ldt = jnp.broadcast_to(log_dt[:, :, None, None], (2, G, P, W))
    br, bi = jnp.tile(b_re, (1, 1, L)), jnp.tile(b_im, (1, 1, L))
    ctr = jnp.tile(jnp.transpose(c_re, (0, 2, 1)), (1, 1, L))
    cti = jnp.tile(jnp.transpose(c_im, (0, 2, 1)), (1, 1, L))
    dg = pl.BlockSpec((1, 1, P, W), lambda d, g: (d, g, 0, 0))
    gp = pl.BlockSpec((1, P, W), lambda d, g: (g, 0, 0))
    gc = pl.BlockSpec((1, H, P), lambda d, g: (g, 0, 0))
    big = jax.ShapeDtypeStruct((2, G, P, W), F32)
    return pl.pallas_call(
        _s5_prep_kernel,
        out_shape=[jax.ShapeDtypeStruct((2, G, H, W), F32), big, big, big, big, big, big],
        grid=(2, G),
        in_specs=[dg, dg, dg, gp, gp, gc, gc, gp, gp],
        out_specs=[pl.BlockSpec((1, 1, H, W), lambda d, g: (d, g, 0, 0)), dg, dg, dg, dg, dg, dg],
        compiler_params=_params(("arbitrary", "arbitrary")),
        name="s5_prep",
    )(lr, li, ldt, br, bi, c_re, c_im, ctr, cti)


def _s5_assemble(k, m2r, m2i, nr, nin, a16r, a16i):
    G, P, H, L, Q = SSM_GROUPS, SSM_STATE, SSM_GROUP, S5_L, S5_PAIRS
    s = jnp.arange(L)
    k4 = k.reshape(2, G, H, L, H)

    def toeplitz(kd, diff, keep):
        t4 = kd[:, :, jnp.clip(diff, 0, L - 1), :]
        t4 = jnp.where(keep[None, None, :, :, None], t4, 0.0)
        return jnp.transpose(t4, (0, 2, 4, 3, 1)).reshape(G, L * H, L * H)

    dts = s[None, :] - s[:, None]
    tmat = jnp.stack([toeplitz(k4[0], dts, dts >= 0), toeplitz(k4[1], -dts, dts <= 0)])
    tmat = tmat.reshape(2, Q, 2, L * H, L * H).astype(BF16)
    eye2 = jnp.eye(2, dtype=F32)

    def drive(m2, d):
        m4 = m2[d].reshape(G, P, L, H)
        m4 = m4[:, :, ::-1, :] if d == 0 else m4
        x = jnp.transpose(m4, (0, 2, 3, 1)).reshape(Q, 2, L * H, P)
        return (x[:, :, :, None, :] * eye2[None, :, None, :, None]).reshape(Q, 2 * L * H, 2 * P)

    def read(n, d):
        n4 = n[d].reshape(G, P, L, H)
        n4 = n4[:, :, ::-1, :] if d == 1 else n4
        x = n4.reshape(Q, 2, P, L * H)
        return (x[:, :, :, None, :] * eye2[None, :, None, :, None]).reshape(Q, 2 * P, 2 * L * H)

    pmat = jnp.concatenate([drive(m, d) for d in range(2) for m in (m2r, m2i)], axis=2).astype(BF16)
    qmat = jnp.concatenate([read(n, d) for d in range(2) for n in (nr, nin)], axis=1).astype(BF16)
    last = (L - 1) * H
    a16 = jnp.stack([a16r[..., last], a16i[..., last]], axis=1)
    a16 = a16.reshape(2, 2, Q, 1, 2 * P).transpose(2, 0, 1, 3, 4)
    return tmat, pmat, qmat, a16


def _s5_chunk_kernel(u_ref, t_ref, p_ref, q_ref, a_ref, y_ref, d_ref, *, nc_ctx, nb, n_blocks):
    R = u_ref.shape[1]
    nc = R // nb
    hw = S5_PW // 2
    rb = R // n_blocks
    w = 2 * SSM_STATE
    for r in range(n_blocks):
        rows = slice(r * rb, (r + 1) * rb)
        d_ref[rows, :] = _dot(u_ref[0, rows, :], p_ref[0])

    af_r, af_i, ab_r, ab_i = [jnp.broadcast_to(a_ref[0, d, c], (nb, w)) for d in range(2) for c in range(2)]

    def step(k, carry):
        fr, fi, br, bi = carry
        rf = pl.ds(pl.multiple_of(k * nb, nb), nb)
        cb = jnp.where(k < nc_ctx, nc_ctx - 1 - k, nc - 1 - (k - nc_ctx))
        rbk = pl.ds(pl.multiple_of(cb * nb, nb), nb)
        dfr, dfi = d_ref[rf, 0:w], d_ref[rf, w:2 * w]
        dbr, dbi = d_ref[rbk, 2 * w:3 * w], d_ref[rbk, 3 * w:4 * w]
        d_ref[rf, 0:w] = fr
        d_ref[rf, w:2 * w] = fi
        d_ref[rbk, 2 * w:3 * w] = br
        d_ref[rbk, 3 * w:4 * w] = bi
        return (af_r * fr - af_i * fi + dfr, af_r * fi + af_i * fr + dfi,
                ab_r * br - ab_i * bi + dbr, ab_r * bi + ab_i * br + dbi)

    z = jnp.zeros((nb, w), F32)
    lax.fori_loop(0, nc, step, (z, z, z, z))

    for r in range(n_blocks):
        rows = slice(r * rb, (r + 1) * rb)
        y = _dot(d_ref[rows, :].astype(BF16), q_ref[0])
        intra = []
        for g2 in range(2):
            ug = u_ref[0, rows, g2 * hw:(g2 + 1) * hw]
            intra.append(_dot(ug, t_ref[0, 0, g2]) + _dot(ug, t_ref[1, 0, g2]))
        y_ref[0, rows, :] = y + jnp.concatenate(intra, axis=1)


def _s5_chunk_call(ug, tmat, pmat, qmat, a16, nc_ctx, nb):
    Q, R, PW = ug.shape
    n_blocks = 4
    assert R % (n_blocks * nb) == 0
    blk = lambda a: pl.BlockSpec((1,) + a.shape[1:], lambda q: (q,) + (0,) * (a.ndim - 1))
    return pl.pallas_call(
        functools.partial(_s5_chunk_kernel, nc_ctx=nc_ctx, nb=nb, n_blocks=n_blocks),
        out_shape=jax.ShapeDtypeStruct((Q, R, PW), F32),
        grid=(Q,),
        in_specs=[blk(ug), pl.BlockSpec((2, 1, 2, PW // 2, PW // 2), lambda q: (0, q, 0, 0, 0)),
                  blk(pmat), blk(qmat), blk(a16)],
        out_specs=pl.BlockSpec((1, R, PW), lambda q: (q, 0, 0)),
        scratch_shapes=[pltpu.VMEM((R, 8 * SSM_STATE), F32)],
        compiler_params=_params(("parallel",)),
        name="s5_chunk",
    )(ug, tmat, pmat, qmat, a16)


def _s5_mix(uc, ul, lam_re, lam_im, log_dt, b_re, b_im, c_re, c_im):
    B, C, W = uc.shape
    T = ul.shape[1]
    L, Q, H = S5_L, S5_PAIRS, SSM_GROUP
    tmat, pmat, qmat, a16 = _s5_assemble(*_s5_prep_call(lam_re, lam_im, log_dt, b_re, b_im, c_re, c_im))
    nc = (C + T) // L
    u = jnp.concatenate([uc, ul], axis=1).astype(BF16)
    ug = u.reshape(B, nc, L, Q, 2, H).transpose(3, 1, 0, 4, 2, 5).reshape(Q, nc * B, S5_PW)
    y = _s5_chunk_call(ug, tmat, pmat, qmat, a16, C // L, B)
    y = y.reshape(Q, nc, B, 2, L, H).transpose(2, 1, 4, 0, 3, 5).reshape(B, C + T, W)
    return y[:, :C], y[:, C:]


def _split3(x):
    hi = x.astype(BF16)
    r1 = x - hi.astype(F32)
    mid = r1.astype(BF16)
    lo = (r1 - mid.astype(F32)).astype(BF16)
    return hi, mid, lo


def _gla_block(q, k, v, lg, s, d, cst, n_chunks):
    tri_ref, mask_ref, same_ref, hmask_ref, bmask_ref = cst
    L = GLA_CHUNK
    parts = _split3(lg)
    b = sum(_dot(tri_ref[d], p) for p in parts)
    tot = sum(_dot(same_ref[...], p) for p in parts)
    q_in = q * (GLA_DK ** -0.5) * jnp.exp(b)
    k_in = (k * jnp.exp(-b)).astype(BF16)
    k_dec = (k * jnp.exp(tot - b)).astype(BF16)
    vb = v.astype(BF16)
    mask = mask_ref[d] > 0.0
    outs = []
    for hd in range(GLA_HEADS):
        qh = (q_in * hmask_ref[hd]).astype(BF16)
        att = jnp.where(mask, _dot_nt(qh, k_in), 0.0).astype(BF16)
        outs.append(_dot(att, vb[:, hd * GLA_DV:(hd + 1) * GLA_DV]))
    o_intra = jnp.concatenate(outs, axis=1)
    q_in_b = q_in.astype(BF16)
    bmask = bmask_ref[...]
    o_rows = [None] * n_chunks
    for c in (range(n_chunks) if d == 0 else reversed(range(n_chunks))):
        rows = slice(c * L, (c + 1) * L)
        o_rows[c] = _dot_nt(q_in_b[rows], s.astype(BF16))
        dec = jnp.exp(tot[c * L:c * L + 1, :])
        s = dec * s + _dot_tn(vb[rows], k_dec[rows]) * bmask
    return o_intra + jnp.concatenate(o_rows, axis=0), s


def _gla_kernel(qf_ref, kf_ref, vf_ref, lgf_ref, qb_ref, kb_ref, vb_ref, lgb_ref, s0_ref,
                tri_ref, mask_ref, same_ref, hmask_ref, bmask_ref,
                of_ref, ob_ref, sfin_ref, s_ref, *, n_chunks):
    i = pl.program_id(1)
    cst = (tri_ref, mask_ref, same_ref, hmask_ref, bmask_ref)

    @pl.when(i == 0)
    def _():
        s_ref[...] = s0_ref[0]

    o, s = _gla_block(qf_ref[0], kf_ref[0], vf_ref[0], lgf_ref[0], s_ref[0], 0, cst, n_chunks)
    of_ref[0] = o
    s_ref[0] = s
    o, s = _gla_block(qb_ref[0], kb_ref[0], vb_ref[0], lgb_ref[0], s_ref[1], 1, cst, n_chunks)
    ob_ref[0] = o
    s_ref[1] = s

    @pl.when(i == pl.num_programs(1) - 1)
    def _():
        sfin_ref[0] = s_ref[...]


def _gla_consts(tg):
    t = jnp.arange(tg)
    same = (t[:, None] // GLA_CHUNK) == (t[None, :] // GLA_CHUNK)
    low = same & (t[None, :] <= t[:, None])
    upp = same & (t[None, :] >= t[:, None])
    tri = jnp.stack([low, upp])
    hmask = (jnp.arange(GLA_KW)[None, None, :] // GLA_DK) == jnp.arange(GLA_HEADS)[:, None, None]
    bmask = (jnp.arange(GLA_WIDTH)[:, None] // GLA_DV) == (jnp.arange(GLA_KW)[None, :] // GLA_DK)
    return tri.astype(BF16), tri.astype(F32), same.astype(BF16), hmask.astype(F32), bmask.astype(F32)


def _gla_call(q, k, v, lgf, lgb, s0, tg):
    Bn, Tn, _ = q.shape
    nblk = Tn // tg
    consts = _gla_consts(tg)
    fwd = lambda w: pl.BlockSpec((1, tg, w), lambda b, i: (b, i, 0))
    bwd = lambda w: pl.BlockSpec((1, tg, w), lambda b, i: (b, nblk - 1 - i, 0))
    st = pl.BlockSpec((1, 2, GLA_WIDTH, GLA_KW), lambda b, i: (b, 0, 0, 0))
    return pl.pallas_call(
        functools.partial(_gla_kernel, n_chunks=tg // GLA_CHUNK),
        out_shape=[jax.ShapeDtypeStruct((Bn, Tn, GLA_WIDTH), F32), jax.ShapeDtypeStruct((Bn, Tn, GLA_WIDTH), F32),
                   jax.ShapeDtypeStruct((Bn, 2, GLA_WIDTH, GLA_KW), F32)],
        grid=(Bn, nblk),
        in_specs=[fwd(GLA_KW), fwd(GLA_KW), fwd(GLA_WIDTH), fwd(GLA_KW),
                  bwd(GLA_KW), bwd(GLA_KW), bwd(GLA_WIDTH), bwd(GLA_KW), st]
                 + [_const_spec(a.shape) for a in consts],
        out_specs=[fwd(GLA_WIDTH), bwd(GLA_WIDTH), st],
        scratch_shapes=[pltpu.VMEM((2, GLA_WIDTH, GLA_KW), F32)],
        compiler_params=_params(("parallel", "arbitrary")),
        name="gla",
    )(q, k, v, lgf, q, k, v, lgb, s0, *consts)


def _merge_kernel(x_ref, oa_ref, u_ref, ys_ref, of_ref, ob_ref, sh_ref, sc_ref, g1_ref, n1_ref,
                  wgate_ref, wr_ref, wmo_ref, dsk_ref, wglu_ref, bglu_ref, wso_ref, gn_ref, wgo_ref, wout_ref, o_ref):
    x = x_ref[0]
    h = _rms(x) * n1_ref[...]
    hb = (h * (1.0 + sc_ref[0]) + sh_ref[0]).astype(BF16)
    ya = _dot(oa_ref[0], wmo_ref[...])
    sg = jax.nn.gelu(dsk_ref[...] * u_ref[0] + ys_ref[0])
    sg = sg * jax.nn.sigmoid(_dot(sg.astype(BF16), wglu_ref[...]) + bglu_ref[...])
    yb = _dot(sg.astype(BF16), wso_ref[...])
    o = of_ref[0] + ob_ref[0]
    on = jnp.concatenate([_rms(o[:, hd * GLA_DV:(hd + 1) * GLA_DV]) for hd in range(GLA_HEADS)], axis=1)
    r = _dot(hb, wr_ref[...])
    yg = _dot(((on * gn_ref[...]) * (r * jax.nn.sigmoid(r))).astype(BF16), wgo_ref[...])
    m = jnp.zeros_like(ya)
    for j, br in enumerate((ya, yb, yg)):
        gate = jax.nn.sigmoid(_dot(hb, wgate_ref[:, j * D_MODEL:(j + 1) * D_MODEL]))
        m = m + gate * br
    o_ref[0] = x + g1_ref[0] * _dot(m.astype(BF16), wout_ref[...])


def _merge_call(x, oa, u, ys, of, ob, mod_sh, mod_sc, mod_g, mod_row, n1, w, tm):
    Bn, Tn, D = x.shape
    row = (lambda b, i: (b, 0, 0)) if mod_row is None else (lambda b, i: (mod_row, 0, 0))
    tok = lambda width: pl.BlockSpec((1, tm, width), lambda b, i: (b, i, 0))
    mrow = pl.BlockSpec((1, 1, D), row)
    weights = [w["wgate"], w["wr"], w["wmo"], w["dsk"], w["wglu"], w["bglu"], w["wso"], w["gn"], w["wgo"],
               w["wout"]]
    return pl.pallas_call(
        _merge_kernel,
        out_shape=jax.ShapeDtypeStruct((Bn, Tn, D), F32),
        grid=(Bn, Tn // tm),
        in_specs=[tok(D), tok(MLA_WIDTH), tok(SSM_WIDTH), tok(SSM_WIDTH), tok(GLA_WIDTH), tok(GLA_WIDTH),
                  mrow, mrow, mrow, _const_spec((1, D))] + [_const_spec(a.shape) for a in weights],
        out_specs=tok(D),
        compiler_params=_params(("parallel", "parallel")),
        name="merge",
    )(x, oa, u, ys, of, ob, mod_sh, mod_sc, mod_g, n1, *weights)


def _ffn_kernel(x_ref, sh_ref, sc_ref, g2_ref, n2_ref, wg_ref, wu_ref, wd_ref, fn_ref, o_ref, *, final):
    x = x_ref[0]
    h = _rms(x) * n2_ref[...]
    hb = (h * (1.0 + sc_ref[0]) + sh_ref[0]).astype(BF16)
    acc = jnp.zeros_like(x)
    half = FFN_HIDDEN // 2
    for j in range(2):
        cs = slice(j * half, (j + 1) * half)
        g = _dot(hb, wg_ref[:, cs])
        u = _dot(hb, wu_ref[:, cs])
        acc = acc + _dot(((g * jax.nn.sigmoid(g)) * u).astype(BF16), wd_ref[cs, :])
    y = x + g2_ref[0] * acc
    if final:
        y = _rms(y) * fn_ref[...]
    o_ref[0] = y


def _ffn_call(x, mod_sh, mod_sc, mod_g, mod_row, n2, wg, wu, wd, fn, final, tm):
    Bn, Tn, D = x.shape
    row = (lambda b, i: (b, 0, 0)) if mod_row is None else (lambda b, i: (mod_row, 0, 0))
    tok = pl.BlockSpec((1, tm, D), lambda b, i: (b, i, 0))
    mrow = pl.BlockSpec((1, 1, D), row)
    return pl.pallas_call(
        functools.partial(_ffn_kernel, final=final),
        out_shape=jax.ShapeDtypeStruct((Bn, Tn, D), F32),
        grid=(Bn, Tn // tm),
        in_specs=[tok, mrow, mrow, mrow, _const_spec((1, D)), _const_spec(wg.shape), _const_spec(wu.shape),
                  _const_spec(wd.shape), _const_spec((1, D))],
        out_specs=tok,
        compiler_params=_params(("parallel", "parallel")),
        name="ffn",
    )(x, mod_sh, mod_sc, mod_g, n2, wg, wu, wd, fn)


def _pair_swap(w):
    w2 = w.reshape(w.shape[0], -1, 2)
    return jnp.stack([-w2[..., 1], w2[..., 0]], axis=-1).reshape(w.shape)


def _layer_weights(l, w_in, mla_q_norm, mla_w_uq, mla_kv_norm, mla_w_ukv, gla_w_a2, gla_b_a2):
    wi = w_in[l]
    D = wi.shape[0]
    o = 0
    cuts = {}
    for name, sz in (("cq", MLA_RANK), ("ckv", MLA_RANK), ("kr", MLA_ROPE), ("u", SSM_WIDTH), ("gq", GLA_KW),
                     ("gk", GLA_KW), ("gv", GLA_WIDTH), ("gr", GLA_WIDTH), ("af", GLA_GATE_RANK),
                     ("ab", GLA_GATE_RANK), ("gate", 3 * D_MODEL)):
        cuts[name] = wi[:, o:o + sz]
        o += sz
    zeros = lambda n: jnp.zeros((D, n), F32)
    kr = cuts["kr"]
    pad_k = HEAD_SLOT - MLA_NOPE - MLA_ROPE
    wkr = jnp.concatenate([zeros(MLA_NOPE), kr, zeros(pad_k), zeros(MLA_NOPE), _pair_swap(kr), zeros(pad_k)], axis=1)
    wlr = jnp.concatenate([cuts["af"], cuts["ab"], zeros(LR_PAD - 2 * GLA_GATE_RANK)], axis=1)
    uq = mla_w_uq[l].reshape(MLA_RANK, MLA_HEADS, MLA_NOPE + MLA_ROPE)
    zq = jnp.zeros((MLA_RANK, MLA_HEADS, pad_k), F32)
    wuq = jnp.concatenate([uq, zq], axis=2).reshape(MLA_RANK, MLA_HEADS * HEAD_SLOT)
    uq_sw = _pair_swap(uq[..., MLA_NOPE:].reshape(MLA_RANK, -1)).reshape(MLA_RANK, MLA_HEADS, MLA_ROPE)
    wuqs = jnp.concatenate([jnp.zeros((MLA_RANK, MLA_HEADS, MLA_NOPE), F32), uq_sw, zq], axis=2)
    wuqs = wuqs.reshape(MLA_RANK, MLA_HEADS * HEAD_SLOT)
    wa2 = jnp.zeros((2, LR_PAD, GLA_KW), F32)
    wa2 = wa2.at[0, :GLA_GATE_RANK].set(gla_w_a2[l, 0]).at[1, GLA_GATE_RANK:2 * GLA_GATE_RANK].set(gla_w_a2[l, 1])
    b16 = lambda a: a.astype(BF16)
    return dict(
        wa=b16(jnp.concatenate([cuts["cq"], cuts["ckv"]], axis=1)), wkr=b16(wkr), wu=b16(cuts["u"]),
        wg=b16(jnp.concatenate([cuts["gq"], cuts["gk"], cuts["gv"]], axis=1)), wlr=b16(wlr),
        qn=mla_q_norm[l].reshape(1, -1), kvn=mla_kv_norm[l].reshape(1, -1),
        wuq=b16(wuq), wuqs=b16(wuqs), wukv=b16(mla_w_ukv[l]), wa2=b16(wa2),
        ba2=gla_b_a2[l].reshape(2, 1, GLA_KW), wr=b16(cuts["gr"]), wgate=b16(cuts["gate"]))


def _rope_tables(T):
    t = jnp.arange(T, dtype=jnp.int32)
    rows = (t // GRID_W).astype(F32)
    cols = (t % GRID_W).astype(F32)
    inv = ROPE_BASE ** (-jnp.arange(ROPE_FREQS, dtype=F32) / ROPE_FREQS)
    ang = jnp.concatenate([rows[:, None] * inv, cols[:, None] * inv], axis=-1)
    ang2 = jnp.repeat(ang, 2, axis=1)
    pad_k = HEAD_SLOT - MLA_NOPE - MLA_ROPE
    cos = jnp.concatenate([jnp.ones((T, MLA_NOPE), F32), jnp.cos(ang2), jnp.ones((T, pad_k), F32)], axis=1)
    sin = jnp.concatenate([jnp.zeros((T, MLA_NOPE), F32), jnp.sin(ang2), jnp.zeros((T, pad_k), F32)], axis=1)
    return cos, sin


def kernel(x, c, ctx, c_ctx, w_mod, b_mod, norm1, norm2, w_in, mla_q_norm, mla_w_uq, mla_kv_norm, mla_w_ukv, mla_w_o, ssm_lam_re, ssm_lam_im, ssm_log_dt, ssm_b_re, ssm_b_im, ssm_c_re, ssm_c_im, ssm_d, ssm_w_glu, ssm_b_glu, ssm_w_o, gla_w_a2, gla_b_a2, gla_norm, gla_w_o, w_out, ffn_w_up, ffn_w_down, final_norm):
    B, T, D = x.shape
    C = ctx.shape[1]
    depth = w_mod.shape[0]
    n_rows = -(-(B + 1) // 8) * 8
    cvec = jnp.zeros((n_rows, D), F32).at[:B].set(c).at[B].set(c_ctx)
    mod = _mod_call(cvec, w_mod.astype(BF16), b_mod)
    cos_l, sin_l = _rope_tables(T)
    cos_c = jnp.concatenate([jnp.ones((C, HEAD_SLOT), F32)], axis=0)
    sin_c = jnp.zeros((C, HEAD_SLOT), F32)
    tm_in, tm_merge, tm_ffn = min(TM_IN, T), min(TM_MERGE, T), min(TM_FFN, T)
    tq, tk = min(TQ, T), min(TK, T // 2)
    tg = min(GLA_BLOCK, T)
    b16 = lambda a: a.astype(BF16)

    xc = ctx
    for l in range(depth):
        with_ctx = l < depth - 1
        m6 = [mod[l, :, j * D:(j + 1) * D].reshape(n_rows, 1, D) for j in range(6)]
        sh1, sc1, g1, sh2, sc2, g2 = m6
        n1 = norm1[l].reshape(1, D)
        n2 = norm2[l].reshape(1, D)
        w = _layer_weights(l, w_in, mla_q_norm, mla_w_uq, mla_kv_norm, mla_w_ukv, gla_w_a2, gla_b_a2)
        w.update(wmo=b16(mla_w_o[l]), wso=b16(ssm_w_o[l]), gn=gla_norm[l].reshape(1, -1), wgo=b16(gla_w_o[l]),
                 wout=b16(w_out[l]), dsk=ssm_d[l].reshape(1, -1), wglu=b16(ssm_w_glu[l]),
                 bglu=ssm_b_glu[l].reshape(1, -1))
        up = ffn_w_up[l]
        wup_g, wup_u, wdown = b16(up[:, :FFN_HIDDEN]), b16(up[:, FFN_HIDDEN:]), b16(ffn_w_down[l])
        fn = final_norm.reshape(1, D)

        qc, kc, kvc, uc, gqc, gkc, gvc, lgfc, lgbc = _in_proj_call(xc, sh1, sc1, B, n1, cos_c, sin_c, w, C)
        ql, kl, kvl, ul, gql, gkl, gvl, lgfl, lgbl = _in_proj_call(x, sh1, sc1, None, n1, cos_l, sin_l, w, tm_in)

        oa = _attn_call(ql, kc, kvc, kl, kvl, tq, tk)
        oac = _attn_call(qc, kc, kvc, None, None, C, tk) if with_ctx else None

        ysc, ysl = _s5_mix(uc, ul, ssm_lam_re[l], ssm_lam_im[l], ssm_log_dt[l], ssm_b_re[l], ssm_b_im[l],
                           ssm_c_re[l], ssm_c_im[l])

        s0 = jnp.zeros((B, 2, GLA_WIDTH, GLA_KW), F32)
        ofc, obc, sc_fin = _gla_call(gqc, gkc, gvc, lgfc, lgbc, s0, min(GLA_BLOCK, C))
        ofl, obl, _ = _gla_call(gql, gkl, gvl, lgfl, lgbl, sc_fin, tg)

        x = _merge_call(x, oa, ul, ysl, ofl, obl, sh1, sc1, g1, None, n1, w, tm_merge)
        x = _ffn_call(x, sh2, sc2, g2, None, n2, wup_g, wup_u, wdown, fn, l == depth - 1, tm_ffn)
        if with_ctx:
            xc = _merge_call(xc, oac, uc, ysc, ofc, obc, sh1, sc1, g1, B, n1, w, C)
            xc = _ffn_call(xc, sh2, sc2, g2, B, n2, wup_g, wup_u, wdown, fn, False, C)
    return x
```

```python
import functools
import math

import jax
import jax.numpy as jnp
from jax import lax
from jax.experimental import pallas as pl
from jax.experimental.pallas import tpu as pltpu

F32 = jnp.float32
BF16 = jnp.bfloat16

D_MODEL = 1024
GRID_W = 64
EPS = 1e-6
MLA_HEADS = 8
MLA_RANK = 256
MLA_NOPE = 64
MLA_ROPE = 32
MLA_V = 64
MLA_WIDTH = MLA_HEADS * MLA_V
MLA_SCALE = (MLA_NOPE + MLA_ROPE) ** -0.5
ROPE_FREQS = MLA_ROPE // 4
ROPE_BASE = 10000.0
LOG2_E = math.log2(math.e)
HEAD_SLOT = 128
SSM_WIDTH = 512
SSM_GROUP = 16
SSM_GROUPS = SSM_WIDTH // SSM_GROUP
SSM_STATE = 64
SSM_N = SSM_GROUPS * SSM_STATE
GLA_HEADS = 4
GLA_DK = 64
GLA_DV = 128
GLA_KW = GLA_HEADS * GLA_DK
GLA_WIDTH = GLA_HEADS * GLA_DV
GLA_GATE_RANK = 16
GLA_GATE_TAU = 16.0
GLA_CHUNK = 64
FFN_HIDDEN = 2816
LR_PAD = 128

VMEM_LIMIT = 56 * 1024 * 1024

TM_IN = 512
TM_MERGE = 512
TM_FFN = 512
TQ = 512
TK = 1024
ATT_STRIP = 64
S5_TL = 64
S5_ROWS_OUT = 2048
GLA_BLOCK = 256


def _const_spec(shape):
    nd = len(shape)
    return pl.BlockSpec(shape, lambda *_: (0,) * nd, pipeline_mode=pl.Buffered(1))


def _params(sem):
    return pltpu.CompilerParams(dimension_semantics=sem, vmem_limit_bytes=VMEM_LIMIT)


def _rms(x):
    return x * lax.rsqrt(jnp.mean(x * x, axis=-1, keepdims=True) + EPS)


def _dot(a, b):
    return jnp.dot(a, b, preferred_element_type=F32)


def _dot_nt(a, b):
    return lax.dot_general(a, b, (((1,), (1,)), ((), ())), preferred_element_type=F32)


def _dot_tn(a, b):
    return lax.dot_general(a, b, (((0,), (0,)), ((), ())), preferred_element_type=F32)


def _mod_kernel(c_ref, w_ref, b_ref, o_ref):
    c = c_ref[...]
    s = (c * jax.nn.sigmoid(c)).astype(BF16)
    o_ref[0] = _dot(s, w_ref[0]) + b_ref[0]


def _mod_call(cvec, w_mod, b_mod):
    L, D, N = w_mod.shape
    R = cvec.shape[0]
    tn = 1536
    return pl.pallas_call(
        _mod_kernel,
        out_shape=jax.ShapeDtypeStruct((L, R, N), F32),
        grid=(L, N // tn),
        in_specs=[pl.BlockSpec((R, D), lambda l, j: (0, 0)),
                  pl.BlockSpec((1, D, tn), lambda l, j: (l, 0, j)),
                  pl.BlockSpec((1, 1, tn), lambda l, j: (l, 0, j))],
        out_specs=pl.BlockSpec((1, R, tn), lambda l, j: (l, 0, j)),
        compiler_params=_params(("arbitrary", "arbitrary")),
        name="mod",
    )(cvec, w_mod, b_mod.reshape(L, 1, N))


def _in_proj_kernel(x_ref, sh_ref, sc_ref, n1_ref, cos_ref, sin_ref,
                    wa_ref, wkr_ref, wu_ref, wg_ref, wlr_ref,
                    qn_ref, kvn_ref, wuq_ref, wuqs_ref, wukv_ref, wa2_ref, ba2_ref,
                    q_ref, k_ref, kv_ref, u_ref, gq_ref, gk_ref, gv_ref, lgf_ref, lgb_ref):
    x = x_ref[0]
    h = _rms(x) * n1_ref[...]
    h = h * (1.0 + sc_ref[0]) + sh_ref[0]
    hb = h.astype(BF16)
    cos = cos_ref[...]
    sin = sin_ref[...]
    cos8 = jnp.concatenate([cos] * MLA_HEADS, axis=1)
    sin8 = jnp.concatenate([sin] * MLA_HEADS, axis=1)

    za = _dot(hb, wa_ref[...])
    cqn = (_rms(za[:, :MLA_RANK]) * qn_ref[...]).astype(BF16)
    q = _dot(cqn, wuq_ref[...]) * cos8 + _dot(cqn, wuqs_ref[...]) * sin8
    q_ref[0] = (q * (MLA_SCALE * LOG2_E)).astype(BF16)

    ckvn = (_rms(za[:, MLA_RANK:]) * kvn_ref[...]).astype(BF16)
    kv = _dot(ckvn, wukv_ref[...])
    kr2 = _dot(hb, wkr_ref[...])
    krr = kr2[:, :HEAD_SLOT] * cos + kr2[:, HEAD_SLOT:] * sin
    nope = lax.broadcasted_iota(jnp.int32, krr.shape, 1) < MLA_NOPE
    for hd in range(MLA_HEADS):
        sl = slice(hd * HEAD_SLOT, (hd + 1) * HEAD_SLOT)
        k_ref[0, :, sl] = jnp.where(nope, kv[:, sl], krr).astype(BF16)
        kv_ref[0, :, sl] = jnp.where(nope, 1.0, kv[:, sl]).astype(BF16)

    u_ref[0] = _dot(hb, wu_ref[...])
    zg = _dot(hb, wg_ref[...])
    gq_ref[0] = zg[:, :GLA_KW]
    gk_ref[0] = zg[:, GLA_KW:2 * GLA_KW]
    gv_ref[0] = zg[:, 2 * GLA_KW:]
    zlr = _dot(hb, wlr_ref[...]).astype(BF16)
    for d, o_ref in enumerate((lgf_ref, lgb_ref)):
        z = _dot(zlr, wa2_ref[d]) + ba2_ref[d]
        o_ref[0] = (jnp.minimum(z, 0.0) - jnp.log1p(jnp.exp(-jnp.abs(z)))) * (1.0 / GLA_GATE_TAU)


def _in_proj_call(x, mod_sh, mod_sc, mod_row, n1, cos_t, sin_t, w, tm):
    Bn, Tn, D = x.shape
    nt = Tn // tm
    row = (lambda b, i: (b, 0, 0)) if mod_row is None else (lambda b, i: (mod_row, 0, 0))
    tok = lambda width: pl.BlockSpec((1, tm, width), lambda b, i: (b, i, 0))
    weights = [w["wa"], w["wkr"], w["wu"], w["wg"], w["wlr"], w["qn"], w["kvn"],
               w["wuq"], w["wuqs"], w["wukv"], w["wa2"], w["ba2"]]
    outs = [(D, BF16), (D, BF16), (D, BF16), (SSM_WIDTH, F32), (GLA_KW, F32), (GLA_KW, F32),
            (GLA_WIDTH, F32), (GLA_KW, F32), (GLA_KW, F32)]
    out_shape = [jax.ShapeDtypeStruct((Bn, Tn, wd), dt) for wd, dt in outs]
    out_specs = [tok(wd) for wd, _ in outs]
    return pl.pallas_call(
        _in_proj_kernel,
        out_shape=out_shape,
        grid=(Bn, nt),
        in_specs=[tok(D),
                  pl.BlockSpec((1, 1, D), row), pl.BlockSpec((1, 1, D), row),
                  _const_spec((1, D)),
                  pl.BlockSpec((tm, HEAD_SLOT), lambda b, i: (i, 0)),
                  pl.BlockSpec((tm, HEAD_SLOT), lambda b, i: (i, 0))]
                 + [_const_spec(a.shape) for a in weights],
        out_specs=out_specs,
        compiler_params=_params(("parallel", "parallel")),
        name="in_proj",
    )(x, mod_sh, mod_sc, n1, cos_t, sin_t, *weights)


def _lane_fold(x, op):
    out = x[:, :HEAD_SLOT]
    for j in range(1, x.shape[1] // HEAD_SLOT):
        out = op(out, x[:, j * HEAD_SLOT:(j + 1) * HEAD_SLOT])
    return out


def _attn_update(s, kvt, carry):
    m, acc = carry
    m_new = jnp.maximum(m, jnp.max(_lane_fold(s, jnp.maximum), axis=-1, keepdims=True))
    p = jnp.exp2(s - m_new)
    acc = jnp.exp2(m - m_new) * acc + _dot(p.astype(BF16), kvt)
    return m_new, acc


def _attn_kernel(*refs, n_lat_tiles, tk):
    if n_lat_tiles:
        q_ref, kc_ref, kvc_ref, kl_ref, kvl_ref, o_ref, s_ref, mx_ref, p_ref, m_ref, acc_ref = refs
    else:
        q_ref, kc_ref, kvc_ref, o_ref = refs
    tq = q_ref.shape[1]
    sls = [slice(hh * HEAD_SLOT, (hh + 1) * HEAD_SLOT) for hh in range(2)]
    qs = [q_ref[0, :, sl] for sl in sls]
    init = (jnp.full((tq, 1), -jnp.inf, F32), jnp.zeros((tq, HEAD_SLOT), F32))
    carry = tuple(_attn_update(_dot_nt(qs[hh], kc_ref[0, :, sls[hh]]), kvc_ref[0, :, sls[hh]], init)
                  for hh in range(2))
    if n_lat_tiles:
        for hh in range(2):
            m, acc = carry[hh]
            m_ref[hh] = jnp.broadcast_to(m, (tq, HEAD_SLOT))
            acc_ref[hh] = acc
        rows = lambda i: pl.ds(pl.multiple_of(i * tk, tk), tk)

        def scores(i, slot):
            for hh in range(2):
                s = _dot_nt(qs[hh], kl_ref[0, rows(i), sls[hh]])
                s_ref[slot, hh] = s
                mx_ref[slot, hh] = _lane_fold(s, jnp.maximum)

        def update(i, slot):
            for hh in range(2):
                for r in range(0, tq, ATT_STRIP):
                    rs = slice(r, r + ATT_STRIP)
                    m_old = m_ref[hh, rs, :]
                    m_new = jnp.maximum(m_old, jnp.max(mx_ref[slot, hh, rs, :], axis=-1, keepdims=True))
                    p = jnp.exp2(s_ref[slot, hh, rs, :] - jnp.concatenate([m_new] * (tk // HEAD_SLOT), axis=1))
                    m_ref[hh, rs, :] = m_new
                    acc_ref[hh, rs, :] = jnp.exp2(m_old - m_new) * acc_ref[hh, rs, :]
                    p_ref[hh, rs, :] = p.astype(BF16)
                acc_ref[hh] += _dot(p_ref[hh], kvl_ref[0, rows(i), sls[hh]])

        scores(0, 0)

        def body(j, c):
            i = 2 * j
            scores(i + 1, 1)
            update(i, 0)
            scores(jnp.minimum(i + 2, n_lat_tiles - 1), 0)
            update(i + 1, 1)
            return c

        lax.fori_loop(0, n_lat_tiles // 2, body, 0)
        carry = tuple((None, acc_ref[hh]) for hh in range(2))
    outs = [(acc * pltpu.roll(1.0 / acc, MLA_NOPE, 1))[:, MLA_NOPE:] for _, acc in carry]
    o_ref[0] = jnp.concatenate(outs, axis=1).astype(o_ref.dtype)


def _attn_call(q, k_ctx, kv_ctx, k_lat, kv_lat, tq, tk):
    Bn, Tq, D = q.shape
    C = k_ctx.shape[1]
    hp = MLA_HEADS // 2
    w2 = 2 * HEAD_SLOT
    n_lat_tiles = 0 if k_lat is None else k_lat.shape[1] // tk
    in_specs = [pl.BlockSpec((1, tq, w2), lambda b, h, i: (b, i, h)),
                pl.BlockSpec((1, C, w2), lambda b, h, i: (b, 0, h)),
                pl.BlockSpec((1, C, w2), lambda b, h, i: (b, 0, h))]
    args = [q, k_ctx, kv_ctx]
    scratch = []
    if n_lat_tiles:
        assert n_lat_tiles % 2 == 0 and tq % ATT_STRIP == 0
        Tk = k_lat.shape[1]
        in_specs += [pl.BlockSpec((1, Tk, w2), lambda b, h, i: (b, 0, h)),
                     pl.BlockSpec((1, Tk, w2), lambda b, h, i: (b, 0, h))]
        args += [k_lat, kv_lat]
        stat = pltpu.VMEM((2, tq, HEAD_SLOT), F32)
        scratch = [pltpu.VMEM((2, 2, tq, tk), F32), pltpu.VMEM((2, 2, tq, HEAD_SLOT), F32),
                   pltpu.VMEM((2, tq, tk), BF16), stat, stat]
    return pl.pallas_call(
        functools.partial(_attn_kernel, n_lat_tiles=n_lat_tiles, tk=tk),
        out_shape=jax.ShapeDtypeStruct((Bn, Tq, MLA_WIDTH), BF16),
        grid=(Bn, hp, Tq // tq),
        in_specs=in_specs,
        out_specs=pl.BlockSpec((1, tq, 2 * MLA_V), lambda b, h, i: (b, i, h)),
        scratch_shapes=scratch,
        compiler_params=_params(("parallel", "parallel", "parallel")),
        name="attn",
    )(*args)


def _s5_disc_kernel(lr_ref, li_ref, ldt_ref, br_ref, bi_ref, ar_ref, ai_ref, bbr_ref, bbi_ref):
    lr = lr_ref[0]
    li = li_ref[0]
    dt = jnp.exp(ldt_ref[0])
    mag = jnp.exp(dt * lr)
    ar = mag * jnp.cos(dt * li)
    ai = mag * jnp.sin(dt * li)
    den = lr * lr + li * li
    fr = ((ar - 1.0) * lr + ai * li) / den
    fi = (ai * lr - (ar - 1.0) * li) / den
    ar_ref[0] = ar
    ai_ref[0] = ai
    br = br_ref[...]
    bi = bi_ref[...]
    bbr_ref[0] = fr * br - fi * bi
    bbi_ref[0] = fr * bi + fi * br


def _s5_disc_call(lam_re, lam_im, log_dt, b_re, b_im):
    n = SSM_N
    lr = lam_re.reshape(2, 1, n)
    li = lam_im.reshape(2, 1, n)
    ldt = jnp.repeat(log_dt, SSM_STATE, axis=1).reshape(2, 1, n)
    br = b_re.reshape(n, SSM_GROUP).T
    bi = b_im.reshape(n, SSM_GROUP).T
    vec = pl.BlockSpec((1, 1, n), lambda d: (d, 0, 0))
    mat = pl.BlockSpec((SSM_GROUP, n), lambda d: (0, 0))
    omat = pl.BlockSpec((1, SSM_GROUP, n), lambda d: (d, 0, 0))
    return pl.pallas_call(
        _s5_disc_kernel,
        out_shape=[jax.ShapeDtypeStruct((2, 1, n), F32), jax.ShapeDtypeStruct((2, 1, n), F32),
                   jax.ShapeDtypeStruct((2, SSM_GROUP, n), F32), jax.ShapeDtypeStruct((2, SSM_GROUP, n), F32)],
        grid=(2,),
        in_specs=[vec, vec, vec, mat, mat],
        out_specs=[vec, vec, omat, omat],
        compiler_params=_params(("arbitrary",)),
        name="s5_disc",
    )(lr, li, ldt, br, bi)


def _s5_scan_kernel(uf_ref, ub_ref, bd_ref, ar_ref, ai_ref, cre_ref, cim_ref, h0_ref,
                    yf_ref, yb_ref, hfin_ref, xs_ref, h_ref, *, tl, nb):
    i = pl.program_id(0)
    n = SSM_N

    @pl.when(i == 0)
    def _():
        h_ref[...] = h0_ref[...]

    half = n // 2
    hw = SSM_WIDTH // 2
    for d, u_ref in enumerate((uf_ref, ub_ref)):
        u = u_ref[...].reshape(tl * nb, SSM_WIDTH).astype(BF16)
        for k in range(2):
            xk = _dot(u[:, k * hw:(k + 1) * hw], bd_ref[d, k])
            xs_ref[d, :, k * half:(k + 1) * half] = xk[:, :half]
            xs_ref[d, :, n + k * half:n + (k + 1) * half] = xk[:, half:]

    for d in range(2):
        for cb in range(2):
            re = slice(cb * half, (cb + 1) * half)
            im = slice(n + cb * half, n + (cb + 1) * half)
            ar = jnp.broadcast_to(ar_ref[d, :, re], (nb, half))
            ai = jnp.broadcast_to(ai_ref[d, :, re], (nb, half))

            def body(t, carry, d=d, re=re, im=im, ar=ar, ai=ai):
                hr, hi = carry
                tt = t if d == 0 else tl - 1 - t
                row = pl.multiple_of(tt * nb, nb)
                nr = ar * hr - ai * hi + xs_ref[d, pl.ds(row, nb), re]
                ni = ar * hi + ai * hr + xs_ref[d, pl.ds(row, nb), im]
                xs_ref[d, pl.ds(row, nb), re] = nr
                xs_ref[d, pl.ds(row, nb), im] = ni
                return nr, ni

            hr, hi = lax.fori_loop(0, tl, body, (h_ref[d, :, re], h_ref[d, :, im]), unroll=4)
            h_ref[d, :, re] = hr
            h_ref[d, :, im] = hi

    for d, y_ref in enumerate((yf_ref, yb_ref)):
        ys = []
        for k in range(2):
            hr = xs_ref[d, :, k * half:(k + 1) * half].astype(BF16)
            hi = xs_ref[d, :, n + k * half:n + (k + 1) * half].astype(BF16)
            ys.append(_dot(hr, cre_ref[k]) - _dot(hi, cim_ref[k]))
        y_ref[...] = jnp.concatenate(ys, axis=1).reshape(tl, nb, SSM_WIDTH)

    @pl.when(i == pl.num_programs(0) - 1)
    def _():
        hfin_ref[...] = h_ref[...]


def _s5_scan_call(u_tb, bd, ar, ai, cre_t, cim_t, h0, tl):
    Tn, nb, W = u_tb.shape
    nc = Tn // tl
    n = SSM_N
    fwd = pl.BlockSpec((tl, nb, W), lambda i: (i, 0, 0))
    bwd = pl.BlockSpec((tl, nb, W), lambda i: (nc - 1 - i, 0, 0))
    return pl.pallas_call(
        functools.partial(_s5_scan_kernel, tl=tl, nb=nb),
        out_shape=[jax.ShapeDtypeStruct((Tn, nb, W), F32), jax.ShapeDtypeStruct((Tn, nb, W), F32),
                   jax.ShapeDtypeStruct((2, nb, 2 * n), F32)],
        grid=(nc,),
        in_specs=[fwd, bwd, _const_spec(bd.shape), _const_spec(ar.shape), _const_spec(ai.shape),
                  _const_spec(cre_t.shape), _const_spec(cim_t.shape), _const_spec(h0.shape)],
        out_specs=[fwd, bwd, pl.BlockSpec((2, nb, 2 * n), lambda i: (0, 0, 0))],
        scratch_shapes=[pltpu.VMEM((2, tl * nb, 2 * n), F32), pltpu.VMEM((2, nb, 2 * n), F32)],
        compiler_params=_params(("arbitrary",)),
        name="s5_scan",
    )(u_tb, u_tb, bd, ar, ai, cre_t, cim_t, h0)


def _s5_out_kernel(u_ref, yf_ref, yb_ref, d_ref, wglu_ref, bglu_ref, o_ref):
    y = d_ref[...] * u_ref[...] + yf_ref[...] + yb_ref[...]
    g = jax.nn.gelu(y)
    z = _dot(g.astype(BF16), wglu_ref[...]) + bglu_ref[...]
    o_ref[...] = (g * jax.nn.sigmoid(z)).astype(o_ref.dtype)


def _s5_out_call(u, yf, yb, d_skip, wglu, bglu, tr):
    R, W = u.shape
    tr = min(tr, R)
    blk = pl.BlockSpec((tr, W), lambda i: (i, 0))
    return pl.pallas_call(
        _s5_out_kernel,
        out_shape=jax.ShapeDtypeStruct((R, W), BF16),
        grid=(R // tr,),
        in_specs=[blk, blk, blk, _const_spec(d_skip.shape), _const_spec(wglu.shape), _const_spec(bglu.shape)],
        out_specs=blk,
        compiler_params=_params(("parallel",)),
        name="s5_out",
    )(u, yf, yb, d_skip, wglu, bglu)


def _split3(x):
    hi = x.astype(BF16)
    r1 = x - hi.astype(F32)
    mid = r1.astype(BF16)
    lo = (r1 - mid.astype(F32)).astype(BF16)
    return hi, mid, lo


def _gla_block(q, k, v, lg, s, d, cst, n_chunks):
    tri_ref, mask_ref, same_ref, hmask_ref, bmask_ref = cst
    L = GLA_CHUNK
    parts = _split3(lg)
    b = sum(_dot(tri_ref[d], p) for p in parts)
    tot = sum(_dot(same_ref[...], p) for p in parts)
    q_in = q * (GLA_DK ** -0.5) * jnp.exp(b)
    k_in = (k * jnp.exp(-b)).astype(BF16)
    k_dec = (k * jnp.exp(tot - b)).astype(BF16)
    vb = v.astype(BF16)
    mask = mask_ref[d] > 0.0
    outs = []
    for hd in range(GLA_HEADS):
        qh = (q_in * hmask_ref[hd]).astype(BF16)
        att = jnp.where(mask, _dot_nt(qh, k_in), 0.0).astype(BF16)
        outs.append(_dot(att, vb[:, hd * GLA_DV:(hd + 1) * GLA_DV]))
    o_intra = jnp.concatenate(outs, axis=1)
    q_in_b = q_in.astype(BF16)
    bmask = bmask_ref[...]
    o_rows = [None] * n_chunks
    for c in (range(n_chunks) if d == 0 else reversed(range(n_chunks))):
        rows = slice(c * L, (c + 1) * L)
        o_rows[c] = _dot_nt(q_in_b[rows], s.astype(BF16))
        dec = jnp.exp(tot[c * L:c * L + 1, :])
        s = dec * s + _dot_tn(vb[rows], k_dec[rows]) * bmask
    return o_intra + jnp.concatenate(o_rows, axis=0), s


def _gla_kernel(qf_ref, kf_ref, vf_ref, lgf_ref, qb_ref, kb_ref, vb_ref, lgb_ref, s0_ref,
                tri_ref, mask_ref, same_ref, hmask_ref, bmask_ref,
                of_ref, ob_ref, sfin_ref, s_ref, *, n_chunks):
    i = pl.program_id(1)
    cst = (tri_ref, mask_ref, same_ref, hmask_ref, bmask_ref)

    @pl.when(i == 0)
    def _():
        s_ref[...] = s0_ref[0]

    o, s = _gla_block(qf_ref[0], kf_ref[0], vf_ref[0], lgf_ref[0], s_ref[0], 0, cst, n_chunks)
    of_ref[0] = o
    s_ref[0] = s
    o, s = _gla_block(qb_ref[0], kb_ref[0], vb_ref[0], lgb_ref[0], s_ref[1], 1, cst, n_chunks)
    ob_ref[0] = o
    s_ref[1] = s

    @pl.when(i == pl.num_programs(1) - 1)
    def _():
        sfin_ref[0] = s_ref[...]


def _gla_consts(tg):
    t = jnp.arange(tg)
    same = (t[:, None] // GLA_CHUNK) == (t[None, :] // GLA_CHUNK)
    low = same & (t[None, :] <= t[:, None])
    upp = same & (t[None, :] >= t[:, None])
    tri = jnp.stack([low, upp])
    hmask = (jnp.arange(GLA_KW)[None, None, :] // GLA_DK) == jnp.arange(GLA_HEADS)[:, None, None]
    bmask = (jnp.arange(GLA_WIDTH)[:, None] // GLA_DV) == (jnp.arange(GLA_KW)[None, :] // GLA_DK)
    return tri.astype(BF16), tri.astype(F32), same.astype(BF16), hmask.astype(F32), bmask.astype(F32)


def _gla_call(q, k, v, lgf, lgb, s0, tg):
    Bn, Tn, _ = q.shape
    nblk = Tn // tg
    consts = _gla_consts(tg)
    fwd = lambda w: pl.BlockSpec((1, tg, w), lambda b, i: (b, i, 0))
    bwd = lambda w: pl.BlockSpec((1, tg, w), lambda b, i: (b, nblk - 1 - i, 0))
    st = pl.BlockSpec((1, 2, GLA_WIDTH, GLA_KW), lambda b, i: (b, 0, 0, 0))
    return pl.pallas_call(
        functools.partial(_gla_kernel, n_chunks=tg // GLA_CHUNK),
        out_shape=[jax.ShapeDtypeStruct((Bn, Tn, GLA_WIDTH), F32), jax.ShapeDtypeStruct((Bn, Tn, GLA_WIDTH), F32),
                   jax.ShapeDtypeStruct((Bn, 2, GLA_WIDTH, GLA_KW), F32)],
        grid=(Bn, nblk),
        in_specs=[fwd(GLA_KW), fwd(GLA_KW), fwd(GLA_WIDTH), fwd(GLA_KW),
                  bwd(GLA_KW), bwd(GLA_KW), bwd(GLA_WIDTH), bwd(GLA_KW), st]
                 + [_const_spec(a.shape) for a in consts],
        out_specs=[fwd(GLA_WIDTH), bwd(GLA_WIDTH), st],
        scratch_shapes=[pltpu.VMEM((2, GLA_WIDTH, GLA_KW), F32)],
        compiler_params=_params(("parallel", "arbitrary")),
        name="gla",
    )(q, k, v, lgf, q, k, v, lgb, s0, *consts)


def _merge_kernel(x_ref, oa_ref, ys_ref, of_ref, ob_ref, sh_ref, sc_ref, g1_ref, n1_ref,
                  wgate_ref, wr_ref, wmo_ref, wso_ref, gn_ref, wgo_ref, wout_ref, o_ref):
    x = x_ref[0]
    h = _rms(x) * n1_ref[...]
    hb = (h * (1.0 + sc_ref[0]) + sh_ref[0]).astype(BF16)
    ya = _dot(oa_ref[0], wmo_ref[...])
    yb = _dot(ys_ref[0], wso_ref[...])
    o = of_ref[0] + ob_ref[0]
    on = jnp.concatenate([_rms(o[:, hd * GLA_DV:(hd + 1) * GLA_DV]) for hd in range(GLA_HEADS)], axis=1)
    r = _dot(hb, wr_ref[...])
    yg = _dot(((on * gn_ref[...]) * (r * jax.nn.sigmoid(r))).astype(BF16), wgo_ref[...])
    m = jnp.zeros_like(ya)
    for j, br in enumerate((ya, yb, yg)):
        gate = jax.nn.sigmoid(_dot(hb, wgate_ref[:, j * D_MODEL:(j + 1) * D_MODEL]))
        m = m + gate * br
    o_ref[0] = x + g1_ref[0] * _dot(m.astype(BF16), wout_ref[...])


def _merge_call(x, oa, ys, of, ob, mod_sh, mod_sc, mod_g, mod_row, n1, w, tm):
    Bn, Tn, D = x.shape
    row = (lambda b, i: (b, 0, 0)) if mod_row is None else (lambda b, i: (mod_row, 0, 0))
    tok = lambda width: pl.BlockSpec((1, tm, width), lambda b, i: (b, i, 0))
    mrow = pl.BlockSpec((1, 1, D), row)
    weights = [w["wgate"], w["wr"], w["wmo"], w["wso"], w["gn"], w["wgo"], w["wout"]]
    return pl.pallas_call(
        _merge_kernel,
        out_shape=jax.ShapeDtypeStruct((Bn, Tn, D), F32),
        grid=(Bn, Tn // tm),
        in_specs=[tok(D), tok(MLA_WIDTH), tok(SSM_WIDTH), tok(GLA_WIDTH), tok(GLA_WIDTH),
                  mrow, mrow, mrow, _const_spec((1, D))] + [_const_spec(a.shape) for a in weights],
        out_specs=tok(D),
        compiler_params=_params(("parallel", "parallel")),
        name="merge",
    )(x, oa, ys, of, ob, mod_sh, mod_sc, mod_g, n1, *weights)


def _ffn_kernel(x_ref, sh_ref, sc_ref, g2_ref, n2_ref, wg_ref, wu_ref, wd_ref, fn_ref, o_ref, *, final):
    x = x_ref[0]
    h = _rms(x) * n2_ref[...]
    hb = (h * (1.0 + sc_ref[0]) + sh_ref[0]).astype(BF16)
    acc = jnp.zeros_like(x)
    half = FFN_HIDDEN // 2
    for j in range(2):
        cs = slice(j * half, (j + 1) * half)
        g = _dot(hb, wg_ref[:, cs])
        u = _dot(hb, wu_ref[:, cs])
        acc = acc + _dot(((g * jax.nn.sigmoid(g)) * u).astype(BF16), wd_ref[cs, :])
    y = x + g2_ref[0] * acc
    if final:
        y = _rms(y) * fn_ref[...]
    o_ref[0] = y


def _ffn_call(x, mod_sh, mod_sc, mod_g, mod_row, n2, wg, wu, wd, fn, final, tm):
    Bn, Tn, D = x.shape
    row = (lambda b, i: (b, 0, 0)) if mod_row is None else (lambda b, i: (mod_row, 0, 0))
    tok = pl.BlockSpec((1, tm, D), lambda b, i: (b, i, 0))
    mrow = pl.BlockSpec((1, 1, D), row)
    return pl.pallas_call(
        functools.partial(_ffn_kernel, final=final),
        out_shape=jax.ShapeDtypeStruct((Bn, Tn, D), F32),
        grid=(Bn, Tn // tm),
        in_specs=[tok, mrow, mrow, mrow, _const_spec((1, D)), _const_spec(wg.shape), _const_spec(wu.shape),
                  _const_spec(wd.shape), _const_spec((1, D))],
        out_specs=tok,
        compiler_params=_params(("parallel", "parallel")),
        name="ffn",
    )(x, mod_sh, mod_sc, mod_g, n2, wg, wu, wd, fn)


def _pair_swap(w):
    w2 = w.reshape(w.shape[0], -1, 2)
    return jnp.stack([-w2[..., 1], w2[..., 0]], axis=-1).reshape(w.shape)


def _layer_weights(l, w_in, mla_q_norm, mla_w_uq, mla_kv_norm, mla_w_ukv, gla_w_a2, gla_b_a2):
    wi = w_in[l]
    D = wi.shape[0]
    o = 0
    cuts = {}
    for name, sz in (("cq", MLA_RANK), ("ckv", MLA_RANK), ("kr", MLA_ROPE), ("u", SSM_WIDTH), ("gq", GLA_KW),
                     ("gk", GLA_KW), ("gv", GLA_WIDTH), ("gr", GLA_WIDTH), ("af", GLA_GATE_RANK),
                     ("ab", GLA_GATE_RANK), ("gate", 3 * D_MODEL)):
        cuts[name] = wi[:, o:o + sz]
        o += sz
    zeros = lambda n: jnp.zeros((D, n), F32)
    kr = cuts["kr"]
    pad_k = HEAD_SLOT - MLA_NOPE - MLA_ROPE
    wkr = jnp.concatenate([zeros(MLA_NOPE), kr, zeros(pad_k), zeros(MLA_NOPE), _pair_swap(kr), zeros(pad_k)], axis=1)
    wlr = jnp.concatenate([cuts["af"], cuts["ab"], zeros(LR_PAD - 2 * GLA_GATE_RANK)], axis=1)
    uq = mla_w_uq[l].reshape(MLA_RANK, MLA_HEADS, MLA_NOPE + MLA_ROPE)
    zq = jnp.zeros((MLA_RANK, MLA_HEADS, pad_k), F32)
    wuq = jnp.concatenate([uq, zq], axis=2).reshape(MLA_RANK, MLA_HEADS * HEAD_SLOT)
    uq_sw = _pair_swap(uq[..., MLA_NOPE:].reshape(MLA_RANK, -1)).reshape(MLA_RANK, MLA_HEADS, MLA_ROPE)
    wuqs = jnp.concatenate([jnp.zeros((MLA_RANK, MLA_HEADS, MLA_NOPE), F32), uq_sw, zq], axis=2)
    wuqs = wuqs.reshape(MLA_RANK, MLA_HEADS * HEAD_SLOT)
    wa2 = jnp.zeros((2, LR_PAD, GLA_KW), F32)
    wa2 = wa2.at[0, :GLA_GATE_RANK].set(gla_w_a2[l, 0]).at[1, GLA_GATE_RANK:2 * GLA_GATE_RANK].set(gla_w_a2[l, 1])
    b16 = lambda a: a.astype(BF16)
    return dict(
        wa=b16(jnp.concatenate([cuts["cq"], cuts["ckv"]], axis=1)), wkr=b16(wkr), wu=b16(cuts["u"]),
        wg=b16(jnp.concatenate([cuts["gq"], cuts["gk"], cuts["gv"]], axis=1)), wlr=b16(wlr),
        qn=mla_q_norm[l].reshape(1, -1), kvn=mla_kv_norm[l].reshape(1, -1),
        wuq=b16(wuq), wuqs=b16(wuqs), wukv=b16(mla_w_ukv[l]), wa2=b16(wa2),
        ba2=gla_b_a2[l].reshape(2, 1, GLA_KW), wr=b16(cuts["gr"]), wgate=b16(cuts["gate"]))


def _rope_tables(T):
    t = jnp.arange(T, dtype=jnp.int32)
    rows = (t // GRID_W).astype(F32)
    cols = (t % GRID_W).astype(F32)
    inv = ROPE_BASE ** (-jnp.arange(ROPE_FREQS, dtype=F32) / ROPE_FREQS)
    ang = jnp.concatenate([rows[:, None] * inv, cols[:, None] * inv], axis=-1)
    ang2 = jnp.repeat(ang, 2, axis=1)
    pad_k = HEAD_SLOT - MLA_NOPE - MLA_ROPE
    cos = jnp.concatenate([jnp.ones((T, MLA_NOPE), F32), jnp.cos(ang2), jnp.ones((T, pad_k), F32)], axis=1)
    sin = jnp.concatenate([jnp.zeros((T, MLA_NOPE), F32), jnp.sin(ang2), jnp.zeros((T, pad_k), F32)], axis=1)
    return cos, sin


def _s5_matrices(bbr, bbi, c_re, c_im):
    n = SSM_N
    grp_row = jnp.arange(SSM_WIDTH)[:, None] // SSM_GROUP
    grp_col = jnp.arange(n)[None, :] // SSM_STATE
    blk = grp_row == grp_col
    tile = lambda m: jnp.where(blk, jnp.tile(m, (SSM_GROUPS, 1)), 0.0)
    hw, hn = SSM_WIDTH // 2, n // 2
    halves = lambda m: [m[k * hw:(k + 1) * hw, k * hn:(k + 1) * hn] for k in range(2)]
    bd = jnp.stack([jnp.stack([jnp.concatenate([r, i], axis=1)
                               for r, i in zip(halves(tile(bbr[d])), halves(tile(bbi[d])))])
                    for d in range(2)]).astype(BF16)

    def readout(c):
        ct = jnp.transpose(c, (0, 2, 1)).reshape(n, SSM_GROUP)
        full = jnp.where(blk.T, jnp.tile(ct, (1, SSM_GROUPS)), 0.0)
        return jnp.stack([full[k * hn:(k + 1) * hn, k * hw:(k + 1) * hw] for k in range(2)]).astype(BF16)

    return bd, readout(c_re), readout(c_im)


def kernel(x, c, ctx, c_ctx, w_mod, b_mod, norm1, norm2, w_in, mla_q_norm, mla_w_uq, mla_kv_norm, mla_w_ukv, mla_w_o, ssm_lam_re, ssm_lam_im, ssm_log_dt, ssm_b_re, ssm_b_im, ssm_c_re, ssm_c_im, ssm_d, ssm_w_glu, ssm_b_glu, ssm_w_o, gla_w_a2, gla_b_a2, gla_norm, gla_w_o, w_out, ffn_w_up, ffn_w_down, final_norm):
    B, T, D = x.shape
    C = ctx.shape[1]
    depth = w_mod.shape[0]
    n_rows = -(-(B + 1) // 8) * 8
    cvec = jnp.zeros((n_rows, D), F32).at[:B].set(c).at[B].set(c_ctx)
    mod = _mod_call(cvec, w_mod.astype(BF16), b_mod)
    cos_l, sin_l = _rope_tables(T)
    cos_c = jnp.ones((C, HEAD_SLOT), F32)
    sin_c = jnp.zeros((C, HEAD_SLOT), F32)
    tm_in, tm_merge, tm_ffn = min(TM_IN, T), min(TM_MERGE, T), min(TM_FFN, T)
    tq, tk = min(TQ, T), min(TK, T // 2)
    tg = min(GLA_BLOCK, T)
    b16 = lambda a: a.astype(BF16)

    xc = ctx
    for l in range(depth):
        with_ctx = l < depth - 1
        m6 = [mod[l, :, j * D:(j + 1) * D].reshape(n_rows, 1, D) for j in range(6)]
        sh1, sc1, g1, sh2, sc2, g2 = m6
        n1 = norm1[l].reshape(1, D)
        n2 = norm2[l].reshape(1, D)
        w = _layer_weights(l, w_in, mla_q_norm, mla_w_uq, mla_kv_norm, mla_w_ukv, gla_w_a2, gla_b_a2)
        w.update(wmo=b16(mla_w_o[l]), wso=b16(ssm_w_o[l]), gn=gla_norm[l].reshape(1, -1), wgo=b16(gla_w_o[l]),
                 wout=b16(w_out[l]))
        up = ffn_w_up[l]
        wup_g, wup_u, wdown = b16(up[:, :FFN_HIDDEN]), b16(up[:, FFN_HIDDEN:]), b16(ffn_w_down[l])
        fn = final_norm.reshape(1, D)

        qc, kc, kvc, uc, gqc, gkc, gvc, lgfc, lgbc = _in_proj_call(xc, sh1, sc1, B, n1, cos_c, sin_c, w, C)
        ql, kl, kvl, ul, gql, gkl, gvl, lgfl, lgbl = _in_proj_call(x, sh1, sc1, None, n1, cos_l, sin_l, w, tm_in)

        oa = _attn_call(ql, kc, kvc, kl, kvl, tq, tk)
        oac = _attn_call(qc, kc, kvc, None, None, C, tk) if with_ctx else None

        ar, ai, bbr, bbi = _s5_disc_call(ssm_lam_re[l], ssm_lam_im[l], ssm_log_dt[l], ssm_b_re[l], ssm_b_im[l])
        bd, cre_t, cim_t = _s5_matrices(bbr, bbi, ssm_c_re[l], ssm_c_im[l])
        uc_tb = jnp.transpose(uc, (1, 0, 2))
        ul_tb = jnp.transpose(ul, (1, 0, 2))
        h0 = jnp.zeros((2, B, 2 * SSM_N), F32)
        yfc, ybc, hc = _s5_scan_call(uc_tb, bd, ar, ai, cre_t, cim_t, h0, S5_TL)
        yfl, ybl, _ = _s5_scan_call(ul_tb, bd, ar, ai, cre_t, cim_t, hc, S5_TL)
        d_skip = ssm_d[l].reshape(1, -1)
        wglu, bglu = b16(ssm_w_glu[l]), ssm_b_glu[l].reshape(1, -1)
        flat = lambda a: a.reshape(-1, SSM_WIDTH)
        ysl = _s5_out_call(flat(ul_tb), flat(yfl), flat(ybl), d_skip, wglu, bglu, S5_ROWS_OUT)
        ysl = jnp.transpose(ysl.reshape(T, B, SSM_WIDTH), (1, 0, 2))

        s0 = jnp.zeros((B, 2, GLA_WIDTH, GLA_KW), F32)
        ofc, obc, sc_fin = _gla_call(gqc, gkc, gvc, lgfc, lgbc, s0, min(GLA_BLOCK, C))
        ofl, obl, _ = _gla_call(gql, gkl, gvl, lgfl, lgbl, sc_fin, tg)

        x = _merge_call(x, oa, ysl, ofl, obl, sh1, sc1, g1, None, n1, w, tm_merge)
        x = _ffn_call(x, sh2, sc2, g2, None, n2, wup_g, wup_u, wdown, fn, l == depth - 1, tm_ffn)
        if with_ctx:
            ysc = _s5_out_call(flat(uc_tb), flat(yfc), flat(ybc), d_skip, wglu, bglu, S5_ROWS_OUT)
            ysc = jnp.transpose(ysc.reshape(C, B, SSM_WIDTH), (1, 0, 2))
            xc = _merge_call(xc, oac, ysc, ofc, obc, sh1, sc1, g1, B, n1, w, C)
            xc = _ffn_call(xc, sh2, sc2, g2, B, n2, wup_g, wup_u, wdown, fn, False, C)
    return x
```

```python
import functools
import math

import jax
import jax.numpy as jnp
from jax import lax
from jax.experimental import pallas as pl
from jax.experimental.pallas import tpu as pltpu

F32 = jnp.float32
BF16 = jnp.bfloat16

D_MODEL = 1024
GRID_W = 64
EPS = 1e-6
MLA_HEADS = 8
MLA_RANK = 256
MLA_NOPE = 64
MLA_ROPE = 32
MLA_V = 64
MLA_WIDTH = MLA_HEADS * MLA_V
MLA_SCALE = (MLA_NOPE + MLA_ROPE) ** -0.5
ROPE_FREQS = MLA_ROPE // 4
ROPE_BASE = 10000.0
LOG2_E = math.log2(math.e)
HEAD_SLOT = 128
SSM_WIDTH = 512
SSM_GROUP = 16
SSM_GROUPS = SSM_WIDTH // SSM_GROUP
SSM_STATE = 64
SSM_N = SSM_GROUPS * SSM_STATE
GLA_HEADS = 4
GLA_DK = 64
GLA_DV = 128
GLA_KW = GLA_HEADS * GLA_DK
GLA_WIDTH = GLA_HEADS * GLA_DV
GLA_GATE_RANK = 16
GLA_GATE_TAU = 16.0
GLA_CHUNK = 64
FFN_HIDDEN = 2816
FFN_SPLIT = 1536
LR_PAD = 128

VMEM_LIMIT = 56 * 1024 * 1024

TM_IN = 512
TM_MERGE = 512
TM_FFN = 512
TQ = 512
TK = 1024
ATT_STRIP = 64
S5_TL = 64
S5_ROWS_OUT = 2048
GLA_BLOCK = 256


def _const_spec(shape):
    nd = len(shape)
    return pl.BlockSpec(shape, lambda *_: (0,) * nd, pipeline_mode=pl.Buffered(1))


def _params(sem):
    return pltpu.CompilerParams(dimension_semantics=sem, vmem_limit_bytes=VMEM_LIMIT)


def _rms(x):
    return x * lax.rsqrt(jnp.mean(x * x, axis=-1, keepdims=True) + EPS)


def _dot(a, b):
    return jnp.dot(a, b, preferred_element_type=F32)


def _dot_nt(a, b):
    return lax.dot_general(a, b, (((1,), (1,)), ((), ())), preferred_element_type=F32)


def _dot_tn(a, b):
    return lax.dot_general(a, b, (((0,), (0,)), ((), ())), preferred_element_type=F32)


def _mod_kernel(c_ref, w_ref, b_ref, o_ref):
    c = c_ref[...]
    s = (c * jax.nn.sigmoid(c)).astype(BF16)
    o_ref[0] = _dot(s, w_ref[0]) + b_ref[0]


def _mod_call(cvec, w_mod, b_mod):
    L, D, N = w_mod.shape
    R = cvec.shape[0]
    tn = 1536
    return pl.pallas_call(
        _mod_kernel,
        out_shape=jax.ShapeDtypeStruct((L, R, N), F32),
        grid=(L, N // tn),
        in_specs=[pl.BlockSpec((R, D), lambda l, j: (0, 0)),
                  pl.BlockSpec((1, D, tn), lambda l, j: (l, 0, j)),
                  pl.BlockSpec((1, 1, tn), lambda l, j: (l, 0, j))],
        out_specs=pl.BlockSpec((1, R, tn), lambda l, j: (l, 0, j)),
        compiler_params=_params(("arbitrary", "arbitrary")),
        name="mod",
    )(cvec, w_mod, b_mod.reshape(L, 1, N))


def _in_proj_kernel(x_ref, sh_ref, sc_ref, n1_ref, cos_ref, sin_ref,
                    wa_ref, wkr_ref, wu_ref, wg_ref, wlr_ref,
                    qn_ref, kvn_ref, wuq_ref, wuqs_ref, wukv_ref, wa2_ref, ba2_ref,
                    q_ref, k_ref, kv_ref, u_ref, gq_ref, gk_ref, gv_ref, lgf_ref, lgb_ref):
    x = x_ref[0]
    h = _rms(x) * n1_ref[...]
    h = h * (1.0 + sc_ref[0]) + sh_ref[0]
    hb = h.astype(BF16)
    cos = cos_ref[...]
    sin = sin_ref[...]
    cos8 = jnp.concatenate([cos] * MLA_HEADS, axis=1)
    sin8 = jnp.concatenate([sin] * MLA_HEADS, axis=1)

    za = _dot(hb, wa_ref[...])
    cqn = (_rms(za[:, :MLA_RANK]) * qn_ref[...]).astype(BF16)
    q = _dot(cqn, wuq_ref[...]) * cos8 + _dot(cqn, wuqs_ref[...]) * sin8
    q_ref[0] = (q * (MLA_SCALE * LOG2_E)).astype(BF16)

    ckvn = (_rms(za[:, MLA_RANK:]) * kvn_ref[...]).astype(BF16)
    kv = _dot(ckvn, wukv_ref[...])
    kr2 = _dot(hb, wkr_ref[...])
    krr = kr2[:, :HEAD_SLOT] * cos + kr2[:, HEAD_SLOT:] * sin
    nope = lax.broadcasted_iota(jnp.int32, krr.shape, 1) < MLA_NOPE
    for hd in range(MLA_HEADS):
        sl = slice(hd * HEAD_SLOT, (hd + 1) * HEAD_SLOT)
        k_ref[0, :, sl] = jnp.where(nope, kv[:, sl], krr).astype(BF16)
        kv_ref[0, :, sl] = jnp.where(nope, 1.0, kv[:, sl]).astype(BF16)

    u_ref[0] = _dot(hb, wu_ref[...])
    zg = _dot(hb, wg_ref[...])
    gq_ref[0] = zg[:, :GLA_KW]
    gk_ref[0] = zg[:, GLA_KW:2 * GLA_KW]
    gv_ref[0] = zg[:, 2 * GLA_KW:]
    zlr = _dot(hb, wlr_ref[...]).astype(BF16)
    for d, o_ref in enumerate((lgf_ref, lgb_ref)):
        z = _dot(zlr, wa2_ref[d]) + ba2_ref[d]
        o_ref[0] = (jnp.minimum(z, 0.0) - jnp.log1p(jnp.exp(-jnp.abs(z)))) * (1.0 / GLA_GATE_TAU)


def _in_proj_call(x, mod_sh, mod_sc, mod_row, n1, cos_t, sin_t, w, tm):
    Bn, Tn, D = x.shape
    nt = Tn // tm
    row = (lambda b, i: (b, 0, 0)) if mod_row is None else (lambda b, i: (mod_row, 0, 0))
    tok = lambda width: pl.BlockSpec((1, tm, width), lambda b, i: (b, i, 0))
    weights = [w["wa"], w["wkr"], w["wu"], w["wg"], w["wlr"], w["qn"], w["kvn"],
               w["wuq"], w["wuqs"], w["wukv"], w["wa2"], w["ba2"]]
    outs = [(D, BF16), (D, BF16), (D, BF16), (SSM_WIDTH, F32), (GLA_KW, F32), (GLA_KW, F32),
            (GLA_WIDTH, F32), (GLA_KW, F32), (GLA_KW, F32)]
    out_shape = [jax.ShapeDtypeStruct((Bn, Tn, wd), dt) for wd, dt in outs]
    out_specs = [tok(wd) for wd, _ in outs]
    return pl.pallas_call(
        _in_proj_kernel,
        out_shape=out_shape,
        grid=(Bn, nt),
        in_specs=[tok(D),
                  pl.BlockSpec((1, 1, D), row), pl.BlockSpec((1, 1, D), row),
                  _const_spec((1, D)),
                  pl.BlockSpec((tm, HEAD_SLOT), lambda b, i: (i, 0)),
                  pl.BlockSpec((tm, HEAD_SLOT), lambda b, i: (i, 0))]
                 + [_const_spec(a.shape) for a in weights],
        out_specs=out_specs,
        compiler_params=_params(("parallel", "parallel")),
        name="in_proj",
    )(x, mod_sh, mod_sc, n1, cos_t, sin_t, *weights)


def _lane_fold(x, op):
    out = x[:, :HEAD_SLOT]
    for j in range(1, x.shape[1] // HEAD_SLOT):
        out = op(out, x[:, j * HEAD_SLOT:(j + 1) * HEAD_SLOT])
    return out


def _attn_update(s, kvt, carry):
    m, acc = carry
    m_new = jnp.maximum(m, jnp.max(_lane_fold(s, jnp.maximum), axis=-1, keepdims=True))
    p = jnp.exp2(s - m_new)
    acc = jnp.exp2(m - m_new) * acc + _dot(p.astype(BF16), kvt)
    return m_new, acc


def _attn_kernel(*refs, n_lat_tiles, tk):
    if n_lat_tiles:
        q_ref, kc_ref, kvc_ref, kl_ref, kvl_ref, o_ref, s_ref, mx_ref, p_ref, m_ref, acc_ref = refs
    else:
        q_ref, kc_ref, kvc_ref, o_ref = refs
    tq = q_ref.shape[1]
    sls = [slice(hh * HEAD_SLOT, (hh + 1) * HEAD_SLOT) for hh in range(2)]
    qs = [q_ref[0, :, sl] for sl in sls]
    init = (jnp.full((tq, 1), -jnp.inf, F32), jnp.zeros((tq, HEAD_SLOT), F32))
    carry = tuple(_attn_update(_dot_nt(qs[hh], kc_ref[0, :, sls[hh]]), kvc_ref[0, :, sls[hh]], init)
                  for hh in range(2))
    if n_lat_tiles:
        for hh in range(2):
            m, acc = carry[hh]
            m_ref[hh] = jnp.broadcast_to(m, (tq, HEAD_SLOT))
            acc_ref[hh] = acc
        rows = lambda i: pl.ds(pl.multiple_of(i * tk, tk), tk)

        def scores(i, slot):
            for hh in range(2):
                s = _dot_nt(qs[hh], kl_ref[0, rows(i), sls[hh]])
                s_ref[slot, hh] = s
                mx_ref[slot, hh] = _lane_fold(s, jnp.maximum)

        def update(i, slot):
            for hh in range(2):
                for r in range(0, tq, ATT_STRIP):
                    rs = slice(r, r + ATT_STRIP)
                    m_old = m_ref[hh, rs, :]
                    m_new = jnp.maximum(m_old, jnp.max(mx_ref[slot, hh, rs, :], axis=-1, keepdims=True))
                    p = jnp.exp2(s_ref[slot, hh, rs, :] - jnp.concatenate([m_new] * (tk // HEAD_SLOT), axis=1))
                    m_ref[hh, rs, :] = m_new
                    acc_ref[hh, rs, :] = jnp.exp2(m_old - m_new) * acc_ref[hh, rs, :]
                    p_ref[hh, rs, :] = p.astype(BF16)
                acc_ref[hh] += _dot(p_ref[hh], kvl_ref[0, rows(i), sls[hh]])

        scores(0, 0)

        def body(j, c):
            i = 2 * j
            scores(i + 1, 1)
            update(i, 0)
            scores(jnp.minimum(i + 2, n_lat_tiles - 1), 0)
            update(i + 1, 1)
            return c

        lax.fori_loop(0, n_lat_tiles // 2, body, 0)
        carry = tuple((None, acc_ref[hh]) for hh in range(2))
    outs = [(acc * pltpu.roll(1.0 / acc, MLA_NOPE, 1))[:, MLA_NOPE:] for _, acc in carry]
    o_ref[0] = jnp.concatenate(outs, axis=1).astype(o_ref.dtype)


def _attn_call(q, k_ctx, kv_ctx, k_lat, kv_lat, tq, tk):
    Bn, Tq, D = q.shape
    C = k_ctx.shape[1]
    hp = MLA_HEADS // 2
    w2 = 2 * HEAD_SLOT
    n_lat_tiles = 0 if k_lat is None else k_lat.shape[1] // tk
    in_specs = [pl.BlockSpec((1, tq, w2), lambda b, h, i: (b, i, h)),
                pl.BlockSpec((1, C, w2), lambda b, h, i: (b, 0, h)),
                pl.BlockSpec((1, C, w2), lambda b, h, i: (b, 0, h))]
    args = [q, k_ctx, kv_ctx]
    scratch = []
    if n_lat_tiles:
        assert n_lat_tiles % 2 == 0 and tq % ATT_STRIP == 0
        Tk = k_lat.shape[1]
        in_specs += [pl.BlockSpec((1, Tk, w2), lambda b, h, i: (b, 0, h)),
                     pl.BlockSpec((1, Tk, w2), lambda b, h, i: (b, 0, h))]
        args += [k_lat, kv_lat]
        stat = pltpu.VMEM((2, tq, HEAD_SLOT), F32)
        scratch = [pltpu.VMEM((2, 2, tq, tk), F32), pltpu.VMEM((2, 2, tq, HEAD_SLOT), F32),
                   pltpu.VMEM((2, tq, tk), BF16), stat, stat]
    return pl.pallas_call(
        functools.partial(_attn_kernel, n_lat_tiles=n_lat_tiles, tk=tk),
        out_shape=jax.ShapeDtypeStruct((Bn, Tq, MLA_WIDTH), BF16),
        grid=(Bn, hp, Tq // tq),
        in_specs=in_specs,
        out_specs=pl.BlockSpec((1, tq, 2 * MLA_V), lambda b, h, i: (b, i, h)),
        scratch_shapes=scratch,
        compiler_params=_params(("parallel", "parallel", "parallel")),
        name="attn",
    )(*args)


def _s5_disc_kernel(lr_ref, li_ref, ldt_ref, br_ref, bi_ref, ar_ref, ai_ref, bbr_ref, bbi_ref):
    lr = lr_ref[0]
    li = li_ref[0]
    dt = jnp.exp(ldt_ref[0])
    mag = jnp.exp(dt * lr)
    ar = mag * jnp.cos(dt * li)
    ai = mag * jnp.sin(dt * li)
    den = lr * lr + li * li
    fr = ((ar - 1.0) * lr + ai * li) / den
    fi = (ai * lr - (ar - 1.0) * li) / den
    ar_ref[0] = ar
    ai_ref[0] = ai
    br = br_ref[...]
    bi = bi_ref[...]
    bbr_ref[0] = fr * br - fi * bi
    bbi_ref[0] = fr * bi + fi * br


def _s5_disc_call(lam_re, lam_im, log_dt, b_re, b_im):
    n = SSM_N
    lr = lam_re.reshape(2, 1, n)
    li = lam_im.reshape(2, 1, n)
    ldt = jnp.repeat(log_dt, SSM_STATE, axis=1).reshape(2, 1, n)
    br = b_re.reshape(n, SSM_GROUP).T
    bi = b_im.reshape(n, SSM_GROUP).T
    vec = pl.BlockSpec((1, 1, n), lambda d: (d, 0, 0))
    mat = pl.BlockSpec((SSM_GROUP, n), lambda d: (0, 0))
    omat = pl.BlockSpec((1, SSM_GROUP, n), lambda d: (d, 0, 0))
    return pl.pallas_call(
        _s5_disc_kernel,
        out_shape=[jax.ShapeDtypeStruct((2, 1, n), F32), jax.ShapeDtypeStruct((2, 1, n), F32),
                   jax.ShapeDtypeStruct((2, SSM_GROUP, n), F32), jax.ShapeDtypeStruct((2, SSM_GROUP, n), F32)],
        grid=(2,),
        in_specs=[vec, vec, vec, mat, mat],
        out_specs=[vec, vec, omat, omat],
        compiler_params=_params(("arbitrary",)),
        name="s5_disc",
    )(lr, li, ldt, br, bi)


def _s5_scan_kernel(uf_ref, ub_ref, bd_ref, ar_ref, ai_ref, cre_ref, cim_ref, h0_ref,
                    yf_ref, yb_ref, hfin_ref, xs_ref, h_ref, *, tl, nb):
    i = pl.program_id(0)
    n = SSM_N

    @pl.when(i == 0)
    def _():
        h_ref[...] = h0_ref[...]

    half = n // 2
    hw = SSM_WIDTH // 2
    for d, u_ref in enumerate((uf_ref, ub_ref)):
        u = u_ref[...].reshape(tl * nb, SSM_WIDTH).astype(BF16)
        for k in range(2):
            xk = _dot(u[:, k * hw:(k + 1) * hw], bd_ref[d, k])
            xs_ref[d, :, k * half:(k + 1) * half] = xk[:, :half]
            xs_ref[d, :, n + k * half:n + (k + 1) * half] = xk[:, half:]

    for d in range(2):
        for cb in range(2):
            re = slice(cb * half, (cb + 1) * half)
            im = slice(n + cb * half, n + (cb + 1) * half)
            ar = jnp.broadcast_to(ar_ref[d, :, re], (nb, half))
            ai = jnp.broadcast_to(ai_ref[d, :, re], (nb, half))

            def body(t, carry, d=d, re=re, im=im, ar=ar, ai=ai):
                hr, hi = carry
                tt = t if d == 0 else tl - 1 - t
                row = pl.multiple_of(tt * nb, nb)
                nr = ar * hr - ai * hi + xs_ref[d, pl.ds(row, nb), re]
                ni = ar * hi + ai * hr + xs_ref[d, pl.ds(row, nb), im]
                xs_ref[d, pl.ds(row, nb), re] = nr
                xs_ref[d, pl.ds(row, nb), im] = ni
                return nr, ni

            hr, hi = lax.fori_loop(0, tl, body, (h_ref[d, :, re], h_ref[d, :, im]), unroll=4)
            h_ref[d, :, re] = hr
            h_ref[d, :, im] = hi

    for d, y_ref in enumerate((yf_ref, yb_ref)):
        ys = []
        for k in range(2):
            hr = xs_ref[d, :, k * half:(k + 1) * half].astype(BF16)
            hi = xs_ref[d, :, n + k * half:n + (k + 1) * half].astype(BF16)
            ys.append(_dot(hr, cre_ref[k]) - _dot(hi, cim_ref[k]))
        y_ref[...] = jnp.concatenate(ys, axis=1).reshape(tl, nb, SSM_WIDTH)

    @pl.when(i == pl.num_programs(0) - 1)
    def _():
        hfin_ref[...] = h_ref[...]


def _s5_scan_call(u_tb, bd, ar, ai, cre_t, cim_t, h0, tl):
    Tn, nb, W = u_tb.shape
    nc = Tn // tl
    n = SSM_N
    fwd = pl.BlockSpec((tl, nb, W), lambda i: (i, 0, 0))
    bwd = pl.BlockSpec((tl, nb, W), lambda i: (nc - 1 - i, 0, 0))
    return pl.pallas_call(
        functools.partial(_s5_scan_kernel, tl=tl, nb=nb),
        out_shape=[jax.ShapeDtypeStruct((Tn, nb, W), F32), jax.ShapeDtypeStruct((Tn, nb, W), F32),
                   jax.ShapeDtypeStruct((2, nb, 2 * n), F32)],
        grid=(nc,),
        in_specs=[fwd, bwd, _const_spec(bd.shape), _const_spec(ar.shape), _const_spec(ai.shape),
                  _const_spec(cre_t.shape), _const_spec(cim_t.shape), _const_spec(h0.shape)],
        out_specs=[fwd, bwd, pl.BlockSpec((2, nb, 2 * n), lambda i: (0, 0, 0))],
        scratch_shapes=[pltpu.VMEM((2, tl * nb, 2 * n), F32), pltpu.VMEM((2, nb, 2 * n), F32)],
        compiler_params=_params(("arbitrary",)),
        name="s5_scan",
    )(u_tb, u_tb, bd, ar, ai, cre_t, cim_t, h0)


def _s5_out_kernel(u_ref, yf_ref, yb_ref, d_ref, wglu_ref, bglu_ref, o_ref):
    y = d_ref[...] * u_ref[...] + yf_ref[...] + yb_ref[...]
    g = jax.nn.gelu(y)
    z = _dot(g.astype(BF16), wglu_ref[...]) + bglu_ref[...]
    o_ref[...] = (g * jax.nn.sigmoid(z)).astype(o_ref.dtype)


def _s5_out_call(u, yf, yb, d_skip, wglu, bglu, tr):
    R, W = u.shape
    tr = min(tr, R)
    blk = pl.BlockSpec((tr, W), lambda i: (i, 0))
    return pl.pallas_call(
        _s5_out_kernel,
        out_shape=jax.ShapeDtypeStruct((R, W), BF16),
        grid=(R // tr,),
        in_specs=[blk, blk, blk, _const_spec(d_skip.shape), _const_spec(wglu.shape), _const_spec(bglu.shape)],
        out_specs=blk,
        compiler_params=_params(("parallel",)),
        name="s5_out",
    )(u, yf, yb, d_skip, wglu, bglu)


def _split3(x):
    hi = x.astype(BF16)
    r1 = x - hi.astype(F32)
    mid = r1.astype(BF16)
    lo = (r1 - mid.astype(F32)).astype(BF16)
    return hi, mid, lo


def _gla_block(q, k, v, lg, s, d, cst, n_chunks):
    tri_ref, mask_ref, same_ref, hmask_ref, bmask_ref = cst
    L = GLA_CHUNK
    parts = _split3(lg)
    b = sum(_dot(tri_ref[d], p) for p in parts)
    tot = sum(_dot(same_ref[...], p) for p in parts)
    q_in = q * (GLA_DK ** -0.5) * jnp.exp(b)
    k_in = (k * jnp.exp(-b)).astype(BF16)
    k_dec = (k * jnp.exp(tot - b)).astype(BF16)
    vb = v.astype(BF16)
    mask = mask_ref[d] > 0.0
    outs = []
    for hd in range(GLA_HEADS):
        qh = (q_in * hmask_ref[hd]).astype(BF16)
        att = jnp.where(mask, _dot_nt(qh, k_in), 0.0).astype(BF16)
        outs.append(_dot(att, vb[:, hd * GLA_DV:(hd + 1) * GLA_DV]))
    o_intra = jnp.concatenate(outs, axis=1)
    q_in_b = q_in.astype(BF16)
    bmask = bmask_ref[...]
    o_rows = [None] * n_chunks
    for c in (range(n_chunks) if d == 0 else reversed(range(n_chunks))):
        rows = slice(c * L, (c + 1) * L)
        o_rows[c] = _dot_nt(q_in_b[rows], s.astype(BF16))
        dec = jnp.exp(tot[c * L:c * L + 1, :])
        s = dec * s + _dot_tn(vb[rows], k_dec[rows]) * bmask
    return o_intra + jnp.concatenate(o_rows, axis=0), s


def _gla_kernel(qf_ref, kf_ref, vf_ref, lgf_ref, qb_ref, kb_ref, vb_ref, lgb_ref, s0_ref,
                tri_ref, mask_ref, same_ref, hmask_ref, bmask_ref,
                of_ref, ob_ref, sfin_ref, s_ref, *, n_chunks):
    i = pl.program_id(1)
    cst = (tri_ref, mask_ref, same_ref, hmask_ref, bmask_ref)

    @pl.when(i == 0)
    def _():
        s_ref[...] = s0_ref[0]

    o, s = _gla_block(qf_ref[0], kf_ref[0], vf_ref[0], lgf_ref[0], s_ref[0], 0, cst, n_chunks)
    of_ref[0] = o
    s_ref[0] = s
    o, s = _gla_block(qb_ref[0], kb_ref[0], vb_ref[0], lgb_ref[0], s_ref[1], 1, cst, n_chunks)
    ob_ref[0] = o
    s_ref[1] = s

    @pl.when(i == pl.num_programs(1) - 1)
    def _():
        sfin_ref[0] = s_ref[...]


def _gla_consts(tg):
    t = jnp.arange(tg)
    same = (t[:, None] // GLA_CHUNK) == (t[None, :] // GLA_CHUNK)
    low = same & (t[None, :] <= t[:, None])
    upp = same & (t[None, :] >= t[:, None])
    tri = jnp.stack([low, upp])
    hmask = (jnp.arange(GLA_KW)[None, None, :] // GLA_DK) == jnp.arange(GLA_HEADS)[:, None, None]
    bmask = (jnp.arange(GLA_WIDTH)[:, None] // GLA_DV) == (jnp.arange(GLA_KW)[None, :] // GLA_DK)
    return tri.astype(BF16), tri.astype(F32), same.astype(BF16), hmask.astype(F32), bmask.astype(F32)


def _gla_call(q, k, v, lgf, lgb, s0, tg):
    Bn, Tn, _ = q.shape
    nblk = Tn // tg
    consts = _gla_consts(tg)
    fwd = lambda w: pl.BlockSpec((1, tg, w), lambda b, i: (b, i, 0))
    bwd = lambda w: pl.BlockSpec((1, tg, w), lambda b, i: (b, nblk - 1 - i, 0))
    st = pl.BlockSpec((1, 2, GLA_WIDTH, GLA_KW), lambda b, i: (b, 0, 0, 0))
    return pl.pallas_call(
        functools.partial(_gla_kernel, n_chunks=tg // GLA_CHUNK),
        out_shape=[jax.ShapeDtypeStruct((Bn, Tn, GLA_WIDTH), F32), jax.ShapeDtypeStruct((Bn, Tn, GLA_WIDTH), F32),
                   jax.ShapeDtypeStruct((Bn, 2, GLA_WIDTH, GLA_KW), F32)],
        grid=(Bn, nblk),
        in_specs=[fwd(GLA_KW), fwd(GLA_KW), fwd(GLA_WIDTH), fwd(GLA_KW),
                  bwd(GLA_KW), bwd(GLA_KW), bwd(GLA_WIDTH), bwd(GLA_KW), st]
                 + [_const_spec(a.shape) for a in consts],
        out_specs=[fwd(GLA_WIDTH), bwd(GLA_WIDTH), st],
        scratch_shapes=[pltpu.VMEM((2, GLA_WIDTH, GLA_KW), F32)],
        compiler_params=_params(("parallel", "arbitrary")),
        name="gla",
    )(q, k, v, lgf, q, k, v, lgb, s0, *consts)


def _merge_kernel(x_ref, oa_ref, ys_ref, of_ref, ob_ref, sh_ref, sc_ref, g1_ref, n1_ref,
                  wgate_ref, wr_ref, wmo_ref, wso_ref, gn_ref, wgo_ref, wout_ref, o_ref):
    x = x_ref[0]
    h = _rms(x) * n1_ref[...]
    hb = (h * (1.0 + sc_ref[0]) + sh_ref[0]).astype(BF16)
    ya = _dot(oa_ref[0], wmo_ref[...])
    yb = _dot(ys_ref[0], wso_ref[...])
    o = of_ref[0] + ob_ref[0]
    on = jnp.concatenate([_rms(o[:, hd * GLA_DV:(hd + 1) * GLA_DV]) for hd in range(GLA_HEADS)], axis=1)
    r = _dot(hb, wr_ref[...])
    yg = _dot(((on * gn_ref[...]) * (r * jax.nn.sigmoid(r))).astype(BF16), wgo_ref[...])
    m = jnp.zeros_like(ya)
    for j, br in enumerate((ya, yb, yg)):
        gate = jax.nn.sigmoid(_dot(hb, wgate_ref[:, j * D_MODEL:(j + 1) * D_MODEL]))
        m = m + gate * br
    o_ref[0] = x + g1_ref[0] * _dot(m.astype(BF16), wout_ref[...])


def _merge_call(x, oa, ys, of, ob, mod_sh, mod_sc, mod_g, mod_row, n1, w, tm):
    Bn, Tn, D = x.shape
    row = (lambda b, i: (b, 0, 0)) if mod_row is None else (lambda b, i: (mod_row, 0, 0))
    tok = lambda width: pl.BlockSpec((1, tm, width), lambda b, i: (b, i, 0))
    mrow = pl.BlockSpec((1, 1, D), row)
    weights = [w["wgate"], w["wr"], w["wmo"], w["wso"], w["gn"], w["wgo"], w["wout"]]
    return pl.pallas_call(
        _merge_kernel,
        out_shape=jax.ShapeDtypeStruct((Bn, Tn, D), F32),
        grid=(Bn, Tn // tm),
        in_specs=[tok(D), tok(MLA_WIDTH), tok(SSM_WIDTH), tok(GLA_WIDTH), tok(GLA_WIDTH),
                  mrow, mrow, mrow, _const_spec((1, D))] + [_const_spec(a.shape) for a in weights],
        out_specs=tok(D),
        compiler_params=_params(("parallel", "parallel")),
        name="merge",
    )(x, oa, ys, of, ob, mod_sh, mod_sc, mod_g, n1, *weights)


def _ffn_kernel(x_ref, sh_ref, sc_ref, g2_ref, n2_ref, wg_ref, wu_ref, wd_ref, fn_ref, o_ref, *, final):
    x = x_ref[0]
    h = _rms(x) * n2_ref[...]
    hb = (h * (1.0 + sc_ref[0]) + sh_ref[0]).astype(BF16)
    acc = jnp.zeros_like(x)
    for cs in (slice(0, FFN_SPLIT), slice(FFN_SPLIT, FFN_HIDDEN)):
        g = _dot(hb, wg_ref[:, cs])
        u = _dot(hb, wu_ref[:, cs])
        acc = acc + _dot(((g * jax.nn.sigmoid(g)) * u).astype(BF16), wd_ref[cs, :])
    y = x + g2_ref[0] * acc
    if final:
        y = _rms(y) * fn_ref[...]
    o_ref[0] = y


def _ffn_call(x, mod_sh, mod_sc, mod_g, mod_row, n2, wg, wu, wd, fn, final, tm):
    Bn, Tn, D = x.shape
    row = (lambda b, i: (b, 0, 0)) if mod_row is None else (lambda b, i: (mod_row, 0, 0))
    tok = pl.BlockSpec((1, tm, D), lambda b, i: (b, i, 0))
    mrow = pl.BlockSpec((1, 1, D), row)
    return pl.pallas_call(
        functools.partial(_ffn_kernel, final=final),
        out_shape=jax.ShapeDtypeStruct((Bn, Tn, D), F32),
        grid=(Bn, Tn // tm),
        in_specs=[tok, mrow, mrow, mrow, _const_spec((1, D)), _const_spec(wg.shape), _const_spec(wu.shape),
                  _const_spec(wd.shape), _const_spec((1, D))],
        out_specs=tok,
        compiler_params=_params(("parallel", "parallel")),
        name="ffn",
    )(x, mod_sh, mod_sc, mod_g, n2, wg, wu, wd, fn)


def _pair_swap(w):
    w2 = w.reshape(w.shape[0], -1, 2)
    return jnp.stack([-w2[..., 1], w2[..., 0]], axis=-1).reshape(w.shape)


def _layer_weights(l, w_in, mla_q_norm, mla_w_uq, mla_kv_norm, mla_w_ukv, gla_w_a2, gla_b_a2):
    wi = w_in[l]
    D = wi.shape[0]
    o = 0
    cuts = {}
    for name, sz in (("cq", MLA_RANK), ("ckv", MLA_RANK), ("kr", MLA_ROPE), ("u", SSM_WIDTH), ("gq", GLA_KW),
                     ("gk", GLA_KW), ("gv", GLA_WIDTH), ("gr", GLA_WIDTH), ("af", GLA_GATE_RANK),
                     ("ab", GLA_GATE_RANK), ("gate", 3 * D_MODEL)):
        cuts[name] = wi[:, o:o + sz]
        o += sz
    zeros = lambda n: jnp.zeros((D, n), F32)
    kr = cuts["kr"]
    pad_k = HEAD_SLOT - MLA_NOPE - MLA_ROPE
    wkr = jnp.concatenate([zeros(MLA_NOPE), kr, zeros(pad_k), zeros(MLA_NOPE), _pair_swap(kr), zeros(pad_k)], axis=1)
    wlr = jnp.concatenate([cuts["af"], cuts["ab"], zeros(LR_PAD - 2 * GLA_GATE_RANK)], axis=1)
    uq = mla_w_uq[l].reshape(MLA_RANK, MLA_HEADS, MLA_NOPE + MLA_ROPE)
    zq = jnp.zeros((MLA_RANK, MLA_HEADS, pad_k), F32)
    wuq = jnp.concatenate([uq, zq], axis=2).reshape(MLA_RANK, MLA_HEADS * HEAD_SLOT)
    uq_sw = _pair_swap(uq[..., MLA_NOPE:].reshape(MLA_RANK, -1)).reshape(MLA_RANK, MLA_HEADS, MLA_ROPE)
    wuqs = jnp.concatenate([jnp.zeros((MLA_RANK, MLA_HEADS, MLA_NOPE), F32), uq_sw, zq], axis=2)
    wuqs = wuqs.reshape(MLA_RANK, MLA_HEADS * HEAD_SLOT)
    wa2 = jnp.zeros((2, LR_PAD, GLA_KW), F32)
    wa2 = wa2.at[0, :GLA_GATE_RANK].set(gla_w_a2[l, 0]).at[1, GLA_GATE_RANK:2 * GLA_GATE_RANK].set(gla_w_a2[l, 1])
    b16 = lambda a: a.astype(BF16)
    return dict(
        wa=b16(jnp.concatenate([cuts["cq"], cuts["ckv"]], axis=1)), wkr=b16(wkr), wu=b16(cuts["u"]),
        wg=b16(jnp.concatenate([cuts["gq"], cuts["gk"], cuts["gv"]], axis=1)), wlr=b16(wlr),
        qn=mla_q_norm[l].reshape(1, -1), kvn=mla_kv_norm[l].reshape(1, -1),
        wuq=b16(wuq), wuqs=b16(wuqs), wukv=b16(mla_w_ukv[l]), wa2=b16(wa2),
        ba2=gla_b_a2[l].reshape(2, 1, GLA_KW), wr=b16(cuts["gr"]), wgate=b16(cuts["gate"]))


def _rope_tables(T):
    t = jnp.arange(T, dtype=jnp.int32)
    rows = (t // GRID_W).astype(F32)
    cols = (t % GRID_W).astype(F32)
    inv = ROPE_BASE ** (-jnp.arange(ROPE_FREQS, dtype=F32) / ROPE_FREQS)
    ang = jnp.concatenate([rows[:, None] * inv, cols[:, None] * inv], axis=-1)
    ang2 = jnp.repeat(ang, 2, axis=1)
    pad_k = HEAD_SLOT - MLA_NOPE - MLA_ROPE
    cos = jnp.concatenate([jnp.ones((T, MLA_NOPE), F32), jnp.cos(ang2), jnp.ones((T, pad_k), F32)], axis=1)
    sin = jnp.concatenate([jnp.zeros((T, MLA_NOPE), F32), jnp.sin(ang2), jnp.zeros((T, pad_k), F32)], axis=1)
    return cos, sin


def _s5_matrices(bbr, bbi, c_re, c_im):
    n = SSM_N
    grp_row = jnp.arange(SSM_WIDTH)[:, None] // SSM_GROUP
    grp_col = jnp.arange(n)[None, :] // SSM_STATE
    blk = grp_row == grp_col
    tile = lambda m: jnp.where(blk, jnp.tile(m, (SSM_GROUPS, 1)), 0.0)
    hw, hn = SSM_WIDTH // 2, n // 2
    halves = lambda m: [m[k * hw:(k + 1) * hw, k * hn:(k + 1) * hn] for k in range(2)]
    bd = jnp.stack([jnp.stack([jnp.concatenate([r, i], axis=1)
                               for r, i in zip(halves(tile(bbr[d])), halves(tile(bbi[d])))])
                    for d in range(2)]).astype(BF16)

    def readout(c):
        ct = jnp.transpose(c, (0, 2, 1)).reshape(n, SSM_GROUP)
        full = jnp.where(blk.T, jnp.tile(ct, (1, SSM_GROUPS)), 0.0)
        return jnp.stack([full[k * hn:(k + 1) * hn, k * hw:(k + 1) * hw] for k in range(2)]).astype(BF16)

    return bd, readout(c_re), readout(c_im)


def kernel(x, c, ctx, c_ctx, w_mod, b_mod, norm1, norm2, w_in, mla_q_norm, mla_w_uq, mla_kv_norm, mla_w_ukv, mla_w_o, ssm_lam_re, ssm_lam_im, ssm_log_dt, ssm_b_re, ssm_b_im, ssm_c_re, ssm_c_im, ssm_d, ssm_w_glu, ssm_b_glu, ssm_w_o, gla_w_a2, gla_b_a2, gla_norm, gla_w_o, w_out, ffn_w_up, ffn_w_down, final_norm):
    B, T, D = x.shape
    C = ctx.shape[1]
    depth = w_mod.shape[0]
    n_rows = -(-(B + 1) // 8) * 8
    cvec = jnp.zeros((n_rows, D), F32).at[:B].set(c).at[B].set(c_ctx)
    mod = _mod_call(cvec, w_mod.astype(BF16), b_mod)
    cos_l, sin_l = _rope_tables(T)
    cos_c = jnp.ones((C, HEAD_SLOT), F32)
    sin_c = jnp.zeros((C, HEAD_SLOT), F32)
    tm_in, tm_merge, tm_ffn = min(TM_IN, T), min(TM_MERGE, T), min(TM_FFN, T)
    tq, tk = min(TQ, T), min(TK, T // 2)
    tg = min(GLA_BLOCK, T)
    b16 = lambda a: a.astype(BF16)

    xc = ctx
    for l in range(depth):
        with_ctx = l < depth - 1
        m6 = [mod[l, :, j * D:(j + 1) * D].reshape(n_rows, 1, D) for j in range(6)]
        sh1, sc1, g1, sh2, sc2, g2 = m6
        n1 = norm1[l].reshape(1, D)
        n2 = norm2[l].reshape(1, D)
        w = _layer_weights(l, w_in, mla_q_norm, mla_w_uq, mla_kv_norm, mla_w_ukv, gla_w_a2, gla_b_a2)
        w.update(wmo=b16(mla_w_o[l]), wso=b16(ssm_w_o[l]), gn=gla_norm[l].reshape(1, -1), wgo=b16(gla_w_o[l]),
                 wout=b16(w_out[l]))
        up = ffn_w_up[l]
        wup_g, wup_u, wdown = b16(up[:, :FFN_HIDDEN]), b16(up[:, FFN_HIDDEN:]), b16(ffn_w_down[l])
        fn = final_norm.reshape(1, D)

        qc, kc, kvc, uc, gqc, gkc, gvc, lgfc, lgbc = _in_proj_call(xc, sh1, sc1, B, n1, cos_c, sin_c, w, C)
        ql, kl, kvl, ul, gql, gkl, gvl, lgfl, lgbl = _in_proj_call(x, sh1, sc1, None, n1, cos_l, sin_l, w, tm_in)

        oa = _attn_call(ql, kc, kvc, kl, kvl, tq, tk)
        oac = _attn_call(qc, kc, kvc, None, None, C, tk) if with_ctx else None

        ar, ai, bbr, bbi = _s5_disc_call(ssm_lam_re[l], ssm_lam_im[l], ssm_log_dt[l], ssm_b_re[l], ssm_b_im[l])
        bd, cre_t, cim_t = _s5_matrices(bbr, bbi, ssm_c_re[l], ssm_c_im[l])
        uc_tb = jnp.transpose(uc, (1, 0, 2))
        ul_tb = jnp.transpose(ul, (1, 0, 2))
        h0 = jnp.zeros((2, B, 2 * SSM_N), F32)
        yfc, ybc, hc = _s5_scan_call(uc_tb, bd, ar, ai, cre_t, cim_t, h0, S5_TL)
        yfl, ybl, _ = _s5_scan_call(ul_tb, bd, ar, ai, cre_t, cim_t, hc, S5_TL)
        d_skip = ssm_d[l].reshape(1, -1)
        wglu, bglu = b16(ssm_w_glu[l]), ssm_b_glu[l].reshape(1, -1)
        flat = lambda a: a.reshape(-1, SSM_WIDTH)
        ysl = _s5_out_call(flat(ul_tb), flat(yfl), flat(ybl), d_skip, wglu, bglu, S5_ROWS_OUT)
        ysl = jnp.transpose(ysl.reshape(T, B, SSM_WIDTH), (1, 0, 2))

        s0 = jnp.zeros((B, 2, GLA_WIDTH, GLA_KW), F32)
        ofc, obc, sc_fin = _gla_call(gqc, gkc, gvc, lgfc, lgbc, s0, min(GLA_BLOCK, C))
        ofl, obl, _ = _gla_call(gql, gkl, gvl, lgfl, lgbl, sc_fin, tg)

        x = _merge_call(x, oa, ysl, ofl, obl, sh1, sc1, g1, None, n1, w, tm_merge)
        x = _ffn_call(x, sh2, sc2, g2, None, n2, wup_g, wup_u, wdown, fn, l == depth - 1, tm_ffn)
        if with_ctx:
            ysc = _s5_out_call(flat(uc_tb), flat(yfc), flat(ybc), d_skip, wglu, bglu, S5_ROWS_OUT)
            ysc = jnp.transpose(ysc.reshape(C, B, SSM_WIDTH), (1, 0, 2))
            xc = _merge_call(xc, oac, ysc, ofc, obc, sh1, sc1, g1, B, n1, w, C)
            xc = _ffn_call(xc, sh2, sc2, g2, B, n2, wup_g, wup_u, wdown, fn, False, C)
    return x
```

```python
import functools
import math

import jax
import jax.numpy as jnp
from jax import lax
from jax.experimental import pallas as pl
from jax.experimental.pallas import tpu as pltpu

F32 = jnp.float32
BF16 = jnp.bfloat16

D_MODEL = 1024
GRID_W = 64
EPS = 1e-6
MLA_HEADS = 8
MLA_RANK = 256
MLA_NOPE = 64
MLA_ROPE = 32
MLA_V = 64
MLA_WIDTH = MLA_HEADS * MLA_V
MLA_SCALE = (MLA_NOPE + MLA_ROPE) ** -0.5
ROPE_FREQS = MLA_ROPE // 4
ROPE_BASE = 10000.0
LOG2_E = math.log2(math.e)
HEAD_SLOT = 128
SSM_WIDTH = 512
SSM_GROUP = 16
SSM_GROUPS = SSM_WIDTH // SSM_GROUP
SSM_STATE = 64
SSM_N = SSM_GROUPS * SSM_STATE
GLA_HEADS = 4
GLA_DK = 64
GLA_DV = 128
GLA_KW = GLA_HEADS * GLA_DK
GLA_WIDTH = GLA_HEADS * GLA_DV
GLA_GATE_RANK = 16
GLA_GATE_TAU = 16.0
GLA_CHUNK = 64
FFN_HIDDEN = 2816
FFN_SPLIT = 1536
LR_PAD = 128

VMEM_LIMIT = 56 * 1024 * 1024

TM_IN = 512
TM_MERGE = 512
TM_FFN = 512
TQ = 512
TK = 1024
ATT_STRIP = 64
S5_TL = 64
S5_PARTS = 2
S5_ROWS_OUT = 2048
GLA_BLOCK = 256


def _const_spec(shape):
    nd = len(shape)
    return pl.BlockSpec(shape, lambda *_: (0,) * nd, pipeline_mode=pl.Buffered(1))


def _params(sem):
    return pltpu.CompilerParams(dimension_semantics=sem, vmem_limit_bytes=VMEM_LIMIT)


def _rms(x):
    return x * lax.rsqrt(jnp.mean(x * x, axis=-1, keepdims=True) + EPS)


def _dot(a, b):
    return jnp.dot(a, b, preferred_element_type=F32)


def _dot_nt(a, b):
    return lax.dot_general(a, b, (((1,), (1,)), ((), ())), preferred_element_type=F32)


def _dot_tn(a, b):
    return lax.dot_general(a, b, (((0,), (0,)), ((), ())), preferred_element_type=F32)


def _mod_kernel(c_ref, w_ref, b_ref, o_ref):
    c = c_ref[...]
    s = (c * jax.nn.sigmoid(c)).astype(BF16)
    o_ref[0] = _dot(s, w_ref[0]) + b_ref[0]


def _mod_call(cvec, w_mod, b_mod):
    L, D, N = w_mod.shape
    R = cvec.shape[0]
    tn = 1536
    return pl.pallas_call(
        _mod_kernel,
        out_shape=jax.ShapeDtypeStruct((L, R, N), F32),
        grid=(L, N // tn),
        in_specs=[pl.BlockSpec((R, D), lambda l, j: (0, 0)),
                  pl.BlockSpec((1, D, tn), lambda l, j: (l, 0, j)),
                  pl.BlockSpec((1, 1, tn), lambda l, j: (l, 0, j))],
        out_specs=pl.BlockSpec((1, R, tn), lambda l, j: (l, 0, j)),
        compiler_params=_params(("arbitrary", "arbitrary")),
        name="mod",
    )(cvec, w_mod, b_mod.reshape(L, 1, N))


def _in_proj_kernel(x_ref, sh_ref, sc_ref, n1_ref, cos_ref, sin_ref,
                    wa_ref, wkr_ref, wu_ref, wg_ref, wlr_ref,
                    qn_ref, kvn_ref, wuq_ref, wuqs_ref, wukv_ref, wa2_ref, ba2_ref,
                    q_ref, k_ref, kv_ref, u_ref, gq_ref, gk_ref, gv_ref, lgf_ref, lgb_ref):
    x = x_ref[0]
    h = _rms(x) * n1_ref[...]
    h = h * (1.0 + sc_ref[0]) + sh_ref[0]
    hb = h.astype(BF16)
    cos = cos_ref[...]
    sin = sin_ref[...]
    cos8 = jnp.concatenate([cos] * MLA_HEADS, axis=1)
    sin8 = jnp.concatenate([sin] * MLA_HEADS, axis=1)

    za = _dot(hb, wa_ref[...])
    cqn = (_rms(za[:, :MLA_RANK]) * qn_ref[...]).astype(BF16)
    q = _dot(cqn, wuq_ref[...]) * cos8 + _dot(cqn, wuqs_ref[...]) * sin8
    q_ref[0] = (q * (MLA_SCALE * LOG2_E)).astype(BF16)

    ckvn = (_rms(za[:, MLA_RANK:]) * kvn_ref[...]).astype(BF16)
    kv = _dot(ckvn, wukv_ref[...])
    kr2 = _dot(hb, wkr_ref[...])
    krr = kr2[:, :HEAD_SLOT] * cos + kr2[:, HEAD_SLOT:] * sin
    nope = lax.broadcasted_iota(jnp.int32, krr.shape, 1) < MLA_NOPE
    for hd in range(MLA_HEADS):
        sl = slice(hd * HEAD_SLOT, (hd + 1) * HEAD_SLOT)
        k_ref[0, :, sl] = jnp.where(nope, kv[:, sl], krr).astype(BF16)
        kv_ref[0, :, sl] = jnp.where(nope, 1.0, kv[:, sl]).astype(BF16)

    u_ref[0] = _dot(hb, wu_ref[...])
    zg = _dot(hb, wg_ref[...])
    gq_ref[0] = zg[:, :GLA_KW]
    gk_ref[0] = zg[:, GLA_KW:2 * GLA_KW]
    gv_ref[0] = zg[:, 2 * GLA_KW:]
    zlr = _dot(hb, wlr_ref[...]).astype(BF16)
    for d, o_ref in enumerate((lgf_ref, lgb_ref)):
        z = _dot(zlr, wa2_ref[d]) + ba2_ref[d]
        o_ref[0] = (jnp.minimum(z, 0.0) - jnp.log1p(jnp.exp(-jnp.abs(z)))) * (1.0 / GLA_GATE_TAU)


def _in_proj_call(x, mod_sh, mod_sc, mod_row, n1, cos_t, sin_t, w, tm):
    Bn, Tn, D = x.shape
    nt = Tn // tm
    row = (lambda b, i: (b, 0, 0)) if mod_row is None else (lambda b, i: (mod_row, 0, 0))
    tok = lambda width: pl.BlockSpec((1, tm, width), lambda b, i: (b, i, 0))
    weights = [w["wa"], w["wkr"], w["wu"], w["wg"], w["wlr"], w["qn"], w["kvn"],
               w["wuq"], w["wuqs"], w["wukv"], w["wa2"], w["ba2"]]
    outs = [(D, BF16), (D, BF16), (D, BF16), (SSM_WIDTH, F32), (GLA_KW, F32), (GLA_KW, F32),
            (GLA_WIDTH, F32), (GLA_KW, F32), (GLA_KW, F32)]
    out_shape = [jax.ShapeDtypeStruct((Bn, Tn, wd), dt) for wd, dt in outs]
    out_specs = [tok(wd) for wd, _ in outs]
    return pl.pallas_call(
        _in_proj_kernel,
        out_shape=out_shape,
        grid=(Bn, nt),
        in_specs=[tok(D),
                  pl.BlockSpec((1, 1, D), row), pl.BlockSpec((1, 1, D), row),
                  _const_spec((1, D)),
                  pl.BlockSpec((tm, HEAD_SLOT), lambda b, i: (i, 0)),
                  pl.BlockSpec((tm, HEAD_SLOT), lambda b, i: (i, 0))]
                 + [_const_spec(a.shape) for a in weights],
        out_specs=out_specs,
        compiler_params=_params(("parallel", "parallel")),
        name="in_proj",
    )(x, mod_sh, mod_sc, n1, cos_t, sin_t, *weights)


def _lane_fold(x, op):
    out = x[:, :HEAD_SLOT]
    for j in range(1, x.shape[1] // HEAD_SLOT):
        out = op(out, x[:, j * HEAD_SLOT:(j + 1) * HEAD_SLOT])
    return out


def _attn_update(s, kvt, carry):
    m, acc = carry
    m_new = jnp.maximum(m, jnp.max(_lane_fold(s, jnp.maximum), axis=-1, keepdims=True))
    p = jnp.exp2(s - m_new)
    acc = jnp.exp2(m - m_new) * acc + _dot(p.astype(BF16), kvt)
    return m_new, acc


def _attn_kernel(*refs, n_lat_tiles, tk):
    if n_lat_tiles:
        q_ref, kc_ref, kvc_ref, kl_ref, kvl_ref, o_ref, s_ref, mx_ref, p_ref, m_ref, acc_ref = refs
    else:
        q_ref, kc_ref, kvc_ref, o_ref = refs
    tq = q_ref.shape[1]
    sls = [slice(hh * HEAD_SLOT, (hh + 1) * HEAD_SLOT) for hh in range(2)]
    qs = [q_ref[0, :, sl] for sl in sls]
    init = (jnp.full((tq, 1), -jnp.inf, F32), jnp.zeros((tq, HEAD_SLOT), F32))
    carry = tuple(_attn_update(_dot_nt(qs[hh], kc_ref[0, :, sls[hh]]), kvc_ref[0, :, sls[hh]], init)
                  for hh in range(2))
    if n_lat_tiles:
        for hh in range(2):
            m, acc = carry[hh]
            m_ref[hh] = jnp.broadcast_to(m, (tq, HEAD_SLOT))
            acc_ref[hh] = acc
        rows = lambda i: pl.ds(pl.multiple_of(i * tk, tk), tk)

        def scores(i, slot):
            for hh in range(2):
                s = _dot_nt(qs[hh], kl_ref[0, rows(i), sls[hh]])
                s_ref[slot, hh] = s
                mx_ref[slot, hh] = _lane_fold(s, jnp.maximum)

        def update(i, slot):
            for hh in range(2):
                for r in range(0, tq, ATT_STRIP):
                    rs = slice(r, r + ATT_STRIP)
                    m_old = m_ref[hh, rs, :]
                    m_new = jnp.maximum(m_old, jnp.max(mx_ref[slot, hh, rs, :], axis=-1, keepdims=True))
                    p = jnp.exp2(s_ref[slot, hh, rs, :] - jnp.concatenate([m_new] * (tk // HEAD_SLOT), axis=1))
                    m_ref[hh, rs, :] = m_new
                    acc_ref[hh, rs, :] = jnp.exp2(m_old - m_new) * acc_ref[hh, rs, :]
                    p_ref[hh, rs, :] = p.astype(BF16)
                acc_ref[hh] += _dot(p_ref[hh], kvl_ref[0, rows(i), sls[hh]])

        scores(0, 0)

        def body(j, c):
            i = 2 * j
            scores(i + 1, 1)
            update(i, 0)
            scores(jnp.minimum(i + 2, n_lat_tiles - 1), 0)
            update(i + 1, 1)
            return c

        lax.fori_loop(0, n_lat_tiles // 2, body, 0)
        carry = tuple((None, acc_ref[hh]) for hh in range(2))
    outs = [(acc * pltpu.roll(1.0 / acc, MLA_NOPE, 1))[:, MLA_NOPE:] for _, acc in carry]
    o_ref[0] = jnp.concatenate(outs, axis=1).astype(o_ref.dtype)


def _attn_call(q, k_ctx, kv_ctx, k_lat, kv_lat, tq, tk):
    Bn, Tq, D = q.shape
    C = k_ctx.shape[1]
    hp = MLA_HEADS // 2
    w2 = 2 * HEAD_SLOT
    n_lat_tiles = 0 if k_lat is None else k_lat.shape[1] // tk
    in_specs = [pl.BlockSpec((1, tq, w2), lambda b, h, i: (b, i, h)),
                pl.BlockSpec((1, C, w2), lambda b, h, i: (b, 0, h)),
                pl.BlockSpec((1, C, w2), lambda b, h, i: (b, 0, h))]
    args = [q, k_ctx, kv_ctx]
    scratch = []
    if n_lat_tiles:
        assert n_lat_tiles % 2 == 0 and tq % ATT_STRIP == 0
        Tk = k_lat.shape[1]
        in_specs += [pl.BlockSpec((1, Tk, w2), lambda b, h, i: (b, 0, h)),
                     pl.BlockSpec((1, Tk, w2), lambda b, h, i: (b, 0, h))]
        args += [k_lat, kv_lat]
        stat = pltpu.VMEM((2, tq, HEAD_SLOT), F32)
        scratch = [pltpu.VMEM((2, 2, tq, tk), F32), pltpu.VMEM((2, 2, tq, HEAD_SLOT), F32),
                   pltpu.VMEM((2, tq, tk), BF16), stat, stat]
    return pl.pallas_call(
        functools.partial(_attn_kernel, n_lat_tiles=n_lat_tiles, tk=tk),
        out_shape=jax.ShapeDtypeStruct((Bn, Tq, MLA_WIDTH), BF16),
        grid=(Bn, hp, Tq // tq),
        in_specs=in_specs,
        out_specs=pl.BlockSpec((1, tq, 2 * MLA_V), lambda b, h, i: (b, i, h)),
        scratch_shapes=scratch,
        compiler_params=_params(("parallel", "parallel", "parallel")),
        name="attn",
    )(*args)


def _s5_disc_kernel(lr_ref, li_ref, ldt_ref, br_ref, bi_ref, ar_ref, ai_ref, bbr_ref, bbi_ref):
    lr = lr_ref[0]
    li = li_ref[0]
    dt = jnp.exp(ldt_ref[0])
    mag = jnp.exp(dt * lr)
    ar = mag * jnp.cos(dt * li)
    ai = mag * jnp.sin(dt * li)
    den = lr * lr + li * li
    fr = ((ar - 1.0) * lr + ai * li) / den
    fi = (ai * lr - (ar - 1.0) * li) / den
    ar_ref[0] = ar
    ai_ref[0] = ai
    br = br_ref[...]
    bi = bi_ref[...]
    bbr_ref[0] = fr * br - fi * bi
    bbi_ref[0] = fr * bi + fi * br


def _s5_disc_call(lam_re, lam_im, log_dt, b_re, b_im):
    n = SSM_N
    lr = lam_re.reshape(2, 1, n)
    li = lam_im.reshape(2, 1, n)
    ldt = jnp.repeat(log_dt, SSM_STATE, axis=1).reshape(2, 1, n)
    br = b_re.reshape(n, SSM_GROUP).T
    bi = b_im.reshape(n, SSM_GROUP).T
    vec = pl.BlockSpec((1, 1, n), lambda d: (d, 0, 0))
    mat = pl.BlockSpec((SSM_GROUP, n), lambda d: (0, 0))
    omat = pl.BlockSpec((1, SSM_GROUP, n), lambda d: (d, 0, 0))
    return pl.pallas_call(
        _s5_disc_kernel,
        out_shape=[jax.ShapeDtypeStruct((2, 1, n), F32), jax.ShapeDtypeStruct((2, 1, n), F32),
                   jax.ShapeDtypeStruct((2, SSM_GROUP, n), F32), jax.ShapeDtypeStruct((2, SSM_GROUP, n), F32)],
        grid=(2,),
        in_specs=[vec, vec, vec, mat, mat],
        out_specs=[vec, vec, omat, omat],
        compiler_params=_params(("arbitrary",)),
        name="s5_disc",
    )(lr, li, ldt, br, bi)


def _s5_scan_kernel(uf_ref, ub_ref, bd_ref, ar_ref, ai_ref, cre_ref, cim_ref, h0_ref,
                    yf_ref, yb_ref, hfin_ref, xs_ref, h_ref, *, tl, nb):
    i = pl.program_id(0)
    n = SSM_N

    @pl.when(i == 0)
    def _():
        h_ref[...] = h0_ref[...]

    half = n // 2
    hw = SSM_WIDTH // 2
    tp = tl // S5_PARTS

    def part_start(d, part):
        return part * tp if d == 0 else tl - (part + 1) * tp

    for part in range(S5_PARTS):
        for d, u_ref in enumerate((uf_ref, ub_ref)):
            t0 = part_start(d, part)
            rows = slice(t0 * nb, (t0 + tp) * nb)
            u = u_ref[t0:t0 + tp].reshape(tp * nb, SSM_WIDTH).astype(BF16)
            for k in range(2):
                xk = _dot(u[:, k * hw:(k + 1) * hw], bd_ref[d, k])
                xs_ref[d, rows, k * half:(k + 1) * half] = xk[:, :half]
                xs_ref[d, rows, n + k * half:n + (k + 1) * half] = xk[:, half:]

    state = {}
    for d in range(2):
        for cb in range(2):
            re = slice(cb * half, (cb + 1) * half)
            im = slice(n + cb * half, n + (cb + 1) * half)
            state[d, cb] = (h_ref[d, :, re], h_ref[d, :, im], jnp.broadcast_to(ar_ref[d, :, re], (nb, half)),
                            jnp.broadcast_to(ai_ref[d, :, re], (nb, half)), re, im)
    for t in range(tl):
        for d in range(2):
            tt = t if d == 0 else tl - 1 - t
            rows = slice(tt * nb, (tt + 1) * nb)
            for cb in range(2):
                hr, hi, ar, ai, re, im = state[d, cb]
                nr = ar * hr - ai * hi + xs_ref[d, rows, re]
                ni = ar * hi + ai * hr + xs_ref[d, rows, im]
                xs_ref[d, rows, re] = nr
                xs_ref[d, rows, im] = ni
                state[d, cb] = (nr, ni, ar, ai, re, im)
    for (d, cb), (hr, hi, _, _, re, im) in state.items():
        h_ref[d, :, re] = hr
        h_ref[d, :, im] = hi

    for part in range(S5_PARTS):
        for d, y_ref in enumerate((yf_ref, yb_ref)):
            t0 = part_start(d, part)
            rows = slice(t0 * nb, (t0 + tp) * nb)
            ys = []
            for k in range(2):
                hr = xs_ref[d, rows, k * half:(k + 1) * half].astype(BF16)
                hi = xs_ref[d, rows, n + k * half:n + (k + 1) * half].astype(BF16)
                ys.append(_dot(hr, cre_ref[k]) - _dot(hi, cim_ref[k]))
            y_ref[t0:t0 + tp] = jnp.concatenate(ys, axis=1).reshape(tp, nb, SSM_WIDTH)

    @pl.when(i == pl.num_programs(0) - 1)
    def _():
        hfin_ref[...] = h_ref[...]


def _s5_scan_call(u_tb, bd, ar, ai, cre_t, cim_t, h0, tl):
    Tn, nb, W = u_tb.shape
    nc = Tn // tl
    n = SSM_N
    fwd = pl.BlockSpec((tl, nb, W), lambda i: (i, 0, 0))
    bwd = pl.BlockSpec((tl, nb, W), lambda i: (nc - 1 - i, 0, 0))
    return pl.pallas_call(
        functools.partial(_s5_scan_kernel, tl=tl, nb=nb),
        out_shape=[jax.ShapeDtypeStruct((Tn, nb, W), F32), jax.ShapeDtypeStruct((Tn, nb, W), F32),
                   jax.ShapeDtypeStruct((2, nb, 2 * n), F32)],
        grid=(nc,),
        in_specs=[fwd, bwd, _const_spec(bd.shape), _const_spec(ar.shape), _const_spec(ai.shape),
                  _const_spec(cre_t.shape), _const_spec(cim_t.shape), _const_spec(h0.shape)],
        out_specs=[fwd, bwd, pl.BlockSpec((2, nb, 2 * n), lambda i: (0, 0, 0))],
        scratch_shapes=[pltpu.VMEM((2, tl * nb, 2 * n), F32), pltpu.VMEM((2, nb, 2 * n), F32)],
        compiler_params=_params(("arbitrary",)),
        name="s5_scan",
    )(u_tb, u_tb, bd, ar, ai, cre_t, cim_t, h0)


def _s5_out_kernel(u_ref, yf_ref, yb_ref, d_ref, wglu_ref, bglu_ref, o_ref):
    y = d_ref[...] * u_ref[...] + yf_ref[...] + yb_ref[...]
    g = jax.nn.gelu(y)
    z = _dot(g.astype(BF16), wglu_ref[...]) + bglu_ref[...]
    o_ref[...] = (g * jax.nn.sigmoid(z)).astype(o_ref.dtype)


def _s5_out_call(u, yf, yb, d_skip, wglu, bglu, tr):
    R, W = u.shape
    tr = min(tr, R)
    blk = pl.BlockSpec((tr, W), lambda i: (i, 0))
    return pl.pallas_call(
        _s5_out_kernel,
        out_shape=jax.ShapeDtypeStruct((R, W), BF16),
        grid=(R // tr,),
        in_specs=[blk, blk, blk, _const_spec(d_skip.shape), _const_spec(wglu.shape), _const_spec(bglu.shape)],
        out_specs=blk,
        compiler_params=_params(("parallel",)),
        name="s5_out",
    )(u, yf, yb, d_skip, wglu, bglu)


def _split3(x):
    hi = x.astype(BF16)
    r1 = x - hi.astype(F32)
    mid = r1.astype(BF16)
    lo = (r1 - mid.astype(F32)).astype(BF16)
    return hi, mid, lo


def _gla_block(q, k, v, lg, s, d, cst, n_chunks):
    tri_ref, mask_ref, same_ref, hmask_ref, bmask_ref = cst
    L = GLA_CHUNK
    parts = _split3(lg)
    b = sum(_dot(tri_ref[d], p) for p in parts)
    tot = sum(_dot(same_ref[...], p) for p in parts)
    q_in = q * (GLA_DK ** -0.5) * jnp.exp(b)
    k_in = (k * jnp.exp(-b)).astype(BF16)
    k_dec = (k * jnp.exp(tot - b)).astype(BF16)
    vb = v.astype(BF16)
    mask = mask_ref[d] > 0.0
    outs = []
    for hd in range(GLA_HEADS):
        qh = (q_in * hmask_ref[hd]).astype(BF16)
        att = jnp.where(mask, _dot_nt(qh, k_in), 0.0).astype(BF16)
        outs.append(_dot(att, vb[:, hd * GLA_DV:(hd + 1) * GLA_DV]))
    o_intra = jnp.concatenate(outs, axis=1)
    q_in_b = q_in.astype(BF16)
    bmask = bmask_ref[...]
    o_rows = [None] * n_chunks
    for c in (range(n_chunks) if d == 0 else reversed(range(n_chunks))):
        rows = slice(c * L, (c + 1) * L)
        o_rows[c] = _dot_nt(q_in_b[rows], s.astype(BF16))
        dec = jnp.exp(tot[c * L:c * L + 1, :])
        s = dec * s + _dot_tn(vb[rows], k_dec[rows]) * bmask
    return o_intra + jnp.concatenate(o_rows, axis=0), s


def _gla_kernel(qf_ref, kf_ref, vf_ref, lgf_ref, qb_ref, kb_ref, vb_ref, lgb_ref, s0_ref,
                tri_ref, mask_ref, same_ref, hmask_ref, bmask_ref,
                of_ref, ob_ref, sfin_ref, s_ref, *, n_chunks):
    i = pl.program_id(1)
    cst = (tri_ref, mask_ref, same_ref, hmask_ref, bmask_ref)

    @pl.when(i == 0)
    def _():
        s_ref[...] = s0_ref[0]

    o, s = _gla_block(qf_ref[0], kf_ref[0], vf_ref[0], lgf_ref[0], s_ref[0], 0, cst, n_chunks)
    of_ref[0] = o
    s_ref[0] = s
    o, s = _gla_block(qb_ref[0], kb_ref[0], vb_ref[0], lgb_ref[0], s_ref[1], 1, cst, n_chunks)
    ob_ref[0] = o
    s_ref[1] = s

    @pl.when(i == pl.num_programs(1) - 1)
    def _():
        sfin_ref[0] = s_ref[...]


def _gla_consts(tg):
    t = jnp.arange(tg)
    same = (t[:, None] // GLA_CHUNK) == (t[None, :] // GLA_CHUNK)
    low = same & (t[None, :] <= t[:, None])
    upp = same & (t[None, :] >= t[:, None])
    tri = jnp.stack([low, upp])
    hmask = (jnp.arange(GLA_KW)[None, None, :] // GLA_DK) == jnp.arange(GLA_HEADS)[:, None, None]
    bmask = (jnp.arange(GLA_WIDTH)[:, None] // GLA_DV) == (jnp.arange(GLA_KW)[None, :] // GLA_DK)
    return tri.astype(BF16), tri.astype(F32), same.astype(BF16), hmask.astype(F32), bmask.astype(F32)


def _gla_call(q, k, v, lgf, lgb, s0, tg):
    Bn, Tn, _ = q.shape
    nblk = Tn // tg
    consts = _gla_consts(tg)
    fwd = lambda w: pl.BlockSpec((1, tg, w), lambda b, i: (b, i, 0))
    bwd = lambda w: pl.BlockSpec((1, tg, w), lambda b, i: (b, nblk - 1 - i, 0))
    st = pl.BlockSpec((1, 2, GLA_WIDTH, GLA_KW), lambda b, i: (b, 0, 0, 0))
    return pl.pallas_call(
        functools.partial(_gla_kernel, n_chunks=tg // GLA_CHUNK),
        out_shape=[jax.ShapeDtypeStruct((Bn, Tn, GLA_WIDTH), F32), jax.ShapeDtypeStruct((Bn, Tn, GLA_WIDTH), F32),
                   jax.ShapeDtypeStruct((Bn, 2, GLA_WIDTH, GLA_KW), F32)],
        grid=(Bn, nblk),
        in_specs=[fwd(GLA_KW), fwd(GLA_KW), fwd(GLA_WIDTH), fwd(GLA_KW),
                  bwd(GLA_KW), bwd(GLA_KW), bwd(GLA_WIDTH), bwd(GLA_KW), st]
                 + [_const_spec(a.shape) for a in consts],
        out_specs=[fwd(GLA_WIDTH), bwd(GLA_WIDTH), st],
        scratch_shapes=[pltpu.VMEM((2, GLA_WIDTH, GLA_KW), F32)],
        compiler_params=_params(("parallel", "arbitrary")),
        name="gla",
    )(q, k, v, lgf, q, k, v, lgb, s0, *consts)


def _merge_kernel(x_ref, oa_ref, ys_ref, of_ref, ob_ref, sh_ref, sc_ref, g1_ref, n1_ref,
                  wgate_ref, wr_ref, wmo_ref, wso_ref, gn_ref, wgo_ref, wout_ref, o_ref):
    x = x_ref[0]
    h = _rms(x) * n1_ref[...]
    hb = (h * (1.0 + sc_ref[0]) + sh_ref[0]).astype(BF16)
    ya = _dot(oa_ref[0], wmo_ref[...])
    yb = _dot(ys_ref[0], wso_ref[...])
    o = of_ref[0] + ob_ref[0]
    on = jnp.concatenate([_rms(o[:, hd * GLA_DV:(hd + 1) * GLA_DV]) for hd in range(GLA_HEADS)], axis=1)
    r = _dot(hb, wr_ref[...])
    yg = _dot(((on * gn_ref[...]) * (r * jax.nn.sigmoid(r))).astype(BF16), wgo_ref[...])
    m = jnp.zeros_like(ya)
    for j, br in enumerate((ya, yb, yg)):
        gate = jax.nn.sigmoid(_dot(hb, wgate_ref[:, j * D_MODEL:(j + 1) * D_MODEL]))
        m = m + gate * br
    o_ref[0] = x + g1_ref[0] * _dot(m.astype(BF16), wout_ref[...])


def _merge_call(x, oa, ys, of, ob, mod_sh, mod_sc, mod_g, mod_row, n1, w, tm):
    Bn, Tn, D = x.shape
    row = (lambda b, i: (b, 0, 0)) if mod_row is None else (lambda b, i: (mod_row, 0, 0))
    tok = lambda width: pl.BlockSpec((1, tm, width), lambda b, i: (b, i, 0))
    mrow = pl.BlockSpec((1, 1, D), row)
    weights = [w["wgate"], w["wr"], w["wmo"], w["wso"], w["gn"], w["wgo"], w["wout"]]
    return pl.pallas_call(
        _merge_kernel,
        out_shape=jax.ShapeDtypeStruct((Bn, Tn, D), F32),
        grid=(Bn, Tn // tm),
        in_specs=[tok(D), tok(MLA_WIDTH), tok(SSM_WIDTH), tok(GLA_WIDTH), tok(GLA_WIDTH),
                  mrow, mrow, mrow, _const_spec((1, D))] + [_const_spec(a.shape) for a in weights],
        out_specs=tok(D),
        compiler_params=_params(("parallel", "parallel")),
        name="merge",
    )(x, oa, ys, of, ob, mod_sh, mod_sc, mod_g, n1, *weights)


def _ffn_kernel(x_ref, sh_ref, sc_ref, g2_ref, n2_ref, wg_ref, wu_ref, wd_ref, fn_ref, o_ref, *, final):
    x = x_ref[0]
    h = _rms(x) * n2_ref[...]
    hb = (h * (1.0 + sc_ref[0]) + sh_ref[0]).astype(BF16)
    acc = jnp.zeros_like(x)
    for cs in (slice(0, FFN_SPLIT), slice(FFN_SPLIT, FFN_HIDDEN)):
        g = _dot(hb, wg_ref[:, cs])
        u = _dot(hb, wu_ref[:, cs])
        acc = acc + _dot(((g * jax.nn.sigmoid(g)) * u).astype(BF16), wd_ref[cs, :])
    y = x + g2_ref[0] * acc
    if final:
        y = _rms(y) * fn_ref[...]
    o_ref[0] = y


def _ffn_call(x, mod_sh, mod_sc, mod_g, mod_row, n2, wg, wu, wd, fn, final, tm):
    Bn, Tn, D = x.shape
    row = (lambda b, i: (b, 0, 0)) if mod_row is None else (lambda b, i: (mod_row, 0, 0))
    tok = pl.BlockSpec((1, tm, D), lambda b, i: (b, i, 0))
    mrow = pl.BlockSpec((1, 1, D), row)
    return pl.pallas_call(
        functools.partial(_ffn_kernel, final=final),
        out_shape=jax.ShapeDtypeStruct((Bn, Tn, D), F32),
        grid=(Bn, Tn // tm),
        in_specs=[tok, mrow, mrow, mrow, _const_spec((1, D)), _const_spec(wg.shape), _const_spec(wu.shape),
                  _const_spec(wd.shape), _const_spec((1, D))],
        out_specs=tok,
        compiler_params=_params(("parallel", "parallel")),
        name="ffn",
    )(x, mod_sh, mod_sc, mod_g, n2, wg, wu, wd, fn)


def _pair_swap(w):
    w2 = w.reshape(w.shape[0], -1, 2)
    return jnp.stack([-w2[..., 1], w2[..., 0]], axis=-1).reshape(w.shape)


def _layer_weights(l, w_in, mla_q_norm, mla_w_uq, mla_kv_norm, mla_w_ukv, gla_w_a2, gla_b_a2):
    wi = w_in[l]
    D = wi.shape[0]
    o = 0
    cuts = {}
    for name, sz in (("cq", MLA_RANK), ("ckv", MLA_RANK), ("kr", MLA_ROPE), ("u", SSM_WIDTH), ("gq", GLA_KW),
                     ("gk", GLA_KW), ("gv", GLA_WIDTH), ("gr", GLA_WIDTH), ("af", GLA_GATE_RANK),
                     ("ab", GLA_GATE_RANK), ("gate", 3 * D_MODEL)):
        cuts[name] = wi[:, o:o + sz]
        o += sz
    zeros = lambda n: jnp.zeros((D, n), F32)
    kr = cuts["kr"]
    pad_k = HEAD_SLOT - MLA_NOPE - MLA_ROPE
    wkr = jnp.concatenate([zeros(MLA_NOPE), kr, zeros(pad_k), zeros(MLA_NOPE), _pair_swap(kr), zeros(pad_k)], axis=1)
    wlr = jnp.concatenate([cuts["af"], cuts["ab"], zeros(LR_PAD - 2 * GLA_GATE_RANK)], axis=1)
    uq = mla_w_uq[l].reshape(MLA_RANK, MLA_HEADS, MLA_NOPE + MLA_ROPE)
    zq = jnp.zeros((MLA_RANK, MLA_HEADS, pad_k), F32)
    wuq = jnp.concatenate([uq, zq], axis=2).reshape(MLA_RANK, MLA_HEADS * HEAD_SLOT)
    uq_sw = _pair_swap(uq[..., MLA_NOPE:].reshape(MLA_RANK, -1)).reshape(MLA_RANK, MLA_HEADS, MLA_ROPE)
    wuqs = jnp.concatenate([jnp.zeros((MLA_RANK, MLA_HEADS, MLA_NOPE), F32), uq_sw, zq], axis=2)
    wuqs = wuqs.reshape(MLA_RANK, MLA_HEADS * HEAD_SLOT)
    wa2 = jnp.zeros((2, LR_PAD, GLA_KW), F32)
    wa2 = wa2.at[0, :GLA_GATE_RANK].set(gla_w_a2[l, 0]).at[1, GLA_GATE_RANK:2 * GLA_GATE_RANK].set(gla_w_a2[l, 1])
    b16 = lambda a: a.astype(BF16)
    return dict(
        wa=b16(jnp.concatenate([cuts["cq"], cuts["ckv"]], axis=1)), wkr=b16(wkr), wu=b16(cuts["u"]),
        wg=b16(jnp.concatenate([cuts["gq"], cuts["gk"], cuts["gv"]], axis=1)), wlr=b16(wlr),
        qn=mla_q_norm[l].reshape(1, -1), kvn=mla_kv_norm[l].reshape(1, -1),
        wuq=b16(wuq), wuqs=b16(wuqs), wukv=b16(mla_w_ukv[l]), wa2=b16(wa2),
        ba2=gla_b_a2[l].reshape(2, 1, GLA_KW), wr=b16(cuts["gr"]), wgate=b16(cuts["gate"]))


def _rope_tables(T):
    t = jnp.arange(T, dtype=jnp.int32)
    rows = (t // GRID_W).astype(F32)
    cols = (t % GRID_W).astype(F32)
    inv = ROPE_BASE ** (-jnp.arange(ROPE_FREQS, dtype=F32) / ROPE_FREQS)
    ang = jnp.concatenate([rows[:, None] * inv, cols[:, None] * inv], axis=-1)
    ang2 = jnp.repeat(ang, 2, axis=1)
    pad_k = HEAD_SLOT - MLA_NOPE - MLA_ROPE
    cos = jnp.concatenate([jnp.ones((T, MLA_NOPE), F32), jnp.cos(ang2), jnp.ones((T, pad_k), F32)], axis=1)
    sin = jnp.concatenate([jnp.zeros((T, MLA_NOPE), F32), jnp.sin(ang2), jnp.zeros((T, pad_k), F32)], axis=1)
    return cos, sin


def _s5_matrices(bbr, bbi, c_re, c_im):
    n = SSM_N
    grp_row = jnp.arange(SSM_WIDTH)[:, None] // SSM_GROUP
    grp_col = jnp.arange(n)[None, :] // SSM_STATE
    blk = grp_row == grp_col
    tile = lambda m: jnp.where(blk, jnp.tile(m, (SSM_GROUPS, 1)), 0.0)
    hw, hn = SSM_WIDTH // 2, n // 2
    halves = lambda m: [m[k * hw:(k + 1) * hw, k * hn:(k + 1) * hn] for k in range(2)]
    bd = jnp.stack([jnp.stack([jnp.concatenate([r, i], axis=1)
                               for r, i in zip(halves(tile(bbr[d])), halves(tile(bbi[d])))])
                    for d in range(2)]).astype(BF16)

    def readout(c):
        ct = jnp.transpose(c, (0, 2, 1)).reshape(n, SSM_GROUP)
        full = jnp.where(blk.T, jnp.tile(ct, (1, SSM_GROUPS)), 0.0)
        return jnp.stack([full[k * hn:(k + 1) * hn, k * hw:(k + 1) * hw] for k in range(2)]).astype(BF16)

    return bd, readout(c_re), readout(c_im)


def kernel(x, c, ctx, c_ctx, w_mod, b_mod, norm1, norm2, w_in, mla_q_norm, mla_w_uq, mla_kv_norm, mla_w_ukv, mla_w_o, ssm_lam_re, ssm_lam_im, ssm_log_dt, ssm_b_re, ssm_b_im, ssm_c_re, ssm_c_im, ssm_d, ssm_w_glu, ssm_b_glu, ssm_w_o, gla_w_a2, gla_b_a2, gla_norm, gla_w_o, w_out, ffn_w_up, ffn_w_down, final_norm):
    B, T, D = x.shape
    C = ctx.shape[1]
    depth = w_mod.shape[0]
    n_rows = -(-(B + 1) // 8) * 8
    cvec = jnp.zeros((n_rows, D), F32).at[:B].set(c).at[B].set(c_ctx)
    mod = _mod_call(cvec, w_mod.astype(BF16), b_mod)
    cos_l, sin_l = _rope_tables(T)
    cos_c = jnp.ones((C, HEAD_SLOT), F32)
    sin_c = jnp.zeros((C, HEAD_SLOT), F32)
    tm_in, tm_merge, tm_ffn = min(TM_IN, T), min(TM_MERGE, T), min(TM_FFN, T)
    tq, tk = min(TQ, T), min(TK, T // 2)
    tg = min(GLA_BLOCK, T)
    b16 = lambda a: a.astype(BF16)

    xc = ctx
    for l in range(depth):
        with_ctx = l < depth - 1
        m6 = [mod[l, :, j * D:(j + 1) * D].reshape(n_rows, 1, D) for j in range(6)]
        sh1, sc1, g1, sh2, sc2, g2 = m6
        n1 = norm1[l].reshape(1, D)
        n2 = norm2[l].reshape(1, D)
        w = _layer_weights(l, w_in, mla_q_norm, mla_w_uq, mla_kv_norm, mla_w_ukv, gla_w_a2, gla_b_a2)
        w.update(wmo=b16(mla_w_o[l]), wso=b16(ssm_w_o[l]), gn=gla_norm[l].reshape(1, -1), wgo=b16(gla_w_o[l]),
                 wout=b16(w_out[l]))
        up = ffn_w_up[l]
        wup_g, wup_u, wdown = b16(up[:, :FFN_HIDDEN]), b16(up[:, FFN_HIDDEN:]), b16(ffn_w_down[l])
        fn = final_norm.reshape(1, D)

        qc, kc, kvc, uc, gqc, gkc, gvc, lgfc, lgbc = _in_proj_call(xc, sh1, sc1, B, n1, cos_c, sin_c, w, C)
        ql, kl, kvl, ul, gql, gkl, gvl, lgfl, lgbl = _in_proj_call(x, sh1, sc1, None, n1, cos_l, sin_l, w, tm_in)

        oa = _attn_call(ql, kc, kvc, kl, kvl, tq, tk)
        oac = _attn_call(qc, kc, kvc, None, None, C, tk) if with_ctx else None

        ar, ai, bbr, bbi = _s5_disc_call(ssm_lam_re[l], ssm_lam_im[l], ssm_log_dt[l], ssm_b_re[l], ssm_b_im[l])
        bd, cre_t, cim_t = _s5_matrices(bbr, bbi, ssm_c_re[l], ssm_c_im[l])
        uc_tb = jnp.transpose(uc, (1, 0, 2))
        ul_tb = jnp.transpose(ul, (1, 0, 2))
        h0 = jnp.zeros((2, B, 2 * SSM_N), F32)
        yfc, ybc, hc = _s5_scan_call(uc_tb, bd, ar, ai, cre_t, cim_t, h0, S5_TL)
        yfl, ybl, _ = _s5_scan_call(ul_tb, bd, ar, ai, cre_t, cim_t, hc, S5_TL)
        d_skip = ssm_d[l].reshape(1, -1)
        wglu, bglu = b16(ssm_w_glu[l]), ssm_b_glu[l].reshape(1, -1)
        flat = lambda a: a.reshape(-1, SSM_WIDTH)
        ysl = _s5_out_call(flat(ul_tb), flat(yfl), flat(ybl), d_skip, wglu, bglu, S5_ROWS_OUT)
        ysl = jnp.transpose(ysl.reshape(T, B, SSM_WIDTH), (1, 0, 2))

        s0 = jnp.zeros((B, 2, GLA_WIDTH, GLA_KW), F32)
        ofc, obc, sc_fin = _gla_call(gqc, gkc, gvc, lgfc, lgbc, s0, min(GLA_BLOCK, C))
        ofl, obl, _ = _gla_call(gql, gkl, gvl, lgfl, lgbl, sc_fin, tg)

        x = _merge_call(x, oa, ysl, ofl, obl, sh1, sc1, g1, None, n1, w, tm_merge)
        x = _ffn_call(x, sh2, sc2, g2, None, n2, wup_g, wup_u, wdown, fn, l == depth - 1, tm_ffn)
        if with_ctx:
            ysc = _s5_out_call(flat(uc_tb), flat(yfc), flat(ybc), d_skip, wglu, bglu, S5_ROWS_OUT)
            ysc = jnp.transpose(ysc.reshape(C, B, SSM_WIDTH), (1, 0, 2))
            xc = _merge_call(xc, oac, ysc, ofc, obc, sh1, sc1, g1, B, n1, w, C)
            xc = _ffn_call(xc, sh2, sc2, g2, B, n2, wup_g, wup_u, wdown, fn, False, C)
    return x
```

```python
import functools
import math

import jax
import jax.numpy as jnp
from jax import lax
from jax.experimental import pallas as pl
from jax.experimental.pallas import tpu as pltpu

F32 = jnp.float32
BF16 = jnp.bfloat16

D_MODEL = 1024
GRID_W = 64
EPS = 1e-6
MLA_HEADS = 8
MLA_RANK = 256
MLA_NOPE = 64
MLA_ROPE = 32
MLA_V = 64
MLA_WIDTH = MLA_HEADS * MLA_V
MLA_SCALE = (MLA_NOPE + MLA_ROPE) ** -0.5
ROPE_FREQS = MLA_ROPE // 4
ROPE_BASE = 10000.0
LOG2_E = math.log2(math.e)
HEAD_SLOT = 128
SSM_WIDTH = 512
SSM_GROUP = 16
SSM_GROUPS = SSM_WIDTH // SSM_GROUP
SSM_STATE = 64
SSM_N = SSM_GROUPS * SSM_STATE
GLA_HEADS = 4
GLA_DK = 64
GLA_DV = 128
GLA_KW = GLA_HEADS * GLA_DK
GLA_WIDTH = GLA_HEADS * GLA_DV
GLA_GATE_RANK = 16
GLA_GATE_TAU = 16.0
GLA_CHUNK = 64
FFN_HIDDEN = 2816
FFN_SPLIT = 1536
LR_PAD = 128

VMEM_LIMIT = 56 * 1024 * 1024

TM_IN = 512
TM_MERGE = 512
TM_FFN = 512
TQ = 512
TK = 1024
ATT_STRIP = 64
S5_TL = 64
S5_PARTS = 2
S5_ROWS_OUT = 2048
GLA_BLOCK = 256


def _const_spec(shape):
    nd = len(shape)
    return pl.BlockSpec(shape, lambda *_: (0,) * nd, pipeline_mode=pl.Buffered(1))


def _params(sem):
    return pltpu.CompilerParams(dimension_semantics=sem, vmem_limit_bytes=VMEM_LIMIT)


def _rms(x):
    return x * lax.rsqrt(jnp.mean(x * x, axis=-1, keepdims=True) + EPS)


def _dot(a, b):
    return jnp.dot(a, b, preferred_element_type=F32)


def _dot_nt(a, b):
    return lax.dot_general(a, b, (((1,), (1,)), ((), ())), preferred_element_type=F32)


def _dot_tn(a, b):
    return lax.dot_general(a, b, (((0,), (0,)), ((), ())), preferred_element_type=F32)


def _mod_kernel(c_ref, w_ref, b_ref, o_ref):
    c = c_ref[...]
    s = (c * jax.nn.sigmoid(c)).astype(BF16)
    o_ref[0] = _dot(s, w_ref[0]) + b_ref[0]


def _mod_call(cvec, w_mod, b_mod):
    L, D, N = w_mod.shape
    R = cvec.shape[0]
    tn = 1536
    return pl.pallas_call(
        _mod_kernel,
        out_shape=jax.ShapeDtypeStruct((L, R, N), F32),
        grid=(L, N // tn),
        in_specs=[pl.BlockSpec((R, D), lambda l, j: (0, 0)),
                  pl.BlockSpec((1, D, tn), lambda l, j: (l, 0, j)),
                  pl.BlockSpec((1, 1, tn), lambda l, j: (l, 0, j))],
        out_specs=pl.BlockSpec((1, R, tn), lambda l, j: (l, 0, j)),
        compiler_params=_params(("arbitrary", "arbitrary")),
        name="mod",
    )(cvec, w_mod, b_mod.reshape(L, 1, N))


def _in_proj_kernel(x_ref, sh_ref, sc_ref, n1_ref, cos_ref, sin_ref,
                    wa_ref, wkr_ref, wu_ref, wg_ref, wlr_ref,
                    qn_ref, kvn_ref, wuq_ref, wuqs_ref, wukv_ref, wa2_ref, ba2_ref,
                    q_ref, k_ref, kv_ref, u_ref, gq_ref, gk_ref, gv_ref, lgf_ref, lgb_ref):
    x = x_ref[0]
    h = _rms(x) * n1_ref[...]
    h = h * (1.0 + sc_ref[0]) + sh_ref[0]
    hb = h.astype(BF16)
    cos = cos_ref[...]
    sin = sin_ref[...]
    cos8 = jnp.concatenate([cos] * MLA_HEADS, axis=1)
    sin8 = jnp.concatenate([sin] * MLA_HEADS, axis=1)

    za = _dot(hb, wa_ref[...])
    cqn = (_rms(za[:, :MLA_RANK]) * qn_ref[...]).astype(BF16)
    q = _dot(cqn, wuq_ref[...]) * cos8 + _dot(cqn, wuqs_ref[...]) * sin8
    q_ref[0] = (q * (MLA_SCALE * LOG2_E)).astype(BF16)

    ckvn = (_rms(za[:, MLA_RANK:]) * kvn_ref[...]).astype(BF16)
    kv = _dot(ckvn, wukv_ref[...])
    kr2 = _dot(hb, wkr_ref[...])
    krr = kr2[:, :HEAD_SLOT] * cos + kr2[:, HEAD_SLOT:] * sin
    nope = lax.broadcasted_iota(jnp.int32, krr.shape, 1) < MLA_NOPE
    for hd in range(MLA_HEADS):
        sl = slice(hd * HEAD_SLOT, (hd + 1) * HEAD_SLOT)
        k_ref[0, :, sl] = jnp.where(nope, kv[:, sl], krr).astype(BF16)
        kv_ref[0, :, sl] = jnp.where(nope, 1.0, kv[:, sl]).astype(BF16)

    u_ref[0] = _dot(hb, wu_ref[...])
    zg = _dot(hb, wg_ref[...])
    gq_ref[0] = zg[:, :GLA_KW]
    gk_ref[0] = zg[:, GLA_KW:2 * GLA_KW]
    gv_ref[0] = zg[:, 2 * GLA_KW:]
    zlr = _dot(hb, wlr_ref[...]).astype(BF16)
    for d, o_ref in enumerate((lgf_ref, lgb_ref)):
        z = _dot(zlr, wa2_ref[d]) + ba2_ref[d]
        o_ref[0] = (jnp.minimum(z, 0.0) - jnp.log1p(jnp.exp(-jnp.abs(z)))) * (1.0 / GLA_GATE_TAU)


def _in_proj_call(x, mod_sh, mod_sc, mod_row, n1, cos_t, sin_t, w, tm):
    Bn, Tn, D = x.shape
    nt = Tn // tm
    row = (lambda b, i: (b, 0, 0)) if mod_row is None else (lambda b, i: (mod_row, 0, 0))
    tok = lambda width: pl.BlockSpec((1, tm, width), lambda b, i: (b, i, 0))
    weights = [w["wa"], w["wkr"], w["wu"], w["wg"], w["wlr"], w["qn"], w["kvn"],
               w["wuq"], w["wuqs"], w["wukv"], w["wa2"], w["ba2"]]
    outs = [(D, BF16), (D, BF16), (D, BF16), (SSM_WIDTH, F32), (GLA_KW, F32), (GLA_KW, F32),
            (GLA_WIDTH, F32), (GLA_KW, F32), (GLA_KW, F32)]
    out_shape = [jax.ShapeDtypeStruct((Bn, Tn, wd), dt) for wd, dt in outs]
    out_specs = [tok(wd) for wd, _ in outs]
    return pl.pallas_call(
        _in_proj_kernel,
        out_shape=out_shape,
        grid=(Bn, nt),
        in_specs=[tok(D),
                  pl.BlockSpec((1, 1, D), row), pl.BlockSpec((1, 1, D), row),
                  _const_spec((1, D)),
                  pl.BlockSpec((tm, HEAD_SLOT), lambda b, i: (i, 0)),
                  pl.BlockSpec((tm, HEAD_SLOT), lambda b, i: (i, 0))]
                 + [_const_spec(a.shape) for a in weights],
        out_specs=out_specs,
        compiler_params=_params(("parallel", "parallel")),
        name="in_proj",
    )(x, mod_sh, mod_sc, n1, cos_t, sin_t, *weights)


def _lane_fold(x, op):
    out = x[:, :HEAD_SLOT]
    for j in range(1, x.shape[1] // HEAD_SLOT):
        out = op(out, x[:, j * HEAD_SLOT:(j + 1) * HEAD_SLOT])
    return out


def _attn_update(s, kvt, carry):
    m, acc = carry
    m_new = jnp.maximum(m, jnp.max(_lane_fold(s, jnp.maximum), axis=-1, keepdims=True))
    p = jnp.exp2(s - m_new)
    acc = jnp.exp2(m - m_new) * acc + _dot(p.astype(BF16), kvt)
    return m_new, acc


def _attn_kernel(*refs, n_lat_tiles, tk, tq):
    if n_lat_tiles:
        q_ref, kc_ref, kvc_ref, kl_ref, kvl_ref, o_ref, s_ref, mx_ref, p_ref, m_ref, acc_ref = refs
    else:
        q_ref, kc_ref, kvc_ref, o_ref = refs
    sls = [slice(hh * HEAD_SLOT, (hh + 1) * HEAD_SLOT) for hh in range(2)]

    def q_tile(qi, c):
        qrows = pl.ds(pl.multiple_of(qi * tq, tq), tq)
        qs = [q_ref[0, qrows, sl] for sl in sls]
        init = (jnp.full((tq, 1), -jnp.inf, F32), jnp.zeros((tq, HEAD_SLOT), F32))
        carry = tuple(_attn_update(_dot_nt(qs[hh], kc_ref[0, :, sls[hh]]), kvc_ref[0, :, sls[hh]], init)
                      for hh in range(2))
        if n_lat_tiles:
            for hh in range(2):
                m, acc = carry[hh]
                m_ref[hh] = jnp.broadcast_to(m, (tq, HEAD_SLOT))
                acc_ref[hh] = acc
            rows = lambda i: pl.ds(pl.multiple_of(i * tk, tk), tk)

            def scores(i, slot):
                for hh in range(2):
                    s = _dot_nt(qs[hh], kl_ref[0, rows(i), sls[hh]])
                    s_ref[slot, hh] = s
                    mx_ref[slot, hh] = _lane_fold(s, jnp.maximum)

            def update(i, slot):
                for hh in range(2):
                    for r in range(0, tq, ATT_STRIP):
                        rs = slice(r, r + ATT_STRIP)
                        m_old = m_ref[hh, rs, :]
                        m_new = jnp.maximum(m_old, jnp.max(mx_ref[slot, hh, rs, :], axis=-1, keepdims=True))
                        p = jnp.exp2(s_ref[slot, hh, rs, :] - jnp.concatenate([m_new] * (tk // HEAD_SLOT), axis=1))
                        m_ref[hh, rs, :] = m_new
                        acc_ref[hh, rs, :] = jnp.exp2(m_old - m_new) * acc_ref[hh, rs, :]
                        p_ref[hh, rs, :] = p.astype(BF16)
                    acc_ref[hh] += _dot(p_ref[hh], kvl_ref[0, rows(i), sls[hh]])

            scores(0, 0)

            def body(j, c2):
                i = 2 * j
                scores(i + 1, 1)
                update(i, 0)
                scores(jnp.minimum(i + 2, n_lat_tiles - 1), 0)
                update(i + 1, 1)
                return c2

            lax.fori_loop(0, n_lat_tiles // 2, body, 0)
            carry = tuple((None, acc_ref[hh]) for hh in range(2))
        outs = [(acc * pltpu.roll(1.0 / acc, MLA_NOPE, 1))[:, MLA_NOPE:] for _, acc in carry]
        o_ref[0, qrows, :] = jnp.concatenate(outs, axis=1).astype(o_ref.dtype)
        return c

    lax.fori_loop(0, q_ref.shape[1] // tq, q_tile, 0)


def _attn_call(q, k_ctx, kv_ctx, k_lat, kv_lat, tq, tk):
    Bn, Tq, D = q.shape
    C = k_ctx.shape[1]
    hp = MLA_HEADS // 2
    w2 = 2 * HEAD_SLOT
    n_lat_tiles = 0 if k_lat is None else k_lat.shape[1] // tk
    in_specs = [pl.BlockSpec((1, Tq, w2), lambda b, h: (b, 0, h)),
                pl.BlockSpec((1, C, w2), lambda b, h: (b, 0, h)),
                pl.BlockSpec((1, C, w2), lambda b, h: (b, 0, h))]
    args = [q, k_ctx, kv_ctx]
    scratch = []
    if n_lat_tiles:
        assert n_lat_tiles % 2 == 0 and tq % ATT_STRIP == 0
        Tk = k_lat.shape[1]
        in_specs += [pl.BlockSpec((1, Tk, w2), lambda b, h: (b, 0, h)),
                     pl.BlockSpec((1, Tk, w2), lambda b, h: (b, 0, h))]
        args += [k_lat, kv_lat]
        stat = pltpu.VMEM((2, tq, HEAD_SLOT), F32)
        scratch = [pltpu.VMEM((2, 2, tq, tk), F32), pltpu.VMEM((2, 2, tq, HEAD_SLOT), F32),
                   pltpu.VMEM((2, tq, tk), BF16), stat, stat]
    return pl.pallas_call(
        functools.partial(_attn_kernel, n_lat_tiles=n_lat_tiles, tk=tk, tq=tq),
        out_shape=jax.ShapeDtypeStruct((Bn, Tq, MLA_WIDTH), BF16),
        grid=(Bn, hp),
        in_specs=in_specs,
        out_specs=pl.BlockSpec((1, Tq, 2 * MLA_V), lambda b, h: (b, 0, h)),
        scratch_shapes=scratch,
        compiler_params=_params(("parallel", "parallel")),
        name="attn",
    )(*args)


def _s5_disc_kernel(lr_ref, li_ref, ldt_ref, br_ref, bi_ref, ar_ref, ai_ref, bbr_ref, bbi_ref):
    lr = lr_ref[0]
    li = li_ref[0]
    dt = jnp.exp(ldt_ref[0])
    mag = jnp.exp(dt * lr)
    ar = mag * jnp.cos(dt * li)
    ai = mag * jnp.sin(dt * li)
    den = lr * lr + li * li
    fr = ((ar - 1.0) * lr + ai * li) / den
    fi = (ai * lr - (ar - 1.0) * li) / den
    ar_ref[0] = ar
    ai_ref[0] = ai
    br = br_ref[...]
    bi = bi_ref[...]
    bbr_ref[0] = fr * br - fi * bi
    bbi_ref[0] = fr * bi + fi * br


def _s5_disc_call(lam_re, lam_im, log_dt, b_re, b_im):
    n = SSM_N
    lr = lam_re.reshape(2, 1, n)
    li = lam_im.reshape(2, 1, n)
    ldt = jnp.repeat(log_dt, SSM_STATE, axis=1).reshape(2, 1, n)
    br = b_re.reshape(n, SSM_GROUP).T
    bi = b_im.reshape(n, SSM_GROUP).T
    vec = pl.BlockSpec((1, 1, n), lambda d: (d, 0, 0))
    mat = pl.BlockSpec((SSM_GROUP, n), lambda d: (0, 0))
    omat = pl.BlockSpec((1, SSM_GROUP, n), lambda d: (d, 0, 0))
    return pl.pallas_call(
        _s5_disc_kernel,
        out_shape=[jax.ShapeDtypeStruct((2, 1, n), F32), jax.ShapeDtypeStruct((2, 1, n), F32),
                   jax.ShapeDtypeStruct((2, SSM_GROUP, n), F32), jax.ShapeDtypeStruct((2, SSM_GROUP, n), F32)],
        grid=(2,),
        in_specs=[vec, vec, vec, mat, mat],
        out_specs=[vec, vec, omat, omat],
        compiler_params=_params(("arbitrary",)),
        name="s5_disc",
    )(lr, li, ldt, br, bi)


def _s5_scan_kernel(uf_ref, ub_ref, bd_ref, ar_ref, ai_ref, cre_ref, cim_ref, h0_ref,
                    yf_ref, yb_ref, hfin_ref, xs_ref, h_ref, *, tl, nb):
    i = pl.program_id(0)
    n = SSM_N

    @pl.when(i == 0)
    def _():
        h_ref[...] = h0_ref[...]

    half = n // 2
    hw = SSM_WIDTH // 2
    tp = tl // S5_PARTS

    def part_start(d, part):
        return part * tp if d == 0 else tl - (part + 1) * tp

    for part in range(S5_PARTS):
        for d, u_ref in enumerate((uf_ref, ub_ref)):
            t0 = part_start(d, part)
            rows = slice(t0 * nb, (t0 + tp) * nb)
            u = u_ref[t0:t0 + tp].reshape(tp * nb, SSM_WIDTH).astype(BF16)
            for k in range(2):
                xk = _dot(u[:, k * hw:(k + 1) * hw], bd_ref[d, k])
                xs_ref[d, rows, k * half:(k + 1) * half] = xk[:, :half]
                xs_ref[d, rows, n + k * half:n + (k + 1) * half] = xk[:, half:]

    state = {}
    for d in range(2):
        for cb in range(2):
            re = slice(cb * half, (cb + 1) * half)
            im = slice(n + cb * half, n + (cb + 1) * half)
            state[d, cb] = (h_ref[d, :, re], h_ref[d, :, im], jnp.broadcast_to(ar_ref[d, :, re], (nb, half)),
                            jnp.broadcast_to(ai_ref[d, :, re], (nb, half)), re, im)
    for t in range(tl):
        for d in range(2):
            tt = t if d == 0 else tl - 1 - t
            rows = slice(tt * nb, (tt + 1) * nb)
            for cb in range(2):
                hr, hi, ar, ai, re, im = state[d, cb]
                nr = ar * hr - ai * hi + xs_ref[d, rows, re]
                ni = ar * hi + ai * hr + xs_ref[d, rows, im]
                xs_ref[d, rows, re] = nr
                xs_ref[d, rows, im] = ni
                state[d, cb] = (nr, ni, ar, ai, re, im)
    for (d, cb), (hr, hi, _, _, re, im) in state.items():
        h_ref[d, :, re] = hr
        h_ref[d, :, im] = hi

    for part in range(S5_PARTS):
        for d, y_ref in enumerate((yf_ref, yb_ref)):
            t0 = part_start(d, part)
            rows = slice(t0 * nb, (t0 + tp) * nb)
            ys = []
            for k in range(2):
                hr = xs_ref[d, rows, k * half:(k + 1) * half].astype(BF16)
                hi = xs_ref[d, rows, n + k * half:n + (k + 1) * half].astype(BF16)
                ys.append(_dot(hr, cre_ref[k]) - _dot(hi, cim_ref[k]))
            y_ref[t0:t0 + tp] = jnp.concatenate(ys, axis=1).reshape(tp, nb, SSM_WIDTH)

    @pl.when(i == pl.num_programs(0) - 1)
    def _():
        hfin_ref[...] = h_ref[...]


def _s5_scan_call(u_tb, bd, ar, ai, cre_t, cim_t, h0, tl):
    Tn, nb, W = u_tb.shape
    nc = Tn // tl
    n = SSM_N
    fwd = pl.BlockSpec((tl, nb, W), lambda i: (i, 0, 0))
    bwd = pl.BlockSpec((tl, nb, W), lambda i: (nc - 1 - i, 0, 0))
    return pl.pallas_call(
        functools.partial(_s5_scan_kernel, tl=tl, nb=nb),
        out_shape=[jax.ShapeDtypeStruct((Tn, nb, W), F32), jax.ShapeDtypeStruct((Tn, nb, W), F32),
                   jax.ShapeDtypeStruct((2, nb, 2 * n), F32)],
        grid=(nc,),
        in_specs=[fwd, bwd, _const_spec(bd.shape), _const_spec(ar.shape), _const_spec(ai.shape),
                  _const_spec(cre_t.shape), _const_spec(cim_t.shape), _const_spec(h0.shape)],
        out_specs=[fwd, bwd, pl.BlockSpec((2, nb, 2 * n), lambda i: (0, 0, 0))],
        scratch_shapes=[pltpu.VMEM((2, tl * nb, 2 * n), F32), pltpu.VMEM((2, nb, 2 * n), F32)],
        compiler_params=_params(("arbitrary",)),
        name="s5_scan",
    )(u_tb, u_tb, bd, ar, ai, cre_t, cim_t, h0)


def _s5_out_kernel(u_ref, yf_ref, yb_ref, d_ref, wglu_ref, bglu_ref, o_ref):
    y = d_ref[...] * u_ref[...] + yf_ref[...] + yb_ref[...]
    g = jax.nn.gelu(y)
    z = _dot(g.astype(BF16), wglu_ref[...]) + bglu_ref[...]
    o_ref[...] = (g * jax.nn.sigmoid(z)).astype(o_ref.dtype)


def _s5_out_call(u, yf, yb, d_skip, wglu, bglu, tr):
    R, W = u.shape
    tr = min(tr, R)
    blk = pl.BlockSpec((tr, W), lambda i: (i, 0))
    return pl.pallas_call(
        _s5_out_kernel,
        out_shape=jax.ShapeDtypeStruct((R, W), BF16),
        grid=(R // tr,),
        in_specs=[blk, blk, blk, _const_spec(d_skip.shape), _const_spec(wglu.shape), _const_spec(bglu.shape)],
        out_specs=blk,
        compiler_params=_params(("parallel",)),
        name="s5_out",
    )(u, yf, yb, d_skip, wglu, bglu)


def _split3(x):
    hi = x.astype(BF16)
    r1 = x - hi.astype(F32)
    mid = r1.astype(BF16)
    lo = (r1 - mid.astype(F32)).astype(BF16)
    return hi, mid, lo


def _gla_block(q, k, v, lg, s, d, cst, n_chunks):
    tri_ref, mask_ref, same_ref, hmask_ref, bmask_ref = cst
    L = GLA_CHUNK
    parts = _split3(lg)
    b = sum(_dot(tri_ref[d], p) for p in parts)
    tot = sum(_dot(same_ref[...], p) for p in parts)
    q_in = q * (GLA_DK ** -0.5) * jnp.exp(b)
    k_in = (k * jnp.exp(-b)).astype(BF16)
    k_dec = (k * jnp.exp(tot - b)).astype(BF16)
    vb = v.astype(BF16)
    mask = mask_ref[d] > 0.0
    outs = []
    for hd in range(GLA_HEADS):
        qh = (q_in * hmask_ref[hd]).astype(BF16)
        att = jnp.where(mask, _dot_nt(qh, k_in), 0.0).astype(BF16)
        outs.append(_dot(att, vb[:, hd * GLA_DV:(hd + 1) * GLA_DV]))
    o_intra = jnp.concatenate(outs, axis=1)
    q_in_b = q_in.astype(BF16)
    bmask = bmask_ref[...]
    o_rows = [None] * n_chunks
    for c in (range(n_chunks) if d == 0 else reversed(range(n_chunks))):
        rows = slice(c * L, (c + 1) * L)
        o_rows[c] = _dot_nt(q_in_b[rows], s.astype(BF16))
        dec = jnp.exp(tot[c * L:c * L + 1, :])
        s = dec * s + _dot_tn(vb[rows], k_dec[rows]) * bmask
    return o_intra + jnp.concatenate(o_rows, axis=0), s


def _gla_kernel(qf_ref, kf_ref, vf_ref, lgf_ref, qb_ref, kb_ref, vb_ref, lgb_ref, s0_ref,
                tri_ref, mask_ref, same_ref, hmask_ref, bmask_ref,
                of_ref, ob_ref, sfin_ref, s_ref, *, n_chunks):
    i = pl.program_id(1)
    cst = (tri_ref, mask_ref, same_ref, hmask_ref, bmask_ref)

    @pl.when(i == 0)
    def _():
        s_ref[...] = s0_ref[0]

    o, s = _gla_block(qf_ref[0], kf_ref[0], vf_ref[0], lgf_ref[0], s_ref[0], 0, cst, n_chunks)
    of_ref[0] = o
    s_ref[0] = s
    o, s = _gla_block(qb_ref[0], kb_ref[0], vb_ref[0], lgb_ref[0], s_ref[1], 1, cst, n_chunks)
    ob_ref[0] = o
    s_ref[1] = s

    @pl.when(i == pl.num_programs(1) - 1)
    def _():
        sfin_ref[0] = s_ref[...]


def _gla_consts(tg):
    t = jnp.arange(tg)
    same = (t[:, None] // GLA_CHUNK) == (t[None, :] // GLA_CHUNK)
    low = same & (t[None, :] <= t[:, None])
    upp = same & (t[None, :] >= t[:, None])
    tri = jnp.stack([low, upp])
    hmask = (jnp.arange(GLA_KW)[None, None, :] // GLA_DK) == jnp.arange(GLA_HEADS)[:, None, None]
    bmask = (jnp.arange(GLA_WIDTH)[:, None] // GLA_DV) == (jnp.arange(GLA_KW)[None, :] // GLA_DK)
    return tri.astype(BF16), tri.astype(F32), same.astype(BF16), hmask.astype(F32), bmask.astype(F32)


def _gla_call(q, k, v, lgf, lgb, s0, tg):
    Bn, Tn, _ = q.shape
    nblk = Tn // tg
    consts = _gla_consts(tg)
    fwd = lambda w: pl.BlockSpec((1, tg, w), lambda b, i: (b, i, 0))
    bwd = lambda w: pl.BlockSpec((1, tg, w), lambda b, i: (b, nblk - 1 - i, 0))
    st = pl.BlockSpec((1, 2, GLA_WIDTH, GLA_KW), lambda b, i: (b, 0, 0, 0))
    return pl.pallas_call(
        functools.partial(_gla_kernel, n_chunks=tg // GLA_CHUNK),
        out_shape=[jax.ShapeDtypeStruct((Bn, Tn, GLA_WIDTH), F32), jax.ShapeDtypeStruct((Bn, Tn, GLA_WIDTH), F32),
                   jax.ShapeDtypeStruct((Bn, 2, GLA_WIDTH, GLA_KW), F32)],
        grid=(Bn, nblk),
        in_specs=[fwd(GLA_KW), fwd(GLA_KW), fwd(GLA_WIDTH), fwd(GLA_KW),
                  bwd(GLA_KW), bwd(GLA_KW), bwd(GLA_WIDTH), bwd(GLA_KW), st]
                 + [_const_spec(a.shape) for a in consts],
        out_specs=[fwd(GLA_WIDTH), bwd(GLA_WIDTH), st],
        scratch_shapes=[pltpu.VMEM((2, GLA_WIDTH, GLA_KW), F32)],
        compiler_params=_params(("parallel", "arbitrary")),
        name="gla",
    )(q, k, v, lgf, q, k, v, lgb, s0, *consts)


def _merge_kernel(x_ref, oa_ref, ys_ref, of_ref, ob_ref, sh_ref, sc_ref, g1_ref, n1_ref,
                  wgate_ref, wr_ref, wmo_ref, wso_ref, gn_ref, wgo_ref, wout_ref, o_ref):
    x = x_ref[0]
    h = _rms(x) * n1_ref[...]
    hb = (h * (1.0 + sc_ref[0]) + sh_ref[0]).astype(BF16)
    ya = _dot(oa_ref[0], wmo_ref[...])
    yb = _dot(ys_ref[0], wso_ref[...])
    o = of_ref[0] + ob_ref[0]
    on = jnp.concatenate([_rms(o[:, hd * GLA_DV:(hd + 1) * GLA_DV]) for hd in range(GLA_HEADS)], axis=1)
    r = _dot(hb, wr_ref[...])
    yg = _dot(((on * gn_ref[...]) * (r * jax.nn.sigmoid(r))).astype(BF16), wgo_ref[...])
    m = jnp.zeros_like(ya)
    for j, br in enumerate((ya, yb, yg)):
        gate = jax.nn.sigmoid(_dot(hb, wgate_ref[:, j * D_MODEL:(j + 1) * D_MODEL]))
        m = m + gate * br
    o_ref[0] = x + g1_ref[0] * _dot(m.astype(BF16), wout_ref[...])


def _merge_call(x, oa, ys, of, ob, mod_sh, mod_sc, mod_g, mod_row, n1, w, tm):
    Bn, Tn, D = x.shape
    row = (lambda b, i: (b, 0, 0)) if mod_row is None else (lambda b, i: (mod_row, 0, 0))
    tok = lambda width: pl.BlockSpec((1, tm, width), lambda b, i: (b, i, 0))
    mrow = pl.BlockSpec((1, 1, D), row)
    weights = [w["wgate"], w["wr"], w["wmo"], w["wso"], w["gn"], w["wgo"], w["wout"]]
    return pl.pallas_call(
        _merge_kernel,
        out_shape=jax.ShapeDtypeStruct((Bn, Tn, D), F32),
        grid=(Bn, Tn // tm),
        in_specs=[tok(D), tok(MLA_WIDTH), tok(SSM_WIDTH), tok(GLA_WIDTH), tok(GLA_WIDTH),
                  mrow, mrow, mrow, _const_spec((1, D))] + [_const_spec(a.shape) for a in weights],
        out_specs=tok(D),
        compiler_params=_params(("parallel", "parallel")),
        name="merge",
    )(x, oa, ys, of, ob, mod_sh, mod_sc, mod_g, n1, *weights)


def _ffn_kernel(x_ref, sh_ref, sc_ref, g2_ref, n2_ref, wg_ref, wu_ref, wd_ref, fn_ref, o_ref, *, final):
    x = x_ref[0]
    h = _rms(x) * n2_ref[...]
    hb = (h * (1.0 + sc_ref[0]) + sh_ref[0]).astype(BF16)
    acc = jnp.zeros_like(x)
    for cs in (slice(0, FFN_SPLIT), slice(FFN_SPLIT, FFN_HIDDEN)):
        g = _dot(hb, wg_ref[:, cs])
        u = _dot(hb, wu_ref[:, cs])
        acc = acc + _dot(((g * jax.nn.sigmoid(g)) * u).astype(BF16), wd_ref[cs, :])
    y = x + g2_ref[0] * acc
    if final:
        y = _rms(y) * fn_ref[...]
    o_ref[0] = y


def _ffn_call(x, mod_sh, mod_sc, mod_g, mod_row, n2, wg, wu, wd, fn, final, tm):
    Bn, Tn, D = x.shape
    row = (lambda b, i: (b, 0, 0)) if mod_row is None else (lambda b, i: (mod_row, 0, 0))
    tok = pl.BlockSpec((1, tm, D), lambda b, i: (b, i, 0))
    mrow = pl.BlockSpec((1, 1, D), row)
    return pl.pallas_call(
        functools.partial(_ffn_kernel, final=final),
        out_shape=jax.ShapeDtypeStruct((Bn, Tn, D), F32),
        grid=(Bn, Tn // tm),
        in_specs=[tok, mrow, mrow, mrow, _const_spec((1, D)), _const_spec(wg.shape), _const_spec(wu.shape),
                  _const_spec(wd.shape), _const_spec((1, D))],
        out_specs=tok,
        compiler_params=_params(("parallel", "parallel")),
        name="ffn",
    )(x, mod_sh, mod_sc, mod_g, n2, wg, wu, wd, fn)


def _pair_swap(w):
    w2 = w.reshape(w.shape[0], -1, 2)
    return jnp.stack([-w2[..., 1], w2[..., 0]], axis=-1).reshape(w.shape)


def _layer_weights(l, w_in, mla_q_norm, mla_w_uq, mla_kv_norm, mla_w_ukv, gla_w_a2, gla_b_a2):
    wi = w_in[l]
    D = wi.shape[0]
    o = 0
    cuts = {}
    for name, sz in (("cq", MLA_RANK), ("ckv", MLA_RANK), ("kr", MLA_ROPE), ("u", SSM_WIDTH), ("gq", GLA_KW),
                     ("gk", GLA_KW), ("gv", GLA_WIDTH), ("gr", GLA_WIDTH), ("af", GLA_GATE_RANK),
                     ("ab", GLA_GATE_RANK), ("gate", 3 * D_MODEL)):
        cuts[name] = wi[:, o:o + sz]
        o += sz
    zeros = lambda n: jnp.zeros((D, n), F32)
    kr = cuts["kr"]
    pad_k = HEAD_SLOT - MLA_NOPE - MLA_ROPE
    wkr = jnp.concatenate([zeros(MLA_NOPE), kr, zeros(pad_k), zeros(MLA_NOPE), _pair_swap(kr), zeros(pad_k)], axis=1)
    wlr = jnp.concatenate([cuts["af"], cuts["ab"], zeros(LR_PAD - 2 * GLA_GATE_RANK)], axis=1)
    uq = mla_w_uq[l].reshape(MLA_RANK, MLA_HEADS, MLA_NOPE + MLA_ROPE)
    zq = jnp.zeros((MLA_RANK, MLA_HEADS, pad_k), F32)
    wuq = jnp.concatenate([uq, zq], axis=2).reshape(MLA_RANK, MLA_HEADS * HEAD_SLOT)
    uq_sw = _pair_swap(uq[..., MLA_NOPE:].reshape(MLA_RANK, -1)).reshape(MLA_RANK, MLA_HEADS, MLA_ROPE)
    wuqs = jnp.concatenate([jnp.zeros((MLA_RANK, MLA_HEADS, MLA_NOPE), F32), uq_sw, zq], axis=2)
    wuqs = wuqs.reshape(MLA_RANK, MLA_HEADS * HEAD_SLOT)
    wa2 = jnp.zeros((2, LR_PAD, GLA_KW), F32)
    wa2 = wa2.at[0, :GLA_GATE_RANK].set(gla_w_a2[l, 0]).at[1, GLA_GATE_RANK:2 * GLA_GATE_RANK].set(gla_w_a2[l, 1])
    b16 = lambda a: a.astype(BF16)
    return dict(
        wa=b16(jnp.concatenate([cuts["cq"], cuts["ckv"]], axis=1)), wkr=b16(wkr), wu=b16(cuts["u"]),
        wg=b16(jnp.concatenate([cuts["gq"], cuts["gk"], cuts["gv"]], axis=1)), wlr=b16(wlr),
        qn=mla_q_norm[l].reshape(1, -1), kvn=mla_kv_norm[l].reshape(1, -1),
        wuq=b16(wuq), wuqs=b16(wuqs), wukv=b16(mla_w_ukv[l]), wa2=b16(wa2),
        ba2=gla_b_a2[l].reshape(2, 1, GLA_KW), wr=b16(cuts["gr"]), wgate=b16(cuts["gate"]))


def _rope_tables(T):
    t = jnp.arange(T, dtype=jnp.int32)
    rows = (t // GRID_W).astype(F32)
    cols = (t % GRID_W).astype(F32)
    inv = ROPE_BASE ** (-jnp.arange(ROPE_FREQS, dtype=F32) / ROPE_FREQS)
    ang = jnp.concatenate([rows[:, None] * inv, cols[:, None] * inv], axis=-1)
    ang2 = jnp.repeat(ang, 2, axis=1)
    pad_k = HEAD_SLOT - MLA_NOPE - MLA_ROPE
    cos = jnp.concatenate([jnp.ones((T, MLA_NOPE), F32), jnp.cos(ang2), jnp.ones((T, pad_k), F32)], axis=1)
    sin = jnp.concatenate([jnp.zeros((T, MLA_NOPE), F32), jnp.sin(ang2), jnp.zeros((T, pad_k), F32)], axis=1)
    return cos, sin


def _s5_matrices(bbr, bbi, c_re, c_im):
    n = SSM_N
    grp_row = jnp.arange(SSM_WIDTH)[:, None] // SSM_GROUP
    grp_col = jnp.arange(n)[None, :] // SSM_STATE
    blk = grp_row == grp_col
    tile = lambda m: jnp.where(blk, jnp.tile(m, (SSM_GROUPS, 1)), 0.0)
    hw, hn = SSM_WIDTH // 2, n // 2
    halves = lambda m: [m[k * hw:(k + 1) * hw, k * hn:(k + 1) * hn] for k in range(2)]
    bd = jnp.stack([jnp.stack([jnp.concatenate([r, i], axis=1)
                               for r, i in zip(halves(tile(bbr[d])), halves(tile(bbi[d])))])
                    for d in range(2)]).astype(BF16)

    def readout(c):
        ct = jnp.transpose(c, (0, 2, 1)).reshape(n, SSM_GROUP)
        full = jnp.where(blk.T, jnp.tile(ct, (1, SSM_GROUPS)), 0.0)
        return jnp.stack([full[k * hn:(k + 1) * hn, k * hw:(k + 1) * hw] for k in range(2)]).astype(BF16)

    return bd, readout(c_re), readout(c_im)


def kernel(x, c, ctx, c_ctx, w_mod, b_mod, norm1, norm2, w_in, mla_q_norm, mla_w_uq, mla_kv_norm, mla_w_ukv, mla_w_o, ssm_lam_re, ssm_lam_im, ssm_log_dt, ssm_b_re, ssm_b_im, ssm_c_re, ssm_c_im, ssm_d, ssm_w_glu, ssm_b_glu, ssm_w_o, gla_w_a2, gla_b_a2, gla_norm, gla_w_o, w_out, ffn_w_up, ffn_w_down, final_norm):
    B, T, D = x.shape
    C = ctx.shape[1]
    depth = w_mod.shape[0]
    n_rows = -(-(B + 1) // 8) * 8
    cvec = jnp.zeros((n_rows, D), F32).at[:B].set(c).at[B].set(c_ctx)
    mod = _mod_call(cvec, w_mod.astype(BF16), b_mod)
    cos_l, sin_l = _rope_tables(T)
    cos_c = jnp.ones((C, HEAD_SLOT), F32)
    sin_c = jnp.zeros((C, HEAD_SLOT), F32)
    tm_in, tm_merge, tm_ffn = min(TM_IN, T), min(TM_MERGE, T), min(TM_FFN, T)
    tq, tk = min(TQ, T), min(TK, T // 2)
    tg = min(GLA_BLOCK, T)
    b16 = lambda a: a.astype(BF16)

    xc = ctx
    for l in range(depth):
        with_ctx = l < depth - 1
        m6 = [mod[l, :, j * D:(j + 1) * D].reshape(n_rows, 1, D) for j in range(6)]
        sh1, sc1, g1, sh2, sc2, g2 = m6
        n1 = norm1[l].reshape(1, D)
        n2 = norm2[l].reshape(1, D)
        w = _layer_weights(l, w_in, mla_q_norm, mla_w_uq, mla_kv_norm, mla_w_ukv, gla_w_a2, gla_b_a2)
        w.update(wmo=b16(mla_w_o[l]), wso=b16(ssm_w_o[l]), gn=gla_norm[l].reshape(1, -1), wgo=b16(gla_w_o[l]),
                 wout=b16(w_out[l]))
        up = ffn_w_up[l]
        wup_g, wup_u, wdown = b16(up[:, :FFN_HIDDEN]), b16(up[:, FFN_HIDDEN:]), b16(ffn_w_down[l])
        fn = final_norm.reshape(1, D)

        qc, kc, kvc, uc, gqc, gkc, gvc, lgfc, lgbc = _in_proj_call(xc, sh1, sc1, B, n1, cos_c, sin_c, w, C)
        ql, kl, kvl, ul, gql, gkl, gvl, lgfl, lgbl = _in_proj_call(x, sh1, sc1, None, n1, cos_l, sin_l, w, tm_in)

        oa = _attn_call(ql, kc, kvc, kl, kvl, tq, tk)
        oac = _attn_call(qc, kc, kvc, None, None, C, tk) if with_ctx else None

        ar, ai, bbr, bbi = _s5_disc_call(ssm_lam_re[l], ssm_lam_im[l], ssm_log_dt[l], ssm_b_re[l], ssm_b_im[l])
        bd, cre_t, cim_t = _s5_matrices(bbr, bbi, ssm_c_re[l], ssm_c_im[l])
        uc_tb = jnp.transpose(uc, (1, 0, 2))
        ul_tb = jnp.transpose(ul, (1, 0, 2))
        h0 = jnp.zeros((2, B, 2 * SSM_N), F32)
        yfc, ybc, hc = _s5_scan_call(uc_tb, bd, ar, ai, cre_t, cim_t, h0, S5_TL)
        yfl, ybl, _ = _s5_scan_call(ul_tb, bd, ar, ai, cre_t, cim_t, hc, S5_TL)
        d_skip = ssm_d[l].reshape(1, -1)
        wglu, bglu = b16(ssm_w_glu[l]), ssm_b_glu[l].reshape(1, -1)
        flat = lambda a: a.reshape(-1, SSM_WIDTH)
        ysl = _s5_out_call(flat(ul_tb), flat(yfl), flat(ybl), d_skip, wglu, bglu, S5_ROWS_OUT)
        ysl = jnp.transpose(ysl.reshape(T, B, SSM_WIDTH), (1, 0, 2))

        s0 = jnp.zeros((B, 2, GLA_WIDTH, GLA_KW), F32)
        ofc, obc, sc_fin = _gla_call(gqc, gkc, gvc, lgfc, lgbc, s0, min(GLA_BLOCK, C))
        ofl, obl, _ = _gla_call(gql, gkl, gvl, lgfl, lgbl, sc_fin, tg)

        x = _merge_call(x, oa, ysl, ofl, obl, sh1, sc1, g1, None, n1, w, tm_merge)
        x = _ffn_call(x, sh2, sc2, g2, None, n2, wup_g, wup_u, wdown, fn, l == depth - 1, tm_ffn)
        if with_ctx:
            ysc = _s5_out_call(flat(uc_tb), flat(yfc), flat(ybc), d_skip, wglu, bglu, S5_ROWS_OUT)
            ysc = jnp.transpose(ysc.reshape(C, B, SSM_WIDTH), (1, 0, 2))
            xc = _merge_call(xc, oac, ysc, ofc, obc, sh1, sc1, g1, B, n1, w, C)
            xc = _ffn_call(xc, sh2, sc2, g2, B, n2, wup_g, wup_u, wdown, fn, False, C)
    return x
```

```python
import functools
import math

import jax
import jax.numpy as jnp
from jax import lax
from jax.experimental import pallas as pl
from jax.experimental.pallas import tpu as pltpu

F32 = jnp.float32
BF16 = jnp.bfloat16

D_MODEL = 1024
GRID_W = 64
EPS = 1e-6
MLA_HEADS = 8
MLA_RANK = 256
MLA_NOPE = 64
MLA_ROPE = 32
MLA_V = 64
MLA_WIDTH = MLA_HEADS * MLA_V
MLA_SCALE = (MLA_NOPE + MLA_ROPE) ** -0.5
ROPE_FREQS = MLA_ROPE // 4
ROPE_BASE = 10000.0
LOG2_E = math.log2(math.e)
HEAD_SLOT = 128
SSM_WIDTH = 512
SSM_GROUP = 16
SSM_GROUPS = SSM_WIDTH // SSM_GROUP
SSM_STATE = 64
SSM_N = SSM_GROUPS * SSM_STATE
GLA_HEADS = 4
GLA_DK = 64
GLA_DV = 128
GLA_KW = GLA_HEADS * GLA_DK
GLA_WIDTH = GLA_HEADS * GLA_DV
GLA_GATE_RANK = 16
GLA_GATE_TAU = 16.0
GLA_CHUNK = 64
FFN_HIDDEN = 2816
FFN_SPLIT = 1536
LR_PAD = 128

VMEM_LIMIT = 56 * 1024 * 1024

TM_IN = 512
TM_MERGE = 512
TM_FFN = 512
TQ = 512
TK = 1024
ATT_STRIP = 64
S5_TL = 64
S5_PARTS = 2
S5_ROWS_OUT = 2048
GLA_BLOCK = 256


def _const_spec(shape):
    nd = len(shape)
    return pl.BlockSpec(shape, lambda *_: (0,) * nd, pipeline_mode=pl.Buffered(1))


def _params(sem):
    return pltpu.CompilerParams(dimension_semantics=sem, vmem_limit_bytes=VMEM_LIMIT)


def _rms(x):
    return x * lax.rsqrt(jnp.mean(x * x, axis=-1, keepdims=True) + EPS)


def _dot(a, b):
    return jnp.dot(a, b, preferred_element_type=F32)


def _dot_nt(a, b):
    return lax.dot_general(a, b, (((1,), (1,)), ((), ())), preferred_element_type=F32)


def _dot_tn(a, b):
    return lax.dot_general(a, b, (((0,), (0,)), ((), ())), preferred_element_type=F32)


def _mod_kernel(c_ref, w_ref, b_ref, o_ref):
    c = c_ref[...]
    s = (c * jax.nn.sigmoid(c)).astype(BF16)
    o_ref[0] = _dot(s, w_ref[0]) + b_ref[0]


def _mod_call(cvec, w_mod, b_mod):
    L, D, N = w_mod.shape
    R = cvec.shape[0]
    tn = 1536
    return pl.pallas_call(
        _mod_kernel,
        out_shape=jax.ShapeDtypeStruct((L, R, N), F32),
        grid=(L, N // tn),
        in_specs=[pl.BlockSpec((R, D), lambda l, j: (0, 0)),
                  pl.BlockSpec((1, D, tn), lambda l, j: (l, 0, j)),
                  pl.BlockSpec((1, 1, tn), lambda l, j: (l, 0, j))],
        out_specs=pl.BlockSpec((1, R, tn), lambda l, j: (l, 0, j)),
        compiler_params=_params(("arbitrary", "arbitrary")),
        name="mod",
    )(cvec, w_mod, b_mod.reshape(L, 1, N))


def _in_proj_kernel(x_ref, sh_ref, sc_ref, n1_ref, cos_ref, sin_ref,
                    wa_ref, wkr_ref, wu_ref, wg_ref, wlr_ref,
                    qn_ref, kvn_ref, wuq_ref, wuqs_ref, wukv_ref, wa2_ref, ba2_ref,
                    q_ref, k_ref, kv_ref, u_ref, gq_ref, gk_ref, gv_ref, lgf_ref, lgb_ref):
    x = x_ref[0]
    h = _rms(x) * n1_ref[...]
    h = h * (1.0 + sc_ref[0]) + sh_ref[0]
    hb = h.astype(BF16)
    cos = cos_ref[...]
    sin = sin_ref[...]
    cos8 = jnp.concatenate([cos] * MLA_HEADS, axis=1)
    sin8 = jnp.concatenate([sin] * MLA_HEADS, axis=1)

    za = _dot(hb, wa_ref[...])
    cqn = (_rms(za[:, :MLA_RANK]) * qn_ref[...]).astype(BF16)
    q = _dot(cqn, wuq_ref[...]) * cos8 + _dot(cqn, wuqs_ref[...]) * sin8
    q_ref[0] = (q * (MLA_SCALE * LOG2_E)).astype(BF16)

    ckvn = (_rms(za[:, MLA_RANK:]) * kvn_ref[...]).astype(BF16)
    kv = _dot(ckvn, wukv_ref[...])
    kr2 = _dot(hb, wkr_ref[...])
    krr = kr2[:, :HEAD_SLOT] * cos + kr2[:, HEAD_SLOT:] * sin
    nope = lax.broadcasted_iota(jnp.int32, krr.shape, 1) < MLA_NOPE
    for hd in range(MLA_HEADS):
        sl = slice(hd * HEAD_SLOT, (hd + 1) * HEAD_SLOT)
        k_ref[0, :, sl] = jnp.where(nope, kv[:, sl], krr).astype(BF16)
        kv_ref[0, :, sl] = jnp.where(nope, 1.0, kv[:, sl]).astype(BF16)

    u_ref[0] = _dot(hb, wu_ref[...])
    zg = _dot(hb, wg_ref[...])
    gq_ref[0] = zg[:, :GLA_KW]
    gk_ref[0] = zg[:, GLA_KW:2 * GLA_KW]
    gv_ref[0] = zg[:, 2 * GLA_KW:]
    zlr = _dot(hb, wlr_ref[...]).astype(BF16)
    for d, o_ref in enumerate((lgf_ref, lgb_ref)):
        z = _dot(zlr, wa2_ref[d]) + ba2_ref[d]
        o_ref[0] = (jnp.minimum(z, 0.0) - jnp.log1p(jnp.exp(-jnp.abs(z)))) * (1.0 / GLA_GATE_TAU)


def _in_proj_call(x, mod_sh, mod_sc, mod_row, n1, cos_t, sin_t, w, tm):
    Bn, Tn, D = x.shape
    nt = Tn // tm
    row = (lambda b, i: (b, 0, 0)) if mod_row is None else (lambda b, i: (mod_row, 0, 0))
    tok = lambda width: pl.BlockSpec((1, tm, width), lambda b, i: (b, i, 0))
    weights = [w["wa"], w["wkr"], w["wu"], w["wg"], w["wlr"], w["qn"], w["kvn"],
               w["wuq"], w["wuqs"], w["wukv"], w["wa2"], w["ba2"]]
    outs = [(D, BF16), (D, BF16), (D, BF16), (SSM_WIDTH, F32), (GLA_KW, F32), (GLA_KW, F32),
            (GLA_WIDTH, F32), (GLA_KW, F32), (GLA_KW, F32)]
    out_shape = [jax.ShapeDtypeStruct((Bn, Tn, wd), dt) for wd, dt in outs]
    out_specs = [tok(wd) for wd, _ in outs]
    return pl.pallas_call(
        _in_proj_kernel,
        out_shape=out_shape,
        grid=(Bn, nt),
        in_specs=[tok(D),
                  pl.BlockSpec((1, 1, D), row), pl.BlockSpec((1, 1, D), row),
                  _const_spec((1, D)),
                  pl.BlockSpec((tm, HEAD_SLOT), lambda b, i: (i, 0)),
                  pl.BlockSpec((tm, HEAD_SLOT), lambda b, i: (i, 0))]
                 + [_const_spec(a.shape) for a in weights],
        out_specs=out_specs,
        compiler_params=_params(("parallel", "parallel")),
        name="in_proj",
    )(x, mod_sh, mod_sc, n1, cos_t, sin_t, *weights)


def _lane_fold(x, op):
    out = x[:, :HEAD_SLOT]
    for j in range(1, x.shape[1] // HEAD_SLOT):
        out = op(out, x[:, j * HEAD_SLOT:(j + 1) * HEAD_SLOT])
    return out


def _attn_update(s, kvt, carry):
    m, acc = carry
    m_new = jnp.maximum(m, jnp.max(_lane_fold(s, jnp.maximum), axis=-1, keepdims=True))
    p = jnp.exp2(s - m_new)
    acc = jnp.exp2(m - m_new) * acc + _dot(p.astype(BF16), kvt)
    return m_new, acc


def _attn_kernel(*refs, n_lat_tiles, tk, tq):
    if n_lat_tiles:
        q_ref, kc_ref, kvc_ref, kl_ref, kvl_ref, o_ref, s_ref, mx_ref, p_ref, m_ref, acc_ref = refs
    else:
        q_ref, kc_ref, kvc_ref, o_ref = refs
    sls = [slice(hh * HEAD_SLOT, (hh + 1) * HEAD_SLOT) for hh in range(2)]

    def q_tile(qi, c):
        qrows = pl.ds(pl.multiple_of(qi * tq, tq), tq)
        qs = [q_ref[0, qrows, sl] for sl in sls]
        init = (jnp.full((tq, 1), -jnp.inf, F32), jnp.zeros((tq, HEAD_SLOT), F32))
        carry = tuple(_attn_update(_dot_nt(qs[hh], kc_ref[0, :, sls[hh]]), kvc_ref[0, :, sls[hh]], init)
                      for hh in range(2))
        if n_lat_tiles:
            for hh in range(2):
                m, acc = carry[hh]
                m_ref[hh] = jnp.broadcast_to(m, (tq, HEAD_SLOT))
                acc_ref[hh] = acc
            rows = lambda i: slice(i * tk, (i + 1) * tk)

            def scores(i, slot):
                for hh in range(2):
                    s = _dot_nt(qs[hh], kl_ref[0, rows(i), sls[hh]])
                    s_ref[slot, hh] = s
                    mx_ref[slot, hh] = _lane_fold(s, jnp.maximum)

            def update(i, slot):
                for hh in range(2):
                    for r in range(0, tq, ATT_STRIP):
                        rs = slice(r, r + ATT_STRIP)
                        m_old = m_ref[hh, rs, :]
                        m_new = jnp.maximum(m_old, jnp.max(mx_ref[slot, hh, rs, :], axis=-1, keepdims=True))
                        p = jnp.exp2(s_ref[slot, hh, rs, :] - jnp.concatenate([m_new] * (tk // HEAD_SLOT), axis=1))
                        m_ref[hh, rs, :] = m_new
                        acc_ref[hh, rs, :] = jnp.exp2(m_old - m_new) * acc_ref[hh, rs, :]
                        p_ref[hh, rs, :] = p.astype(BF16)
                    acc_ref[hh] += _dot(p_ref[hh], kvl_ref[0, rows(i), sls[hh]])

            scores(0, 0)
            for i in range(n_lat_tiles):
                if i + 1 < n_lat_tiles:
                    scores(i + 1, (i + 1) % 2)
                update(i, i % 2)
            carry = tuple((None, acc_ref[hh]) for hh in range(2))
        outs = [(acc * pltpu.roll(1.0 / acc, MLA_NOPE, 1))[:, MLA_NOPE:] for _, acc in carry]
        o_ref[0, qrows, :] = jnp.concatenate(outs, axis=1).astype(o_ref.dtype)
        return c

    lax.fori_loop(0, q_ref.shape[1] // tq, q_tile, 0)


def _attn_call(q, k_ctx, kv_ctx, k_lat, kv_lat, tq, tk):
    Bn, Tq, D = q.shape
    C = k_ctx.shape[1]
    hp = MLA_HEADS // 2
    w2 = 2 * HEAD_SLOT
    n_lat_tiles = 0 if k_lat is None else k_lat.shape[1] // tk
    in_specs = [pl.BlockSpec((1, Tq, w2), lambda b, h: (b, 0, h)),
                pl.BlockSpec((1, C, w2), lambda b, h: (b, 0, h)),
                pl.BlockSpec((1, C, w2), lambda b, h: (b, 0, h))]
    args = [q, k_ctx, kv_ctx]
    scratch = []
    if n_lat_tiles:
        assert tq % ATT_STRIP == 0
        Tk = k_lat.shape[1]
        in_specs += [pl.BlockSpec((1, Tk, w2), lambda b, h: (b, 0, h)),
                     pl.BlockSpec((1, Tk, w2), lambda b, h: (b, 0, h))]
        args += [k_lat, kv_lat]
        stat = pltpu.VMEM((2, tq, HEAD_SLOT), F32)
        scratch = [pltpu.VMEM((2, 2, tq, tk), F32), pltpu.VMEM((2, 2, tq, HEAD_SLOT), F32),
                   pltpu.VMEM((2, tq, tk), BF16), stat, stat]
    return pl.pallas_call(
        functools.partial(_attn_kernel, n_lat_tiles=n_lat_tiles, tk=tk, tq=tq),
        out_shape=jax.ShapeDtypeStruct((Bn, Tq, MLA_WIDTH), BF16),
        grid=(Bn, hp),
        in_specs=in_specs,
        out_specs=pl.BlockSpec((1, Tq, 2 * MLA_V), lambda b, h: (b, 0, h)),
        scratch_shapes=scratch,
        compiler_params=_params(("parallel", "parallel")),
        name="attn",
    )(*args)


def _s5_disc_kernel(lr_ref, li_ref, ldt_ref, br_ref, bi_ref, ar_ref, ai_ref, bbr_ref, bbi_ref):
    lr = lr_ref[0]
    li = li_ref[0]
    dt = jnp.exp(ldt_ref[0])
    mag = jnp.exp(dt * lr)
    ar = mag * jnp.cos(dt * li)
    ai = mag * jnp.sin(dt * li)
    den = lr * lr + li * li
    fr = ((ar - 1.0) * lr + ai * li) / den
    fi = (ai * lr - (ar - 1.0) * li) / den
    ar_ref[0] = ar
    ai_ref[0] = ai
    br = br_ref[...]
    bi = bi_ref[...]
    bbr_ref[0] = fr * br - fi * bi
    bbi_ref[0] = fr * bi + fi * br


def _s5_disc_call(lam_re, lam_im, log_dt, b_re, b_im):
    n = SSM_N
    lr = lam_re.reshape(2, 1, n)
    li = lam_im.reshape(2, 1, n)
    ldt = jnp.repeat(log_dt, SSM_STATE, axis=1).reshape(2, 1, n)
    br = b_re.reshape(n, SSM_GROUP).T
    bi = b_im.reshape(n, SSM_GROUP).T
    vec = pl.BlockSpec((1, 1, n), lambda d: (d, 0, 0))
    mat = pl.BlockSpec((SSM_GROUP, n), lambda d: (0, 0))
    omat = pl.BlockSpec((1, SSM_GROUP, n), lambda d: (d, 0, 0))
    return pl.pallas_call(
        _s5_disc_kernel,
        out_shape=[jax.ShapeDtypeStruct((2, 1, n), F32), jax.ShapeDtypeStruct((2, 1, n), F32),
                   jax.ShapeDtypeStruct((2, SSM_GROUP, n), F32), jax.ShapeDtypeStruct((2, SSM_GROUP, n), F32)],
        grid=(2,),
        in_specs=[vec, vec, vec, mat, mat],
        out_specs=[vec, vec, omat, omat],
        compiler_params=_params(("arbitrary",)),
        name="s5_disc",
    )(lr, li, ldt, br, bi)


def _s5_scan_kernel(uf_ref, ub_ref, bd_ref, ar_ref, ai_ref, cre_ref, cim_ref, h0_ref,
                    yf_ref, yb_ref, hfin_ref, xs_ref, h_ref, *, tl, nb):
    i = pl.program_id(0)
    n = SSM_N

    @pl.when(i == 0)
    def _():
        h_ref[...] = h0_ref[...]

    half = n // 2
    hw = SSM_WIDTH // 2
    tp = tl // S5_PARTS

    def part_start(d, part):
        return part * tp if d == 0 else tl - (part + 1) * tp

    for part in range(S5_PARTS):
        for d, u_ref in enumerate((uf_ref, ub_ref)):
            t0 = part_start(d, part)
            rows = slice(t0 * nb, (t0 + tp) * nb)
            u = u_ref[t0:t0 + tp].reshape(tp * nb, SSM_WIDTH).astype(BF16)
            for k in range(2):
                xk = _dot(u[:, k * hw:(k + 1) * hw], bd_ref[d, k])
                xs_ref[d, rows, k * half:(k + 1) * half] = xk[:, :half]
                xs_ref[d, rows, n + k * half:n + (k + 1) * half] = xk[:, half:]

    state = {}
    for d in range(2):
        for cb in range(2):
            re = slice(cb * half, (cb + 1) * half)
            im = slice(n + cb * half, n + (cb + 1) * half)
            state[d, cb] = (h_ref[d, :, re], h_ref[d, :, im], jnp.broadcast_to(ar_ref[d, :, re], (nb, half)),
                            jnp.broadcast_to(ai_ref[d, :, re], (nb, half)), re, im)
    for t in range(tl):
        for d in range(2):
            tt = t if d == 0 else tl - 1 - t
            rows = slice(tt * nb, (tt + 1) * nb)
            for cb in range(2):
                hr, hi, ar, ai, re, im = state[d, cb]
                nr = ar * hr - ai * hi + xs_ref[d, rows, re]
                ni = ar * hi + ai * hr + xs_ref[d, rows, im]
                xs_ref[d, rows, re] = nr
                xs_ref[d, rows, im] = ni
                state[d, cb] = (nr, ni, ar, ai, re, im)
    for (d, cb), (hr, hi, _, _, re, im) in state.items():
        h_ref[d, :, re] = hr
        h_ref[d, :, im] = hi

    for part in range(S5_PARTS):
        for d, y_ref in enumerate((yf_ref, yb_ref)):
            t0 = part_start(d, part)
            rows = slice(t0 * nb, (t0 + tp) * nb)
            ys = []
            for k in range(2):
                hr = xs_ref[d, rows, k * half:(k + 1) * half].astype(BF16)
                hi = xs_ref[d, rows, n + k * half:n + (k + 1) * half].astype(BF16)
                ys.append(_dot(hr, cre_ref[k]) - _dot(hi, cim_ref[k]))
            y_ref[t0:t0 + tp] = jnp.concatenate(ys, axis=1).reshape(tp, nb, SSM_WIDTH)

    @pl.when(i == pl.num_programs(0) - 1)
    def _():
        hfin_ref[...] = h_ref[...]


def _s5_scan_call(u_tb, bd, ar, ai, cre_t, cim_t, h0, tl):
    Tn, nb, W = u_tb.shape
    nc = Tn // tl
    n = SSM_N
    fwd = pl.BlockSpec((tl, nb, W), lambda i: (i, 0, 0))
    bwd = pl.BlockSpec((tl, nb, W), lambda i: (nc - 1 - i, 0, 0))
    return pl.pallas_call(
        functools.partial(_s5_scan_kernel, tl=tl, nb=nb),
        out_shape=[jax.ShapeDtypeStruct((Tn, nb, W), F32), jax.ShapeDtypeStruct((Tn, nb, W), F32),
                   jax.ShapeDtypeStruct((2, nb, 2 * n), F32)],
        grid=(nc,),
        in_specs=[fwd, bwd, _const_spec(bd.shape), _const_spec(ar.shape), _const_spec(ai.shape),
                  _const_spec(cre_t.shape), _const_spec(cim_t.shape), _const_spec(h0.shape)],
        out_specs=[fwd, bwd, pl.BlockSpec((2, nb, 2 * n), lambda i: (0, 0, 0))],
        scratch_shapes=[pltpu.VMEM((2, tl * nb, 2 * n), F32), pltpu.VMEM((2, nb, 2 * n), F32)],
        compiler_params=_params(("arbitrary",)),
        name="s5_scan",
    )(u_tb, u_tb, bd, ar, ai, cre_t, cim_t, h0)


def _s5_out_kernel(u_ref, yf_ref, yb_ref, d_ref, wglu_ref, bglu_ref, o_ref):
    y = d_ref[...] * u_ref[...] + yf_ref[...] + yb_ref[...]
    g = jax.nn.gelu(y)
    z = _dot(g.astype(BF16), wglu_ref[...]) + bglu_ref[...]
    o_ref[...] = (g * jax.nn.sigmoid(z)).astype(o_ref.dtype)


def _s5_out_call(u, yf, yb, d_skip, wglu, bglu, tr):
    R, W = u.shape
    tr = min(tr, R)
    blk = pl.BlockSpec((tr, W), lambda i: (i, 0))
    return pl.pallas_call(
        _s5_out_kernel,
        out_shape=jax.ShapeDtypeStruct((R, W), BF16),
        grid=(R // tr,),
        in_specs=[blk, blk, blk, _const_spec(d_skip.shape), _const_spec(wglu.shape), _const_spec(bglu.shape)],
        out_specs=blk,
        compiler_params=_params(("parallel",)),
        name="s5_out",
    )(u, yf, yb, d_skip, wglu, bglu)


def _split3(x):
    hi = x.astype(BF16)
    r1 = x - hi.astype(F32)
    mid = r1.astype(BF16)
    lo = (r1 - mid.astype(F32)).astype(BF16)
    return hi, mid, lo


def _gla_block(q, k, v, lg, s, d, cst, n_chunks):
    tri_ref, mask_ref, same_ref, hmask_ref, bmask_ref = cst
    L = GLA_CHUNK
    parts = _split3(lg)
    b = sum(_dot(tri_ref[d], p) for p in parts)
    tot = sum(_dot(same_ref[...], p) for p in parts)
    q_in = q * (GLA_DK ** -0.5) * jnp.exp(b)
    k_in = (k * jnp.exp(-b)).astype(BF16)
    k_dec = (k * jnp.exp(tot - b)).astype(BF16)
    vb = v.astype(BF16)
    mask = mask_ref[d] > 0.0
    outs = []
    for hd in range(GLA_HEADS):
        qh = (q_in * hmask_ref[hd]).astype(BF16)
        att = jnp.where(mask, _dot_nt(qh, k_in), 0.0).astype(BF16)
        outs.append(_dot(att, vb[:, hd * GLA_DV:(hd + 1) * GLA_DV]))
    o_intra = jnp.concatenate(outs, axis=1)
    q_in_b = q_in.astype(BF16)
    bmask = bmask_ref[...]
    o_rows = [None] * n_chunks
    for c in (range(n_chunks) if d == 0 else reversed(range(n_chunks))):
        rows = slice(c * L, (c + 1) * L)
        o_rows[c] = _dot_nt(q_in_b[rows], s.astype(BF16))
        dec = jnp.exp(tot[c * L:c * L + 1, :])
        s = dec * s + _dot_tn(vb[rows], k_dec[rows]) * bmask
    return o_intra + jnp.concatenate(o_rows, axis=0), s


def _gla_kernel(qf_ref, kf_ref, vf_ref, lgf_ref, qb_ref, kb_ref, vb_ref, lgb_ref, s0_ref,
                tri_ref, mask_ref, same_ref, hmask_ref, bmask_ref,
                of_ref, ob_ref, sfin_ref, s_ref, *, n_chunks):
    i = pl.program_id(1)
    cst = (tri_ref, mask_ref, same_ref, hmask_ref, bmask_ref)

    @pl.when(i == 0)
    def _():
        s_ref[...] = s0_ref[0]

    o, s = _gla_block(qf_ref[0], kf_ref[0], vf_ref[0], lgf_ref[0], s_ref[0], 0, cst, n_chunks)
    of_ref[0] = o
    s_ref[0] = s
    o, s = _gla_block(qb_ref[0], kb_ref[0], vb_ref[0], lgb_ref[0], s_ref[1], 1, cst, n_chunks)
    ob_ref[0] = o
    s_ref[1] = s

    @pl.when(i == pl.num_programs(1) - 1)
    def _():
        sfin_ref[0] = s_ref[...]


def _gla_consts(tg):
    t = jnp.arange(tg)
    same = (t[:, None] // GLA_CHUNK) == (t[None, :] // GLA_CHUNK)
    low = same & (t[None, :] <= t[:, None])
    upp = same & (t[None, :] >= t[:, None])
    tri = jnp.stack([low, upp])
    hmask = (jnp.arange(GLA_KW)[None, None, :] // GLA_DK) == jnp.arange(GLA_HEADS)[:, None, None]
    bmask = (jnp.arange(GLA_WIDTH)[:, None] // GLA_DV) == (jnp.arange(GLA_KW)[None, :] // GLA_DK)
    return tri.astype(BF16), tri.astype(F32), same.astype(BF16), hmask.astype(F32), bmask.astype(F32)


def _gla_call(q, k, v, lgf, lgb, s0, tg):
    Bn, Tn, _ = q.shape
    nblk = Tn // tg
    consts = _gla_consts(tg)
    fwd = lambda w: pl.BlockSpec((1, tg, w), lambda b, i: (b, i, 0))
    bwd = lambda w: pl.BlockSpec((1, tg, w), lambda b, i: (b, nblk - 1 - i, 0))
    st = pl.BlockSpec((1, 2, GLA_WIDTH, GLA_KW), lambda b, i: (b, 0, 0, 0))
    return pl.pallas_call(
        functools.partial(_gla_kernel, n_chunks=tg // GLA_CHUNK),
        out_shape=[jax.ShapeDtypeStruct((Bn, Tn, GLA_WIDTH), F32), jax.ShapeDtypeStruct((Bn, Tn, GLA_WIDTH), F32),
                   jax.ShapeDtypeStruct((Bn, 2, GLA_WIDTH, GLA_KW), F32)],
        grid=(Bn, nblk),
        in_specs=[fwd(GLA_KW), fwd(GLA_KW), fwd(GLA_WIDTH), fwd(GLA_KW),
                  bwd(GLA_KW), bwd(GLA_KW), bwd(GLA_WIDTH), bwd(GLA_KW), st]
                 + [_const_spec(a.shape) for a in consts],
        out_specs=[fwd(GLA_WIDTH), bwd(GLA_WIDTH), st],
        scratch_shapes=[pltpu.VMEM((2, GLA_WIDTH, GLA_KW), F32)],
        compiler_params=_params(("parallel", "arbitrary")),
        name="gla",
    )(q, k, v, lgf, q, k, v, lgb, s0, *consts)


def _merge_kernel(x_ref, oa_ref, ys_ref, of_ref, ob_ref, sh_ref, sc_ref, g1_ref, n1_ref,
                  wgate_ref, wr_ref, wmo_ref, wso_ref, gn_ref, wgo_ref, wout_ref, o_ref):
    x = x_ref[0]
    h = _rms(x) * n1_ref[...]
    hb = (h * (1.0 + sc_ref[0]) + sh_ref[0]).astype(BF16)
    ya = _dot(oa_ref[0], wmo_ref[...])
    yb = _dot(ys_ref[0], wso_ref[...])
    o = of_ref[0] + ob_ref[0]
    on = jnp.concatenate([_rms(o[:, hd * GLA_DV:(hd + 1) * GLA_DV]) for hd in range(GLA_HEADS)], axis=1)
    r = _dot(hb, wr_ref[...])
    yg = _dot(((on * gn_ref[...]) * (r * jax.nn.sigmoid(r))).astype(BF16), wgo_ref[...])
    m = jnp.zeros_like(ya)
    for j, br in enumerate((ya, yb, yg)):
        gate = jax.nn.sigmoid(_dot(hb, wgate_ref[:, j * D_MODEL:(j + 1) * D_MODEL]))
        m = m + gate * br
    o_ref[0] = x + g1_ref[0] * _dot(m.astype(BF16), wout_ref[...])


def _merge_call(x, oa, ys, of, ob, mod_sh, mod_sc, mod_g, mod_row, n1, w, tm):
    Bn, Tn, D = x.shape
    row = (lambda b, i: (b, 0, 0)) if mod_row is None else (lambda b, i: (mod_row, 0, 0))
    tok = lambda width: pl.BlockSpec((1, tm, width), lambda b, i: (b, i, 0))
    mrow = pl.BlockSpec((1, 1, D), row)
    weights = [w["wgate"], w["wr"], w["wmo"], w["wso"], w["gn"], w["wgo"], w["wout"]]
    return pl.pallas_call(
        _merge_kernel,
        out_shape=jax.ShapeDtypeStruct((Bn, Tn, D), F32),
        grid=(Bn, Tn // tm),
        in_specs=[tok(D), tok(MLA_WIDTH), tok(SSM_WIDTH), tok(GLA_WIDTH), tok(GLA_WIDTH),
                  mrow, mrow, mrow, _const_spec((1, D))] + [_const_spec(a.shape) for a in weights],
        out_specs=tok(D),
        compiler_params=_params(("parallel", "parallel")),
        name="merge",
    )(x, oa, ys, of, ob, mod_sh, mod_sc, mod_g, n1, *weights)


def _ffn_kernel(x_ref, sh_ref, sc_ref, g2_ref, n2_ref, wg_ref, wu_ref, wd_ref, fn_ref, o_ref, *, final):
    x = x_ref[0]
    h = _rms(x) * n2_ref[...]
    hb = (h * (1.0 + sc_ref[0]) + sh_ref[0]).astype(BF16)
    acc = jnp.zeros_like(x)
    for cs in (slice(0, FFN_SPLIT), slice(FFN_SPLIT, FFN_HIDDEN)):
        g = _dot(hb, wg_ref[:, cs])
        u = _dot(hb, wu_ref[:, cs])
        acc = acc + _dot(((g * jax.nn.sigmoid(g)) * u).astype(BF16), wd_ref[cs, :])
    y = x + g2_ref[0] * acc
    if final:
        y = _rms(y) * fn_ref[...]
    o_ref[0] = y


def _ffn_call(x, mod_sh, mod_sc, mod_g, mod_row, n2, wg, wu, wd, fn, final, tm):
    Bn, Tn, D = x.shape
    row = (lambda b, i: (b, 0, 0)) if mod_row is None else (lambda b, i: (mod_row, 0, 0))
    tok = pl.BlockSpec((1, tm, D), lambda b, i: (b, i, 0))
    mrow = pl.BlockSpec((1, 1, D), row)
    return pl.pallas_call(
        functools.partial(_ffn_kernel, final=final),
        out_shape=jax.ShapeDtypeStruct((Bn, Tn, D), F32),
        grid=(Bn, Tn // tm),
        in_specs=[tok, mrow, mrow, mrow, _const_spec((1, D)), _const_spec(wg.shape), _const_spec(wu.shape),
                  _const_spec(wd.shape), _const_spec((1, D))],
        out_specs=tok,
        compiler_params=_params(("parallel", "parallel")),
        name="ffn",
    )(x, mod_sh, mod_sc, mod_g, n2, wg, wu, wd, fn)


def _pair_swap(w):
    w2 = w.reshape(w.shape[0], -1, 2)
    return jnp.stack([-w2[..., 1], w2[..., 0]], axis=-1).reshape(w.shape)


def _layer_weights(l, w_in, mla_q_norm, mla_w_uq, mla_kv_norm, mla_w_ukv, gla_w_a2, gla_b_a2):
    wi = w_in[l]
    D = wi.shape[0]
    o = 0
    cuts = {}
    for name, sz in (("cq", MLA_RANK), ("ckv", MLA_RANK), ("kr", MLA_ROPE), ("u", SSM_WIDTH), ("gq", GLA_KW),
                     ("gk", GLA_KW), ("gv", GLA_WIDTH), ("gr", GLA_WIDTH), ("af", GLA_GATE_RANK),
                     ("ab", GLA_GATE_RANK), ("gate", 3 * D_MODEL)):
        cuts[name] = wi[:, o:o + sz]
        o += sz
    zeros = lambda n: jnp.zeros((D, n), F32)
    kr = cuts["kr"]
    pad_k = HEAD_SLOT - MLA_NOPE - MLA_ROPE
    wkr = jnp.concatenate([zeros(MLA_NOPE), kr, zeros(pad_k), zeros(MLA_NOPE), _pair_swap(kr), zeros(pad_k)], axis=1)
    wlr = jnp.concatenate([cuts["af"], cuts["ab"], zeros(LR_PAD - 2 * GLA_GATE_RANK)], axis=1)
    uq = mla_w_uq[l].reshape(MLA_RANK, MLA_HEADS, MLA_NOPE + MLA_ROPE)
    zq = jnp.zeros((MLA_RANK, MLA_HEADS, pad_k), F32)
    wuq = jnp.concatenate([uq, zq], axis=2).reshape(MLA_RANK, MLA_HEADS * HEAD_SLOT)
    uq_sw = _pair_swap(uq[..., MLA_NOPE:].reshape(MLA_RANK, -1)).reshape(MLA_RANK, MLA_HEADS, MLA_ROPE)
    wuqs = jnp.concatenate([jnp.zeros((MLA_RANK, MLA_HEADS, MLA_NOPE), F32), uq_sw, zq], axis=2)
    wuqs = wuqs.reshape(MLA_RANK, MLA_HEADS * HEAD_SLOT)
    wa2 = jnp.zeros((2, LR_PAD, GLA_KW), F32)
    wa2 = wa2.at[0, :GLA_GATE_RANK].set(gla_w_a2[l, 0]).at[1, GLA_GATE_RANK:2 * GLA_GATE_RANK].set(gla_w_a2[l, 1])
    b16 = lambda a: a.astype(BF16)
    return dict(
        wa=b16(jnp.concatenate([cuts["cq"], cuts["ckv"]], axis=1)), wkr=b16(wkr), wu=b16(cuts["u"]),
        wg=b16(jnp.concatenate([cuts["gq"], cuts["gk"], cuts["gv"]], axis=1)), wlr=b16(wlr),
        qn=mla_q_norm[l].reshape(1, -1), kvn=mla_kv_norm[l].reshape(1, -1),
        wuq=b16(wuq), wuqs=b16(wuqs), wukv=b16(mla_w_ukv[l]), wa2=b16(wa2),
        ba2=gla_b_a2[l].reshape(2, 1, GLA_KW), wr=b16(cuts["gr"]), wgate=b16(cuts["gate"]))


def _rope_tables(T):
    t = jnp.arange(T, dtype=jnp.int32)
    rows = (t // GRID_W).astype(F32)
    cols = (t % GRID_W).astype(F32)
    inv = ROPE_BASE ** (-jnp.arange(ROPE_FREQS, dtype=F32) / ROPE_FREQS)
    ang = jnp.concatenate([rows[:, None] * inv, cols[:, None] * inv], axis=-1)
    ang2 = jnp.repeat(ang, 2, axis=1)
    pad_k = HEAD_SLOT - MLA_NOPE - MLA_ROPE
    cos = jnp.concatenate([jnp.ones((T, MLA_NOPE), F32), jnp.cos(ang2), jnp.ones((T, pad_k), F32)], axis=1)
    sin = jnp.concatenate([jnp.zeros((T, MLA_NOPE), F32), jnp.sin(ang2), jnp.zeros((T, pad_k), F32)], axis=1)
    return cos, sin


def _s5_matrices(bbr, bbi, c_re, c_im):
    n = SSM_N
    grp_row = jnp.arange(SSM_WIDTH)[:, None] // SSM_GROUP
    grp_col = jnp.arange(n)[None, :] // SSM_STATE
    blk = grp_row == grp_col
    tile = lambda m: jnp.where(blk, jnp.tile(m, (SSM_GROUPS, 1)), 0.0)
    hw, hn = SSM_WIDTH // 2, n // 2
    halves = lambda m: [m[k * hw:(k + 1) * hw, k * hn:(k + 1) * hn] for k in range(2)]
    bd = jnp.stack([jnp.stack([jnp.concatenate([r, i], axis=1)
                               for r, i in zip(halves(tile(bbr[d])), halves(tile(bbi[d])))])
                    for d in range(2)]).astype(BF16)

    def readout(c):
        ct = jnp.transpose(c, (0, 2, 1)).reshape(n, SSM_GROUP)
        full = jnp.where(blk.T, jnp.tile(ct, (1, SSM_GROUPS)), 0.0)
        return jnp.stack([full[k * hn:(k + 1) * hn, k * hw:(k + 1) * hw] for k in range(2)]).astype(BF16)

    return bd, readout(c_re), readout(c_im)


def kernel(x, c, ctx, c_ctx, w_mod, b_mod, norm1, norm2, w_in, mla_q_norm, mla_w_uq, mla_kv_norm, mla_w_ukv, mla_w_o, ssm_lam_re, ssm_lam_im, ssm_log_dt, ssm_b_re, ssm_b_im, ssm_c_re, ssm_c_im, ssm_d, ssm_w_glu, ssm_b_glu, ssm_w_o, gla_w_a2, gla_b_a2, gla_norm, gla_w_o, w_out, ffn_w_up, ffn_w_down, final_norm):
    B, T, D = x.shape
    C = ctx.shape[1]
    depth = w_mod.shape[0]
    n_rows = -(-(B + 1) // 8) * 8
    cvec = jnp.zeros((n_rows, D), F32).at[:B].set(c).at[B].set(c_ctx)
    mod = _mod_call(cvec, w_mod.astype(BF16), b_mod)
    cos_l, sin_l = _rope_tables(T)
    cos_c = jnp.ones((C, HEAD_SLOT), F32)
    sin_c = jnp.zeros((C, HEAD_SLOT), F32)
    tm_in, tm_merge, tm_ffn = min(TM_IN, T), min(TM_MERGE, T), min(TM_FFN, T)
    tq, tk = min(TQ, T), min(TK, T // 2)
    tg = min(GLA_BLOCK, T)
    b16 = lambda a: a.astype(BF16)

    xc = ctx
    for l in range(depth):
        with_ctx = l < depth - 1
        m6 = [mod[l, :, j * D:(j + 1) * D].reshape(n_rows, 1, D) for j in range(6)]
        sh1, sc1, g1, sh2, sc2, g2 = m6
        n1 = norm1[l].reshape(1, D)
        n2 = norm2[l].reshape(1, D)
        w = _layer_weights(l, w_in, mla_q_norm, mla_w_uq, mla_kv_norm, mla_w_ukv, gla_w_a2, gla_b_a2)
        w.update(wmo=b16(mla_w_o[l]), wso=b16(ssm_w_o[l]), gn=gla_norm[l].reshape(1, -1), wgo=b16(gla_w_o[l]),
                 wout=b16(w_out[l]))
        up = ffn_w_up[l]
        wup_g, wup_u, wdown = b16(up[:, :FFN_HIDDEN]), b16(up[:, FFN_HIDDEN:]), b16(ffn_w_down[l])
        fn = final_norm.reshape(1, D)

        qc, kc, kvc, uc, gqc, gkc, gvc, lgfc, lgbc = _in_proj_call(xc, sh1, sc1, B, n1, cos_c, sin_c, w, C)
        ql, kl, kvl, ul, gql, gkl, gvl, lgfl, lgbl = _in_proj_call(x, sh1, sc1, None, n1, cos_l, sin_l, w, tm_in)

        oa = _attn_call(ql, kc, kvc, kl, kvl, tq, tk)
        oac = _attn_call(qc, kc, kvc, None, None, C, tk) if with_ctx else None

        ar, ai, bbr, bbi = _s5_disc_call(ssm_lam_re[l], ssm_lam_im[l], ssm_log_dt[l], ssm_b_re[l], ssm_b_im[l])
        bd, cre_t, cim_t = _s5_matrices(bbr, bbi, ssm_c_re[l], ssm_c_im[l])
        uc_tb = jnp.transpose(uc, (1, 0, 2))
        ul_tb = jnp.transpose(ul, (1, 0, 2))
        h0 = jnp.zeros((2, B, 2 * SSM_N), F32)
        yfc, ybc, hc = _s5_scan_call(uc_tb, bd, ar, ai, cre_t, cim_t, h0, S5_TL)
        yfl, ybl, _ = _s5_scan_call(ul_tb, bd, ar, ai, cre_t, cim_t, hc, S5_TL)
        d_skip = ssm_d[l].reshape(1, -1)
        wglu, bglu = b16(ssm_w_glu[l]), ssm_b_glu[l].reshape(1, -1)
        flat = lambda a: a.reshape(-1, SSM_WIDTH)
        ysl = _s5_out_call(flat(ul_tb), flat(yfl), flat(ybl), d_skip, wglu, bglu, S5_ROWS_OUT)
        ysl = jnp.transpose(ysl.reshape(T, B, SSM_WIDTH), (1, 0, 2))

        s0 = jnp.zeros((B, 2, GLA_WIDTH, GLA_KW), F32)
        ofc, obc, sc_fin = _gla_call(gqc, gkc, gvc, lgfc, lgbc, s0, min(GLA_BLOCK, C))
        ofl, obl, _ = _gla_call(gql, gkl, gvl, lgfl, lgbl, sc_fin, tg)

        x = _merge_call(x, oa, ysl, ofl, obl, sh1, sc1, g1, None, n1, w, tm_merge)
        x = _ffn_call(x, sh2, sc2, g2, None, n2, wup_g, wup_u, wdown, fn, l == depth - 1, tm_ffn)
        if with_ctx:
            ysc = _s5_out_call(flat(uc_tb), flat(yfc), flat(ybc), d_skip, wglu, bglu, S5_ROWS_OUT)
            ysc = jnp.transpose(ysc.reshape(C, B, SSM_WIDTH), (1, 0, 2))
            xc = _merge_call(xc, oac, ysc, ofc, obc, sh1, sc1, g1, B, n1, w, C)
            xc = _ffn_call(xc, sh2, sc2, g2, B, n2, wup_g, wup_u, wdown, fn, False, C)
    return x
```

```python
import functools
import math

import jax
import jax.numpy as jnp
from jax import lax
from jax.experimental import pallas as pl
from jax.experimental.pallas import tpu as pltpu

F32 = jnp.float32
BF16 = jnp.bfloat16

D_MODEL = 1024
GRID_W = 64
EPS = 1e-6
MLA_HEADS = 8
MLA_RANK = 256
MLA_NOPE = 64
MLA_ROPE = 32
MLA_V = 64
MLA_WIDTH = MLA_HEADS * MLA_V
MLA_SCALE = (MLA_NOPE + MLA_ROPE) ** -0.5
ROPE_FREQS = MLA_ROPE // 4
ROPE_BASE = 10000.0
LOG2_E = math.log2(math.e)
HEAD_SLOT = 128
SSM_WIDTH = 512
SSM_GROUP = 16
SSM_GROUPS = SSM_WIDTH // SSM_GROUP
SSM_STATE = 64
SSM_N = SSM_GROUPS * SSM_STATE
GLA_HEADS = 4
GLA_DK = 64
GLA_DV = 128
GLA_KW = GLA_HEADS * GLA_DK
GLA_WIDTH = GLA_HEADS * GLA_DV
GLA_GATE_RANK = 16
GLA_GATE_TAU = 16.0
GLA_CHUNK = 64
FFN_HIDDEN = 2816
FFN_SPLIT = 1536
LR_PAD = 128

VMEM_LIMIT = 56 * 1024 * 1024

TM_IN = 512
TM_MERGE = 512
TM_FFN = 512
TQ = 512
TK = 1024
ATT_STRIP = 64
S5_TL = 64
S5_PARTS = 2
S5_ROWS_OUT = 2048
GLA_BLOCK = 256
GLA_BATCH = 4


def _const_spec(shape):
    nd = len(shape)
    return pl.BlockSpec(shape, lambda *_: (0,) * nd, pipeline_mode=pl.Buffered(1))


def _params(sem):
    return pltpu.CompilerParams(dimension_semantics=sem, vmem_limit_bytes=VMEM_LIMIT)


def _rms(x):
    return x * lax.rsqrt(jnp.mean(x * x, axis=-1, keepdims=True) + EPS)


def _dot(a, b):
    return jnp.dot(a, b, preferred_element_type=F32)


def _dot_nt(a, b):
    return lax.dot_general(a, b, (((1,), (1,)), ((), ())), preferred_element_type=F32)


def _dot_tn(a, b):
    return lax.dot_general(a, b, (((0,), (0,)), ((), ())), preferred_element_type=F32)


def _mod_kernel(c_ref, w_ref, b_ref, o_ref):
    c = c_ref[...]
    s = (c * jax.nn.sigmoid(c)).astype(BF16)
    o_ref[0] = _dot(s, w_ref[0]) + b_ref[0]


def _mod_call(cvec, w_mod, b_mod):
    L, D, N = w_mod.shape
    R = cvec.shape[0]
    tn = 1536
    return pl.pallas_call(
        _mod_kernel,
        out_shape=jax.ShapeDtypeStruct((L, R, N), F32),
        grid=(L, N // tn),
        in_specs=[pl.BlockSpec((R, D), lambda l, j: (0, 0)),
                  pl.BlockSpec((1, D, tn), lambda l, j: (l, 0, j)),
                  pl.BlockSpec((1, 1, tn), lambda l, j: (l, 0, j))],
        out_specs=pl.BlockSpec((1, R, tn), lambda l, j: (l, 0, j)),
        compiler_params=_params(("arbitrary", "arbitrary")),
        name="mod",
    )(cvec, w_mod, b_mod.reshape(L, 1, N))


def _in_proj_kernel(x_ref, sh_ref, sc_ref, n1_ref, cos_ref, sin_ref,
                    wa_ref, wkr_ref, wu_ref, wg_ref, wlr_ref,
                    qn_ref, kvn_ref, wuq_ref, wuqs_ref, wukv_ref, wa2_ref, ba2_ref,
                    q_ref, k_ref, kv_ref, u_ref, gq_ref, gk_ref, gv_ref, lgf_ref, lgb_ref):
    x = x_ref[0]
    h = _rms(x) * n1_ref[...]
    h = h * (1.0 + sc_ref[0]) + sh_ref[0]
    hb = h.astype(BF16)
    cos = cos_ref[...]
    sin = sin_ref[...]
    cos8 = jnp.concatenate([cos] * MLA_HEADS, axis=1)
    sin8 = jnp.concatenate([sin] * MLA_HEADS, axis=1)

    za = _dot(hb, wa_ref[...])
    cqn = (_rms(za[:, :MLA_RANK]) * qn_ref[...]).astype(BF16)
    q = _dot(cqn, wuq_ref[...]) * cos8 + _dot(cqn, wuqs_ref[...]) * sin8
    q_ref[0] = (q * (MLA_SCALE * LOG2_E)).astype(BF16)

    ckvn = (_rms(za[:, MLA_RANK:]) * kvn_ref[...]).astype(BF16)
    kv = _dot(ckvn, wukv_ref[...])
    kr2 = _dot(hb, wkr_ref[...])
    krr = kr2[:, :HEAD_SLOT] * cos + kr2[:, HEAD_SLOT:] * sin
    nope = lax.broadcasted_iota(jnp.int32, krr.shape, 1) < MLA_NOPE
    for hd in range(MLA_HEADS):
        sl = slice(hd * HEAD_SLOT, (hd + 1) * HEAD_SLOT)
        k_ref[0, :, sl] = jnp.where(nope, kv[:, sl], krr).astype(BF16)
        kv_ref[0, :, sl] = jnp.where(nope, 1.0, kv[:, sl]).astype(BF16)

    u_ref[0] = _dot(hb, wu_ref[...])
    zg = _dot(hb, wg_ref[...])
    gq_ref[0] = zg[:, :GLA_KW]
    gk_ref[0] = zg[:, GLA_KW:2 * GLA_KW]
    gv_ref[0] = zg[:, 2 * GLA_KW:]
    zlr = _dot(hb, wlr_ref[...]).astype(BF16)
    for d, o_ref in enumerate((lgf_ref, lgb_ref)):
        z = _dot(zlr, wa2_ref[d]) + ba2_ref[d]
        o_ref[0] = (jnp.minimum(z, 0.0) - jnp.log1p(jnp.exp(-jnp.abs(z)))) * (1.0 / GLA_GATE_TAU)


def _in_proj_call(x, mod_sh, mod_sc, mod_row, n1, cos_t, sin_t, w, tm):
    Bn, Tn, D = x.shape
    nt = Tn // tm
    row = (lambda b, i: (b, 0, 0)) if mod_row is None else (lambda b, i: (mod_row, 0, 0))
    tok = lambda width: pl.BlockSpec((1, tm, width), lambda b, i: (b, i, 0))
    weights = [w["wa"], w["wkr"], w["wu"], w["wg"], w["wlr"], w["qn"], w["kvn"],
               w["wuq"], w["wuqs"], w["wukv"], w["wa2"], w["ba2"]]
    outs = [(D, BF16), (D, BF16), (D, BF16), (SSM_WIDTH, F32), (GLA_KW, F32), (GLA_KW, F32),
            (GLA_WIDTH, F32), (GLA_KW, F32), (GLA_KW, F32)]
    out_shape = [jax.ShapeDtypeStruct((Bn, Tn, wd), dt) for wd, dt in outs]
    out_specs = [tok(wd) for wd, _ in outs]
    return pl.pallas_call(
        _in_proj_kernel,
        out_shape=out_shape,
        grid=(Bn, nt),
        in_specs=[tok(D),
                  pl.BlockSpec((1, 1, D), row), pl.BlockSpec((1, 1, D), row),
                  _const_spec((1, D)),
                  pl.BlockSpec((tm, HEAD_SLOT), lambda b, i: (i, 0)),
                  pl.BlockSpec((tm, HEAD_SLOT), lambda b, i: (i, 0))]
                 + [_const_spec(a.shape) for a in weights],
        out_specs=out_specs,
        compiler_params=_params(("parallel", "parallel")),
        name="in_proj",
    )(x, mod_sh, mod_sc, n1, cos_t, sin_t, *weights)


def _lane_fold(x, op):
    out = x[:, :HEAD_SLOT]
    for j in range(1, x.shape[1] // HEAD_SLOT):
        out = op(out, x[:, j * HEAD_SLOT:(j + 1) * HEAD_SLOT])
    return out


def _attn_update(s, kvt, carry):
    m, acc = carry
    m_new = jnp.maximum(m, jnp.max(_lane_fold(s, jnp.maximum), axis=-1, keepdims=True))
    p = jnp.exp2(s - m_new)
    acc = jnp.exp2(m - m_new) * acc + _dot(p.astype(BF16), kvt)
    return m_new, acc


def _attn_kernel(*refs, n_lat_tiles, tk, tq):
    if n_lat_tiles:
        q_ref, kc_ref, kvc_ref, kl_ref, kvl_ref, o_ref, s_ref, mx_ref, p_ref, m_ref, acc_ref = refs
    else:
        q_ref, kc_ref, kvc_ref, o_ref = refs
    sls = [slice(hh * HEAD_SLOT, (hh + 1) * HEAD_SLOT) for hh in range(2)]

    def q_tile(qi, c):
        qrows = pl.ds(pl.multiple_of(qi * tq, tq), tq)
        qs = [q_ref[0, qrows, sl] for sl in sls]
        init = (jnp.full((tq, 1), -jnp.inf, F32), jnp.zeros((tq, HEAD_SLOT), F32))
        carry = tuple(_attn_update(_dot_nt(qs[hh], kc_ref[0, :, sls[hh]]), kvc_ref[0, :, sls[hh]], init)
                      for hh in range(2))
        if n_lat_tiles:
            for hh in range(2):
                m, acc = carry[hh]
                m_ref[hh] = jnp.broadcast_to(m, (tq, HEAD_SLOT))
                acc_ref[hh] = acc
            rows = lambda i: slice(i * tk, (i + 1) * tk)

            def scores(i, slot):
                for hh in range(2):
                    s = _dot_nt(qs[hh], kl_ref[0, rows(i), sls[hh]])
                    s_ref[slot, hh] = s
                    mx_ref[slot, hh] = _lane_fold(s, jnp.maximum)

            def update(i, slot):
                for hh in range(2):
                    for r in range(0, tq, ATT_STRIP):
                        rs = slice(r, r + ATT_STRIP)
                        m_old = m_ref[hh, rs, :]
                        m_new = jnp.maximum(m_old, jnp.max(mx_ref[slot, hh, rs, :], axis=-1, keepdims=True))
                        p = jnp.exp2(s_ref[slot, hh, rs, :] - jnp.concatenate([m_new] * (tk // HEAD_SLOT), axis=1))
                        m_ref[hh, rs, :] = m_new
                        acc_ref[hh, rs, :] = jnp.exp2(m_old - m_new) * acc_ref[hh, rs, :]
                        p_ref[hh, rs, :] = p.astype(BF16)
                    acc_ref[hh] += _dot(p_ref[hh], kvl_ref[0, rows(i), sls[hh]])

            scores(0, 0)
            for i in range(n_lat_tiles):
                if i + 1 < n_lat_tiles:
                    scores(i + 1, (i + 1) % 2)
                update(i, i % 2)
            carry = tuple((None, acc_ref[hh]) for hh in range(2))
        outs = [(acc * pltpu.roll(1.0 / acc, MLA_NOPE, 1))[:, MLA_NOPE:] for _, acc in carry]
        o_ref[0, qrows, :] = jnp.concatenate(outs, axis=1).astype(o_ref.dtype)
        return c

    lax.fori_loop(0, q_ref.shape[1] // tq, q_tile, 0)


def _attn_call(q, k_ctx, kv_ctx, k_lat, kv_lat, tq, tk):
    Bn, Tq, D = q.shape
    C = k_ctx.shape[1]
    hp = MLA_HEADS // 2
    w2 = 2 * HEAD_SLOT
    n_lat_tiles = 0 if k_lat is None else k_lat.shape[1] // tk
    in_specs = [pl.BlockSpec((1, Tq, w2), lambda b, h: (b, 0, h)),
                pl.BlockSpec((1, C, w2), lambda b, h: (b, 0, h)),
                pl.BlockSpec((1, C, w2), lambda b, h: (b, 0, h))]
    args = [q, k_ctx, kv_ctx]
    scratch = []
    if n_lat_tiles:
        assert tq % ATT_STRIP == 0
        Tk = k_lat.shape[1]
        in_specs += [pl.BlockSpec((1, Tk, w2), lambda b, h: (b, 0, h)),
                     pl.BlockSpec((1, Tk, w2), lambda b, h: (b, 0, h))]
        args += [k_lat, kv_lat]
        stat = pltpu.VMEM((2, tq, HEAD_SLOT), F32)
        scratch = [pltpu.VMEM((2, 2, tq, tk), F32), pltpu.VMEM((2, 2, tq, HEAD_SLOT), F32),
                   pltpu.VMEM((2, tq, tk), BF16), stat, stat]
    return pl.pallas_call(
        functools.partial(_attn_kernel, n_lat_tiles=n_lat_tiles, tk=tk, tq=tq),
        out_shape=jax.ShapeDtypeStruct((Bn, Tq, MLA_WIDTH), BF16),
        grid=(Bn, hp),
        in_specs=in_specs,
        out_specs=pl.BlockSpec((1, Tq, 2 * MLA_V), lambda b, h: (b, 0, h)),
        scratch_shapes=scratch,
        compiler_params=_params(("parallel", "parallel")),
        name="attn",
    )(*args)


def _s5_disc_kernel(lr_ref, li_ref, ldt_ref, br_ref, bi_ref, ar_ref, ai_ref, bbr_ref, bbi_ref):
    lr = lr_ref[0]
    li = li_ref[0]
    dt = jnp.exp(ldt_ref[0])
    mag = jnp.exp(dt * lr)
    ar = mag * jnp.cos(dt * li)
    ai = mag * jnp.sin(dt * li)
    den = lr * lr + li * li
    fr = ((ar - 1.0) * lr + ai * li) / den
    fi = (ai * lr - (ar - 1.0) * li) / den
    ar_ref[0] = ar
    ai_ref[0] = ai
    br = br_ref[...]
    bi = bi_ref[...]
    bbr_ref[0] = fr * br - fi * bi
    bbi_ref[0] = fr * bi + fi * br


def _s5_disc_call(lam_re, lam_im, log_dt, b_re, b_im):
    n = SSM_N
    lr = lam_re.reshape(2, 1, n)
    li = lam_im.reshape(2, 1, n)
    ldt = jnp.repeat(log_dt, SSM_STATE, axis=1).reshape(2, 1, n)
    br = b_re.reshape(n, SSM_GROUP).T
    bi = b_im.reshape(n, SSM_GROUP).T
    vec = pl.BlockSpec((1, 1, n), lambda d: (d, 0, 0))
    mat = pl.BlockSpec((SSM_GROUP, n), lambda d: (0, 0))
    omat = pl.BlockSpec((1, SSM_GROUP, n), lambda d: (d, 0, 0))
    return pl.pallas_call(
        _s5_disc_kernel,
        out_shape=[jax.ShapeDtypeStruct((2, 1, n), F32), jax.ShapeDtypeStruct((2, 1, n), F32),
                   jax.ShapeDtypeStruct((2, SSM_GROUP, n), F32), jax.ShapeDtypeStruct((2, SSM_GROUP, n), F32)],
        grid=(2,),
        in_specs=[vec, vec, vec, mat, mat],
        out_specs=[vec, vec, omat, omat],
        compiler_params=_params(("arbitrary",)),
        name="s5_disc",
    )(lr, li, ldt, br, bi)


def _s5_scan_kernel(uf_ref, ub_ref, bd_ref, ar_ref, ai_ref, cre_ref, cim_ref, h0_ref,
                    yf_ref, yb_ref, hfin_ref, xs_ref, h_ref, *, tl, nb):
    i = pl.program_id(0)
    n = SSM_N

    @pl.when(i == 0)
    def _():
        h_ref[...] = h0_ref[...]

    half = n // 2
    hw = SSM_WIDTH // 2
    tp = tl // S5_PARTS

    def part_start(d, part):
        return part * tp if d == 0 else tl - (part + 1) * tp

    for part in range(S5_PARTS):
        for d, u_ref in enumerate((uf_ref, ub_ref)):
            t0 = part_start(d, part)
            rows = slice(t0 * nb, (t0 + tp) * nb)
            u = u_ref[t0:t0 + tp].reshape(tp * nb, SSM_WIDTH).astype(BF16)
            for k in range(2):
                xk = _dot(u[:, k * hw:(k + 1) * hw], bd_ref[d, k])
                xs_ref[d, rows, k * half:(k + 1) * half] = xk[:, :half]
                xs_ref[d, rows, n + k * half:n + (k + 1) * half] = xk[:, half:]

    state = {}
    for d in range(2):
        for cb in range(2):
            re = slice(cb * half, (cb + 1) * half)
            im = slice(n + cb * half, n + (cb + 1) * half)
            state[d, cb] = (h_ref[d, :, re], h_ref[d, :, im], jnp.broadcast_to(ar_ref[d, :, re], (nb, half)),
                            jnp.broadcast_to(ai_ref[d, :, re], (nb, half)), re, im)
    for t in range(tl):
        for d in range(2):
            tt = t if d == 0 else tl - 1 - t
            rows = slice(tt * nb, (tt + 1) * nb)
            for cb in range(2):
                hr, hi, ar, ai, re, im = state[d, cb]
                nr = ar * hr - ai * hi + xs_ref[d, rows, re]
                ni = ar * hi + ai * hr + xs_ref[d, rows, im]
                xs_ref[d, rows, re] = nr
                xs_ref[d, rows, im] = ni
                state[d, cb] = (nr, ni, ar, ai, re, im)
    for (d, cb), (hr, hi, _, _, re, im) in state.items():
        h_ref[d, :, re] = hr
        h_ref[d, :, im] = hi

    for part in range(S5_PARTS):
        for d, y_ref in enumerate((yf_ref, yb_ref)):
            t0 = part_start(d, part)
            rows = slice(t0 * nb, (t0 + tp) * nb)
            ys = []
            for k in range(2):
                hr = xs_ref[d, rows, k * half:(k + 1) * half].astype(BF16)
                hi = xs_ref[d, rows, n + k * half:n + (k + 1) * half].astype(BF16)
                ys.append(_dot(hr, cre_ref[k]) - _dot(hi, cim_ref[k]))
            y_ref[t0:t0 + tp] = jnp.concatenate(ys, axis=1).reshape(tp, nb, SSM_WIDTH)

    @pl.when(i == pl.num_programs(0) - 1)
    def _():
        hfin_ref[...] = h_ref[...]


def _s5_scan_call(u_tb, bd, ar, ai, cre_t, cim_t, h0, tl):
    Tn, nb, W = u_tb.shape
    nc = Tn // tl
    n = SSM_N
    fwd = pl.BlockSpec((tl, nb, W), lambda i: (i, 0, 0))
    bwd = pl.BlockSpec((tl, nb, W), lambda i: (nc - 1 - i, 0, 0))
    return pl.pallas_call(
        functools.partial(_s5_scan_kernel, tl=tl, nb=nb),
        out_shape=[jax.ShapeDtypeStruct((Tn, nb, W), F32), jax.ShapeDtypeStruct((Tn, nb, W), F32),
                   jax.ShapeDtypeStruct((2, nb, 2 * n), F32)],
        grid=(nc,),
        in_specs=[fwd, bwd, _const_spec(bd.shape), _const_spec(ar.shape), _const_spec(ai.shape),
                  _const_spec(cre_t.shape), _const_spec(cim_t.shape), _const_spec(h0.shape)],
        out_specs=[fwd, bwd, pl.BlockSpec((2, nb, 2 * n), lambda i: (0, 0, 0))],
        scratch_shapes=[pltpu.VMEM((2, tl * nb, 2 * n), F32), pltpu.VMEM((2, nb, 2 * n), F32)],
        compiler_params=_params(("arbitrary",)),
        name="s5_scan",
    )(u_tb, u_tb, bd, ar, ai, cre_t, cim_t, h0)


def _s5_out_kernel(u_ref, yf_ref, yb_ref, d_ref, wglu_ref, bglu_ref, o_ref):
    y = d_ref[...] * u_ref[...] + yf_ref[...] + yb_ref[...]
    g = jax.nn.gelu(y)
    z = _dot(g.astype(BF16), wglu_ref[...]) + bglu_ref[...]
    o_ref[...] = (g * jax.nn.sigmoid(z)).astype(o_ref.dtype)


def _s5_out_call(u, yf, yb, d_skip, wglu, bglu, tr):
    R, W = u.shape
    tr = min(tr, R)
    blk = pl.BlockSpec((tr, W), lambda i: (i, 0))
    return pl.pallas_call(
        _s5_out_kernel,
        out_shape=jax.ShapeDtypeStruct((R, W), BF16),
        grid=(R // tr,),
        in_specs=[blk, blk, blk, _const_spec(d_skip.shape), _const_spec(wglu.shape), _const_spec(bglu.shape)],
        out_specs=blk,
        compiler_params=_params(("parallel",)),
        name="s5_out",
    )(u, yf, yb, d_skip, wglu, bglu)


def _split3(x):
    hi = x.astype(BF16)
    r1 = x - hi.astype(F32)
    mid = r1.astype(BF16)
    lo = (r1 - mid.astype(F32)).astype(BF16)
    return hi, mid, lo


def _gla_block(q, k, v, lg, s, d, cst, n_chunks):
    tri_ref, mask_ref, same_ref, hmask_ref, bmask_ref = cst
    L = GLA_CHUNK
    parts = _split3(lg)
    b = sum(_dot(tri_ref[d], p) for p in parts)
    tot = sum(_dot(same_ref[...], p) for p in parts)
    q_in = q * (GLA_DK ** -0.5) * jnp.exp(b)
    k_in = (k * jnp.exp(-b)).astype(BF16)
    k_dec = (k * jnp.exp(tot - b)).astype(BF16)
    vb = v.astype(BF16)
    mask = mask_ref[d] > 0.0
    outs = []
    for hd in range(GLA_HEADS):
        qh = (q_in * hmask_ref[hd]).astype(BF16)
        att = jnp.where(mask, _dot_nt(qh, k_in), 0.0).astype(BF16)
        outs.append(_dot(att, vb[:, hd * GLA_DV:(hd + 1) * GLA_DV]))
    o_intra = jnp.concatenate(outs, axis=1)
    q_in_b = q_in.astype(BF16)
    bmask = bmask_ref[...]
    o_rows = [None] * n_chunks
    for c in (range(n_chunks) if d == 0 else reversed(range(n_chunks))):
        rows = slice(c * L, (c + 1) * L)
        o_rows[c] = _dot_nt(q_in_b[rows], s.astype(BF16))
        dec = jnp.exp(tot[c * L:c * L + 1, :])
        s = dec * s + _dot_tn(vb[rows], k_dec[rows]) * bmask
    return o_intra + jnp.concatenate(o_rows, axis=0), s


def _gla_kernel(qf_ref, kf_ref, vf_ref, lgf_ref, qb_ref, kb_ref, vb_ref, lgb_ref, s0_ref,
                tri_ref, mask_ref, same_ref, hmask_ref, bmask_ref,
                of_ref, ob_ref, sfin_ref, s_ref, *, n_chunks):
    i = pl.program_id(1)
    cst = (tri_ref, mask_ref, same_ref, hmask_ref, bmask_ref)

    @pl.when(i == 0)
    def _():
        s_ref[...] = s0_ref[...]

    for j in range(qf_ref.shape[0]):
        o, s = _gla_block(qf_ref[j], kf_ref[j], vf_ref[j], lgf_ref[j], s_ref[j, 0], 0, cst, n_chunks)
        of_ref[j] = o
        s_ref[j, 0] = s
        o, s = _gla_block(qb_ref[j], kb_ref[j], vb_ref[j], lgb_ref[j], s_ref[j, 1], 1, cst, n_chunks)
        ob_ref[j] = o
        s_ref[j, 1] = s

    @pl.when(i == pl.num_programs(1) - 1)
    def _():
        sfin_ref[...] = s_ref[...]


def _gla_consts(tg):
    t = jnp.arange(tg)
    same = (t[:, None] // GLA_CHUNK) == (t[None, :] // GLA_CHUNK)
    low = same & (t[None, :] <= t[:, None])
    upp = same & (t[None, :] >= t[:, None])
    tri = jnp.stack([low, upp])
    hmask = (jnp.arange(GLA_KW)[None, None, :] // GLA_DK) == jnp.arange(GLA_HEADS)[:, None, None]
    bmask = (jnp.arange(GLA_WIDTH)[:, None] // GLA_DV) == (jnp.arange(GLA_KW)[None, :] // GLA_DK)
    return tri.astype(BF16), tri.astype(F32), same.astype(BF16), hmask.astype(F32), bmask.astype(F32)


def _gla_call(q, k, v, lgf, lgb, s0, tg):
    Bn, Tn, _ = q.shape
    nblk = Tn // tg
    consts = _gla_consts(tg)
    bb = GLA_BATCH if Bn % GLA_BATCH == 0 else 1
    fwd = lambda w: pl.BlockSpec((bb, tg, w), lambda b, i: (b, i, 0))
    bwd = lambda w: pl.BlockSpec((bb, tg, w), lambda b, i: (b, nblk - 1 - i, 0))
    st = pl.BlockSpec((bb, 2, GLA_WIDTH, GLA_KW), lambda b, i: (b, 0, 0, 0))
    return pl.pallas_call(
        functools.partial(_gla_kernel, n_chunks=tg // GLA_CHUNK),
        out_shape=[jax.ShapeDtypeStruct((Bn, Tn, GLA_WIDTH), F32), jax.ShapeDtypeStruct((Bn, Tn, GLA_WIDTH), F32),
                   jax.ShapeDtypeStruct((Bn, 2, GLA_WIDTH, GLA_KW), F32)],
        grid=(Bn // bb, nblk),
        in_specs=[fwd(GLA_KW), fwd(GLA_KW), fwd(GLA_WIDTH), fwd(GLA_KW),
                  bwd(GLA_KW), bwd(GLA_KW), bwd(GLA_WIDTH), bwd(GLA_KW), st]
                 + [_const_spec(a.shape) for a in consts],
        out_specs=[fwd(GLA_WIDTH), bwd(GLA_WIDTH), st],
        scratch_shapes=[pltpu.VMEM((bb, 2, GLA_WIDTH, GLA_KW), F32)],
        compiler_params=_params(("parallel", "arbitrary")),
        name="gla",
    )(q, k, v, lgf, q, k, v, lgb, s0, *consts)


def _merge_kernel(x_ref, oa_ref, ys_ref, of_ref, ob_ref, sh_ref, sc_ref, g1_ref, n1_ref,
                  wgate_ref, wr_ref, wmo_ref, wso_ref, gn_ref, wgo_ref, wout_ref, o_ref):
    x = x_ref[0]
    h = _rms(x) * n1_ref[...]
    hb = (h * (1.0 + sc_ref[0]) + sh_ref[0]).astype(BF16)
    ya = _dot(oa_ref[0], wmo_ref[...])
    yb = _dot(ys_ref[0], wso_ref[...])
    o = of_ref[0] + ob_ref[0]
    on = jnp.concatenate([_rms(o[:, hd * GLA_DV:(hd + 1) * GLA_DV]) for hd in range(GLA_HEADS)], axis=1)
    r = _dot(hb, wr_ref[...])
    yg = _dot(((on * gn_ref[...]) * (r * jax.nn.sigmoid(r))).astype(BF16), wgo_ref[...])
    m = jnp.zeros_like(ya)
    for j, br in enumerate((ya, yb, yg)):
        gate = jax.nn.sigmoid(_dot(hb, wgate_ref[:, j * D_MODEL:(j + 1) * D_MODEL]))
        m = m + gate * br
    o_ref[0] = x + g1_ref[0] * _dot(m.astype(BF16), wout_ref[...])


def _merge_call(x, oa, ys, of, ob, mod_sh, mod_sc, mod_g, mod_row, n1, w, tm):
    Bn, Tn, D = x.shape
    row = (lambda b, i: (b, 0, 0)) if mod_row is None else (lambda b, i: (mod_row, 0, 0))
    tok = lambda width: pl.BlockSpec((1, tm, width), lambda b, i: (b, i, 0))
    mrow = pl.BlockSpec((1, 1, D), row)
    weights = [w["wgate"], w["wr"], w["wmo"], w["wso"], w["gn"], w["wgo"], w["wout"]]
    return pl.pallas_call(
        _merge_kernel,
        out_shape=jax.ShapeDtypeStruct((Bn, Tn, D), F32),
        grid=(Bn, Tn // tm),
        in_specs=[tok(D), tok(MLA_WIDTH), tok(SSM_WIDTH), tok(GLA_WIDTH), tok(GLA_WIDTH),
                  mrow, mrow, mrow, _const_spec((1, D))] + [_const_spec(a.shape) for a in weights],
        out_specs=tok(D),
        compiler_params=_params(("parallel", "parallel")),
        name="merge",
    )(x, oa, ys, of, ob, mod_sh, mod_sc, mod_g, n1, *weights)


def _ffn_kernel(x_ref, sh_ref, sc_ref, g2_ref, n2_ref, wg_ref, wu_ref, wd_ref, fn_ref, o_ref, *, final):
    x = x_ref[0]
    h = _rms(x) * n2_ref[...]
    hb = (h * (1.0 + sc_ref[0]) + sh_ref[0]).astype(BF16)
    acc = jnp.zeros_like(x)
    for cs in (slice(0, FFN_SPLIT), slice(FFN_SPLIT, FFN_HIDDEN)):
        g = _dot(hb, wg_ref[:, cs])
        u = _dot(hb, wu_ref[:, cs])
        acc = acc + _dot(((g * jax.nn.sigmoid(g)) * u).astype(BF16), wd_ref[cs, :])
    y = x + g2_ref[0] * acc
    if final:
        y = _rms(y) * fn_ref[...]
    o_ref[0] = y


def _ffn_call(x, mod_sh, mod_sc, mod_g, mod_row, n2, wg, wu, wd, fn, final, tm):
    Bn, Tn, D = x.shape
    row = (lambda b, i: (b, 0, 0)) if mod_row is None else (lambda b, i: (mod_row, 0, 0))
    tok = pl.BlockSpec((1, tm, D), lambda b, i: (b, i, 0))
    mrow = pl.BlockSpec((1, 1, D), row)
    return pl.pallas_call(
        functools.partial(_ffn_kernel, final=final),
        out_shape=jax.ShapeDtypeStruct((Bn, Tn, D), F32),
        grid=(Bn, Tn // tm),
        in_specs=[tok, mrow, mrow, mrow, _const_spec((1, D)), _const_spec(wg.shape), _const_spec(wu.shape),
                  _const_spec(wd.shape), _const_spec((1, D))],
        out_specs=tok,
        compiler_params=_params(("parallel", "parallel")),
        name="ffn",
    )(x, mod_sh, mod_sc, mod_g, n2, wg, wu, wd, fn)


def _pair_swap(w):
    w2 = w.reshape(w.shape[0], -1, 2)
    return jnp.stack([-w2[..., 1], w2[..., 0]], axis=-1).reshape(w.shape)


def _layer_weights(l, w_in, mla_q_norm, mla_w_uq, mla_kv_norm, mla_w_ukv, gla_w_a2, gla_b_a2):
    wi = w_in[l]
    D = wi.shape[0]
    o = 0
    cuts = {}
    for name, sz in (("cq", MLA_RANK), ("ckv", MLA_RANK), ("kr", MLA_ROPE), ("u", SSM_WIDTH), ("gq", GLA_KW),
                     ("gk", GLA_KW), ("gv", GLA_WIDTH), ("gr", GLA_WIDTH), ("af", GLA_GATE_RANK),
                     ("ab", GLA_GATE_RANK), ("gate", 3 * D_MODEL)):
        cuts[name] = wi[:, o:o + sz]
        o += sz
    zeros = lambda n: jnp.zeros((D, n), F32)
    kr = cuts["kr"]
    pad_k = HEAD_SLOT - MLA_NOPE - MLA_ROPE
    wkr = jnp.concatenate([zeros(MLA_NOPE), kr, zeros(pad_k), zeros(MLA_NOPE), _pair_swap(kr), zeros(pad_k)], axis=1)
    wlr = jnp.concatenate([cuts["af"], cuts["ab"], zeros(LR_PAD - 2 * GLA_GATE_RANK)], axis=1)
    uq = mla_w_uq[l].reshape(MLA_RANK, MLA_HEADS, MLA_NOPE + MLA_ROPE)
    zq = jnp.zeros((MLA_RANK, MLA_HEADS, pad_k), F32)
    wuq = jnp.concatenate([uq, zq], axis=2).reshape(MLA_RANK, MLA_HEADS * HEAD_SLOT)
    uq_sw = _pair_swap(uq[..., MLA_NOPE:].reshape(MLA_RANK, -1)).reshape(MLA_RANK, MLA_HEADS, MLA_ROPE)
    wuqs = jnp.concatenate([jnp.zeros((MLA_RANK, MLA_HEADS, MLA_NOPE), F32), uq_sw, zq], axis=2)
    wuqs = wuqs.reshape(MLA_RANK, MLA_HEADS * HEAD_SLOT)
    wa2 = jnp.zeros((2, LR_PAD, GLA_KW), F32)
    wa2 = wa2.at[0, :GLA_GATE_RANK].set(gla_w_a2[l, 0]).at[1, GLA_GATE_RANK:2 * GLA_GATE_RANK].set(gla_w_a2[l, 1])
    b16 = lambda a: a.astype(BF16)
    return dict(
        wa=b16(jnp.concatenate([cuts["cq"], cuts["ckv"]], axis=1)), wkr=b16(wkr), wu=b16(cuts["u"]),
        wg=b16(jnp.concatenate([cuts["gq"], cuts["gk"], cuts["gv"]], axis=1)), wlr=b16(wlr),
        qn=mla_q_norm[l].reshape(1, -1), kvn=mla_kv_norm[l].reshape(1, -1),
        wuq=b16(wuq), wuqs=b16(wuqs), wukv=b16(mla_w_ukv[l]), wa2=b16(wa2),
        ba2=gla_b_a2[l].reshape(2, 1, GLA_KW), wr=b16(cuts["gr"]), wgate=b16(cuts["gate"]))


def _rope_tables(T):
    t = jnp.arange(T, dtype=jnp.int32)
    rows = (t // GRID_W).astype(F32)
    cols = (t % GRID_W).astype(F32)
    inv = ROPE_BASE ** (-jnp.arange(ROPE_FREQS, dtype=F32) / ROPE_FREQS)
    ang = jnp.concatenate([rows[:, None] * inv, cols[:, None] * inv], axis=-1)
    ang2 = jnp.repeat(ang, 2, axis=1)
    pad_k = HEAD_SLOT - MLA_NOPE - MLA_ROPE
    cos = jnp.concatenate([jnp.ones((T, MLA_NOPE), F32), jnp.cos(ang2), jnp.ones((T, pad_k), F32)], axis=1)
    sin = jnp.concatenate([jnp.zeros((T, MLA_NOPE), F32), jnp.sin(ang2), jnp.zeros((T, pad_k), F32)], axis=1)
    return cos, sin


def _s5_matrices(bbr, bbi, c_re, c_im):
    n = SSM_N
    grp_row = jnp.arange(SSM_WIDTH)[:, None] // SSM_GROUP
    grp_col = jnp.arange(n)[None, :] // SSM_STATE
    blk = grp_row == grp_col
    tile = lambda m: jnp.where(blk, jnp.tile(m, (SSM_GROUPS, 1)), 0.0)
    hw, hn = SSM_WIDTH // 2, n // 2
    halves = lambda m: [m[k * hw:(k + 1) * hw, k * hn:(k + 1) * hn] for k in range(2)]
    bd = jnp.stack([jnp.stack([jnp.concatenate([r, i], axis=1)
                               for r, i in zip(halves(tile(bbr[d])), halves(tile(bbi[d])))])
                    for d in range(2)]).astype(BF16)

    def readout(c):
        ct = jnp.transpose(c, (0, 2, 1)).reshape(n, SSM_GROUP)
        full = jnp.where(blk.T, jnp.tile(ct, (1, SSM_GROUPS)), 0.0)
        return jnp.stack([full[k * hn:(k + 1) * hn, k * hw:(k + 1) * hw] for k in range(2)]).astype(BF16)

    return bd, readout(c_re), readout(c_im)


def kernel(x, c, ctx, c_ctx, w_mod, b_mod, norm1, norm2, w_in, mla_q_norm, mla_w_uq, mla_kv_norm, mla_w_ukv, mla_w_o, ssm_lam_re, ssm_lam_im, ssm_log_dt, ssm_b_re, ssm_b_im, ssm_c_re, ssm_c_im, ssm_d, ssm_w_glu, ssm_b_glu, ssm_w_o, gla_w_a2, gla_b_a2, gla_norm, gla_w_o, w_out, ffn_w_up, ffn_w_down, final_norm):
    B, T, D = x.shape
    C = ctx.shape[1]
    depth = w_mod.shape[0]
    n_rows = -(-(B + 1) // 8) * 8
    cvec = jnp.zeros((n_rows, D), F32).at[:B].set(c).at[B].set(c_ctx)
    mod = _mod_call(cvec, w_mod.astype(BF16), b_mod)
    cos_l, sin_l = _rope_tables(T)
    cos_c = jnp.ones((C, HEAD_SLOT), F32)
    sin_c = jnp.zeros((C, HEAD_SLOT), F32)
    tm_in, tm_merge, tm_ffn = min(TM_IN, T), min(TM_MERGE, T), min(TM_FFN, T)
    tq, tk = min(TQ, T), min(TK, T // 2)
    tg = min(GLA_BLOCK, T)
    b16 = lambda a: a.astype(BF16)

    xc = ctx
    for l in range(depth):
        with_ctx = l < depth - 1
        m6 = [mod[l, :, j * D:(j + 1) * D].reshape(n_rows, 1, D) for j in range(6)]
        sh1, sc1, g1, sh2, sc2, g2 = m6
        n1 = norm1[l].reshape(1, D)
        n2 = norm2[l].reshape(1, D)
        w = _layer_weights(l, w_in, mla_q_norm, mla_w_uq, mla_kv_norm, mla_w_ukv, gla_w_a2, gla_b_a2)
        w.update(wmo=b16(mla_w_o[l]), wso=b16(ssm_w_o[l]), gn=gla_norm[l].reshape(1, -1), wgo=b16(gla_w_o[l]),
                 wout=b16(w_out[l]))
        up = ffn_w_up[l]
        wup_g, wup_u, wdown = b16(up[:, :FFN_HIDDEN]), b16(up[:, FFN_HIDDEN:]), b16(ffn_w_down[l])
        fn = final_norm.reshape(1, D)

        qc, kc, kvc, uc, gqc, gkc, gvc, lgfc, lgbc = _in_proj_call(xc, sh1, sc1, B, n1, cos_c, sin_c, w, C)
        ql, kl, kvl, ul, gql, gkl, gvl, lgfl, lgbl = _in_proj_call(x, sh1, sc1, None, n1, cos_l, sin_l, w, tm_in)

        oa = _attn_call(ql, kc, kvc, kl, kvl, tq, tk)
        oac = _attn_call(qc, kc, kvc, None, None, C, tk) if with_ctx else None

        ar, ai, bbr, bbi = _s5_disc_call(ssm_lam_re[l], ssm_lam_im[l], ssm_log_dt[l], ssm_b_re[l], ssm_b_im[l])
        bd, cre_t, cim_t = _s5_matrices(bbr, bbi, ssm_c_re[l], ssm_c_im[l])
        uc_tb = jnp.transpose(uc, (1, 0, 2))
        ul_tb = jnp.transpose(ul, (1, 0, 2))
        h0 = jnp.zeros((2, B, 2 * SSM_N), F32)
        yfc, ybc, hc = _s5_scan_call(uc_tb, bd, ar, ai, cre_t, cim_t, h0, S5_TL)
        yfl, ybl, _ = _s5_scan_call(ul_tb, bd, ar, ai, cre_t, cim_t, hc, S5_TL)
        d_skip = ssm_d[l].reshape(1, -1)
        wglu, bglu = b16(ssm_w_glu[l]), ssm_b_glu[l].reshape(1, -1)
        flat = lambda a: a.reshape(-1, SSM_WIDTH)
        ysl = _s5_out_call(flat(ul_tb), flat(yfl), flat(ybl), d_skip, wglu, bglu, S5_ROWS_OUT)
        ysl = jnp.transpose(ysl.reshape(T, B, SSM_WIDTH), (1, 0, 2))

        s0 = jnp.zeros((B, 2, GLA_WIDTH, GLA_KW), F32)
        ofc, obc, sc_fin = _gla_call(gqc, gkc, gvc, lgfc, lgbc, s0, min(GLA_BLOCK, C))
        ofl, obl, _ = _gla_call(gql, gkl, gvl, lgfl, lgbl, sc_fin, tg)

        x = _merge_call(x, oa, ysl, ofl, obl, sh1, sc1, g1, None, n1, w, tm_merge)
        x = _ffn_call(x, sh2, sc2, g2, None, n2, wup_g, wup_u, wdown, fn, l == depth - 1, tm_ffn)
        if with_ctx:
            ysc = _s5_out_call(flat(uc_tb), flat(yfc), flat(ybc), d_skip, wglu, bglu, S5_ROWS_OUT)
            ysc = jnp.transpose(ysc.reshape(C, B, SSM_WIDTH), (1, 0, 2))
            xc = _merge_call(xc, oac, ysc, ofc, obc, sh1, sc1, g1, B, n1, w, C)
            xc = _ffn_call(xc, sh2, sc2, g2, B, n2, wup_g, wup_u, wdown, fn, False, C)
    return x
```
